```python
import jax, jax.numpy as jnp
from jax import lax
import numpy as np

D_MODEL = 1024
BATCH = 4
SEQ = 8192
DEPTH = 4
DEC_BATCH = 16
DEC_SEQ = 16
PAST_LEN = 1024

CHUNK = 64
BRANCH_WIDTH = 512
N_BRANCH = 3
RET_HEADS = 4
RET_DK = 128
RET_DV = 128
GLA_HEADS = 4
GLA_DK = 128
GLA_DV = 128
GLA_RANK = 16
GLA_TAU = 16.0
ATT_HEADS = 8
ATT_HD = 64
BAND_CHUNKS = 8
BAND_PAST = 512
MAX_REL = 128
N_GROUPS = 4
EXPERTS_PER_GROUP = 4
N_EXPERTS = 16
TOP_K = 2
D_EXPERT = 512
ROPE_BASE = 10000.0
EPS = 1e-6
GN_EPS = 1e-5
IN_WIDTHS = (512, 512, 512, 512, 512, 512, 512, 512, 16, 512, 512, 512)
IN_COLS = 5648

kernel_name = 'hybrid_streaming_encoder_step'


def rmsnorm(x, g):
    xf = x.astype(jnp.float32)
    return xf * lax.rsqrt(jnp.mean(xf * xf, axis=-1, keepdims=True) + EPS) * g.astype(jnp.float32)


def head_groupnorm(x, g):
    mu = jnp.mean(x, axis=-1, keepdims=True)
    xc = x - mu
    var = jnp.mean(xc * xc, axis=-1, keepdims=True)
    return xc * lax.rsqrt(var + GN_EPS) * g.astype(jnp.float32)


def rope(x, pos):
    half = x.shape[-1] // 2
    inv_freq = ROPE_BASE ** (-jnp.arange(half, dtype=jnp.float32) / half)
    ang = pos[:, None] * inv_freq[None, :]
    cos = jnp.cos(ang)[None, :, None, :]
    sin = jnp.sin(ang)[None, :, None, :]
    x1, x2 = x[..., :half], x[..., half:]
    return jnp.concatenate([x1 * cos - x2 * sin, x1 * sin + x2 * cos], axis=-1)


def to_chunks(a, block):
    b, t, h, d = a.shape
    return a.reshape(b, t // block, block, h, d).transpose(1, 0, 3, 2, 4)


def from_chunks(a):
    nc, b, h, block, d = a.shape
    return a.transpose(1, 0, 3, 2, 4).reshape(b, nc * block, h, d)


def retention(q, k, v, state0, log_gamma):
    block = min(q.shape[1], CHUNK)
    idx = jnp.arange(block, dtype=jnp.float32)
    diff = idx[:, None] - idx[None, :]
    decay_mask = jnp.where(diff >= 0, jnp.exp(log_gamma[:, None, None] * jnp.maximum(diff, 0.0)), 0.0)
    q_decay = jnp.exp(log_gamma[:, None] * (idx + 1.0))[:, :, None]
    k_decay = jnp.exp(log_gamma[:, None] * (block - 1.0 - idx))[:, :, None]
    block_decay = jnp.exp(log_gamma * block)[:, None, None]

    def step(state, qkv):
        qc, kc, vc = qkv
        scores = jnp.einsum('bhld,bhmd->bhlm', qc, kc) * decay_mask
        out = (jnp.einsum('bhlm,bhme->bhle', scores, vc)
               + jnp.einsum('bhld,bhde->bhle', qc * q_decay, state))
        state = state * block_decay + jnp.einsum('bhmd,bhme->bhde', kc * k_decay, vc)
        return state, out

    state, out = lax.scan(step, state0.astype(jnp.float32),
                          (to_chunks(q, block), to_chunks(k, block), to_chunks(v, block)))
    return from_chunks(out), state


def gla(q, k, v, log_a, state0):
    block = min(q.shape[1], CHUNK)
    causal = jnp.tril(jnp.ones((block, block), dtype=bool))

    def step(state, inp):
        qc, kc, vc, lac = inp
        b = jnp.cumsum(lac, axis=2)
        rel = jnp.where(causal[:, :, None], b[:, :, :, None, :] - b[:, :, None, :, :], -jnp.inf)
        scores = jnp.sum(qc[:, :, :, None, :] * kc[:, :, None, :, :] * jnp.exp(rel), axis=-1)
        out = (jnp.einsum('bhts,bhse->bhte', scores, vc)
               + jnp.einsum('bhtd,bhde->bhte', qc * jnp.exp(b), state))
        b_last = b[:, :, -1:, :]
        state = (state * jnp.exp(b_last[:, :, 0, :])[..., None]
                 + jnp.einsum('bhsd,bhse->bhde', kc * jnp.exp(b_last - b), vc))
        return state, out

    state, out = lax.scan(step, state0.astype(jnp.float32),
                          (to_chunks(q, block), to_chunks(k, block), to_chunks(v, block), to_chunks(log_a, block)))
    return from_chunks(out), state


def band_attention(q, k, v, k_past, v_past, rel_bias):
    bsz, t = q.shape[0], q.shape[1]
    p = k_past.shape[1]
    block = min(t, CHUNK)
    n_blocks = t // block
    pad = BAND_PAST - p
    zeros = jnp.zeros((bsz, pad, ATT_HEADS, ATT_HD), jnp.float32)
    kp = jnp.concatenate([zeros, k_past.astype(jnp.float32), k], axis=1)
    vp = jnp.concatenate([zeros, v_past.astype(jnp.float32), v], axis=1)
    n_keys = BAND_PAST + block
    qi = jnp.arange(block)
    kj = jnp.arange(n_keys)
    rel = (BAND_PAST + qi[:, None]) - kj[None, :]
    bias = rel_bias.astype(jnp.float32)[:, jnp.clip(rel, -MAX_REL, MAX_REL) + MAX_REL]
    scale = ATT_HD ** -0.5

    def one_block(c):
        qb = lax.dynamic_slice_in_dim(q, c * block, block, axis=1)
        kb = lax.dynamic_slice_in_dim(kp, c * block, n_keys, axis=1)
        vb = lax.dynamic_slice_in_dim(vp, c * block, n_keys, axis=1)
        valid = (c * block + kj) >= pad
        s = jnp.einsum('bqhd,bkhd->bhqk', qb, kb) * scale + bias[None]
        s = jnp.where(valid[None, None, None, :], s, -jnp.inf)
        probs = jax.nn.softmax(s, axis=-1)
        return jnp.einsum('bhqk,bkhd->bqhd', probs, vb)

    out = lax.map(one_block, jnp.arange(n_blocks))
    return out.transpose(1, 0, 2, 3, 4).reshape(bsz, t, ATT_HEADS, ATT_HD)


def mixer(h, pos0, ret0, gla0, k_past, v_past, lp):
    bsz, t, _ = h.shape
    proj = (h @ lp['w_in']).astype(jnp.float32)
    cuts = [int(c) for c in np.cumsum(IN_WIDTHS)[:-1]]
    rq, rk, rv, rg, gq, gk, gv, gr, ga, aq, ak, av = jnp.split(proj, cuts, axis=-1)

    def heads(a, n):
        return a.reshape(bsz, t, n, -1)

    pos = pos0 + jnp.arange(t, dtype=jnp.float32)
    log_gamma = jnp.log(1.0 - jnp.exp2(-5.0 - jnp.arange(RET_HEADS, dtype=jnp.float32)))
    q_r = rope(heads(rq, RET_HEADS), pos)
    k_r = rope(heads(rk, RET_HEADS), pos) * RET_DK ** -0.5
    o_r, ret_new = retention(q_r, k_r, heads(rv, RET_HEADS), ret0, log_gamma)
    o_r = head_groupnorm(o_r, lp['g_ret_gn'].reshape(RET_HEADS, RET_DV)).reshape(bsz, t, BRANCH_WIDTH)
    o_r = jax.nn.silu(rg) * o_r
    log_a = jax.nn.log_sigmoid(ga @ lp['gla_a2'].astype(jnp.float32) + lp['gla_a_bias'].astype(jnp.float32)) / GLA_TAU
    o_g, gla_new = gla(heads(gq, GLA_HEADS) * GLA_DK ** -0.5, heads(gk, GLA_HEADS), heads(gv, GLA_HEADS),
                       heads(log_a, GLA_HEADS), gla0)
    g_gla = lp['g_gla_gn'].reshape(GLA_HEADS, GLA_DV).astype(jnp.float32)
    o_g = o_g * lax.rsqrt(jnp.mean(o_g * o_g, axis=-1, keepdims=True) + EPS) * g_gla
    o_g = jax.nn.silu(gr) * o_g.reshape(bsz, t, BRANCH_WIDTH)
    q_a = heads(aq, ATT_HEADS)
    k_a = heads(ak, ATT_HEADS)
    q_a = q_a * lax.rsqrt(jnp.mean(q_a * q_a, axis=-1, keepdims=True) + EPS) * lp['g_q_att'].astype(jnp.float32)
    k_a = k_a * lax.rsqrt(jnp.mean(k_a * k_a, axis=-1, keepdims=True) + EPS) * lp['g_k_att'].astype(jnp.float32)
    v_a = heads(av, ATT_HEADS)
    o_a = band_attention(q_a, k_a, v_a, k_past, v_past, lp['rel_bias']).reshape(bsz, t, BRANCH_WIDTH)
    branches = jnp.stack([o_r, o_g, o_a], axis=2)
    y = jnp.einsum('btnc,ncd->btnd', branches, lp['w_branch'].astype(jnp.float32))
    gates = jax.nn.sigmoid(h @ lp['w_gate'] + lp['b_gate']).reshape(bsz, t, N_BRANCH, D_MODEL)
    merged = jnp.sum(gates * y, axis=2)
    return merged @ lp['w_out'], (ret_new, gla_new, k_a, v_a)


def hier_moe(h, lp):
    bsz, t, d = h.shape
    hf = h.reshape(bsz * t, d)
    g_logits = (hf @ lp['w_router_group'] + lp['b_router_group']).astype(jnp.float32)
    g_prob = jax.nn.softmax(g_logits, axis=-1)
    p_group, grp = lax.top_k(g_prob, 1)
    e_logits = (hf @ lp['w_router_exp'] + lp['b_router_exp']).astype(jnp.float32)
    e_logits = e_logits.reshape(-1, N_GROUPS, EXPERTS_PER_GROUP)
    e_sel = jnp.take_along_axis(e_logits, grp[:, :, None], axis=1)[:, 0]
    top_v, top_i = lax.top_k(e_sel, TOP_K)
    w = jax.nn.softmax(top_v, axis=-1) * p_group
    expert_id = grp * EXPERTS_PER_GROUP + top_i
    combine = jnp.einsum('nk,nke->ne', w, jax.nn.one_hot(expert_id, N_EXPERTS, dtype=jnp.float32))
    y = jnp.zeros(hf.shape, jnp.float32)
    for e in range(N_EXPERTS):
        a = jax.nn.silu(hf @ lp['w_exp_gate'][e]) * (hf @ lp['w_exp_up'][e])
        y = y + combine[:, e:e + 1] * (a @ lp['w_exp_down'][e])
    return y.reshape(bsz, t, d)


def layer(x, c, pos0, ret0, gla0, k_past, v_past, lp):
    mod = jax.nn.silu(c.astype(jnp.float32)) @ lp['w_ada'] + lp['b_ada']
    sh_m, sc_m, gt_m, sh_f, sc_f, gt_f = jnp.split(mod[:, None, :], 6, axis=-1)
    h = rmsnorm(x, lp['g_mix']) * (1.0 + sc_m) + sh_m
    mix, states = mixer(h, pos0, ret0, gla0, k_past, v_past, lp)
    x = (x + gt_m * mix).astype(x.dtype)
    h = rmsnorm(x, lp['g_ffn']) * (1.0 + sc_f) + sh_f
    x = (x + gt_f * hier_moe(h, lp)).astype(x.dtype)
    return x, states


def setup_inputs(seed: int = 0) -> dict:
    key = jax.random.key(seed)
    ks = jax.random.split(key, 34)
    f32 = jnp.float32

    def nrm(k, shape, scale):
        return jax.random.normal(k, shape, f32) * scale

    att_rows = min(BAND_PAST, PAST_LEN)
    return {
        'x_prompt': nrm(ks[0], (BATCH, SEQ, D_MODEL), 1.0),
        'x_sample': nrm(ks[1], (DEC_BATCH, DEC_SEQ, D_MODEL), 1.0),
        'state_ret': nrm(ks[2], (DEPTH, DEC_BATCH, RET_HEADS, RET_DK, RET_DV), 0.5),
        'state_gla': nrm(ks[3], (DEPTH, DEC_BATCH, GLA_HEADS, GLA_DK, GLA_DV), 0.5),
        'cache_att_k': nrm(ks[4], (DEPTH, DEC_BATCH, att_rows, ATT_HEADS, ATT_HD), 1.0),
        'cache_att_v': nrm(ks[5], (DEPTH, DEC_BATCH, att_rows, ATT_HEADS, ATT_HD), 1.0),
        'c_prompt': nrm(ks[6], (BATCH, D_MODEL), 1.0),
        'c_sample': nrm(ks[7], (DEC_BATCH, D_MODEL), 1.0),
        'w_ada': nrm(ks[8], (DEPTH, D_MODEL, 6 * D_MODEL), 0.5 * D_MODEL ** -0.5),
        'b_ada': nrm(ks[9], (DEPTH, 6 * D_MODEL), 0.02),
        'g_mix': 1.0 + nrm(ks[10], (DEPTH, D_MODEL), 0.02),
        'w_in': nrm(ks[11], (DEPTH, D_MODEL, IN_COLS), D_MODEL ** -0.5),
        'gla_a2': nrm(ks[12], (DEPTH, GLA_RANK, BRANCH_WIDTH), GLA_RANK ** -0.5),
        'gla_a_bias': nrm(ks[13], (DEPTH, BRANCH_WIDTH), 0.1),
        'g_ret_gn': 1.0 + nrm(ks[14], (DEPTH, BRANCH_WIDTH), 0.02),
        'g_gla_gn': 1.0 + nrm(ks[15], (DEPTH, BRANCH_WIDTH), 0.02),
        'g_q_att': 1.0 + nrm(ks[16], (DEPTH, ATT_HD), 0.02),
        'g_k_att': 1.0 + nrm(ks[17], (DEPTH, ATT_HD), 0.02),
        'rel_bias': nrm(ks[18], (DEPTH, ATT_HEADS, 2 * MAX_REL + 1), 0.1),
        'w_branch': nrm(ks[19], (DEPTH, N_BRANCH, BRANCH_WIDTH, D_MODEL), BRANCH_WIDTH ** -0.5),
        'w_gate': nrm(ks[20], (DEPTH, D_MODEL, N_BRANCH * D_MODEL), D_MODEL ** -0.5),
        'b_gate': nrm(ks[21], (DEPTH, N_BRANCH * D_MODEL), 0.02),
        'w_out': nrm(ks[22], (DEPTH, D_MODEL, D_MODEL), D_MODEL ** -0.5),
        'g_ffn': 1.0 + nrm(ks[23], (DEPTH, D_MODEL), 0.02),
        'w_router_group': nrm(ks[24], (DEPTH, D_MODEL, N_GROUPS), D_MODEL ** -0.5),
        'b_router_group': nrm(ks[25], (DEPTH, N_GROUPS), 0.01),
        'w_router_exp': nrm(ks[26], (DEPTH, D_MODEL, N_EXPERTS), D_MODEL ** -0.5),
        'b_router_exp': nrm(ks[27], (DEPTH, N_EXPERTS), 0.01),
        'w_exp_gate': nrm(ks[28], (DEPTH, N_EXPERTS, D_MODEL, D_EXPERT), D_MODEL ** -0.5),
        'w_exp_up': nrm(ks[29], (DEPTH, N_EXPERTS, D_MODEL, D_EXPERT), D_MODEL ** -0.5),
        'w_exp_down': nrm(ks[30], (DEPTH, N_EXPERTS, D_EXPERT, D_MODEL), D_EXPERT ** -0.5),
    }


def reference(x_prompt, x_sample, state_ret, state_gla, cache_att_k, cache_att_v, c_prompt, c_sample,
              w_ada, b_ada, g_mix, w_in, gla_a2, gla_a_bias, g_ret_gn, g_gla_gn, g_q_att, g_k_att, rel_bias,
              w_branch, w_gate, b_gate, w_out, g_ffn, w_router_group, b_router_group, w_router_exp,
              b_router_exp, w_exp_gate, w_exp_up, w_exp_down):
    bp = x_prompt.shape[0]
    xp, xs = x_prompt, x_sample
    ret_p, gla_p, k_p, v_p = [], [], [], []
    ret_s, gla_s, k_s, v_s = [], [], [], []
    zero_ret = jnp.zeros((bp, RET_HEADS, RET_DK, RET_DV), jnp.float32)
    zero_gla = jnp.zeros((bp, GLA_HEADS, GLA_DK, GLA_DV), jnp.float32)
    empty_kv = jnp.zeros((bp, 0, ATT_HEADS, ATT_HD), jnp.float32)
    for l in range(DEPTH):
        lp = {
            'w_ada': w_ada[l], 'b_ada': b_ada[l], 'g_mix': g_mix[l], 'w_in': w_in[l],
            'gla_a2': gla_a2[l], 'gla_a_bias': gla_a_bias[l], 'g_ret_gn': g_ret_gn[l], 'g_gla_gn': g_gla_gn[l],
            'g_q_att': g_q_att[l], 'g_k_att': g_k_att[l], 'rel_bias': rel_bias[l], 'w_branch': w_branch[l],
            'w_gate': w_gate[l], 'b_gate': b_gate[l], 'w_out': w_out[l], 'g_ffn': g_ffn[l],
            'w_router_group': w_router_group[l], 'b_router_group': b_router_group[l],
            'w_router_exp': w_router_exp[l], 'b_router_exp': b_router_exp[l],
            'w_exp_gate': w_exp_gate[l], 'w_exp_up': w_exp_up[l], 'w_exp_down': w_exp_down[l],
        }
        xp, (rp, sp, kp, vp) = layer(xp, c_prompt, 0, zero_ret, zero_gla, empty_kv, empty_kv, lp)
        ret_p.append(rp)
        gla_p.append(sp)
        k_p.append(kp[:, -BAND_PAST:])
        v_p.append(vp[:, -BAND_PAST:])
        xs, (rs, ss, ks_, vs_) = layer(xs, c_sample, PAST_LEN, state_ret[l], state_gla[l],
                                       cache_att_k[l], cache_att_v[l], lp)
        ret_s.append(rs.astype(state_ret.dtype))
        gla_s.append(ss.astype(state_gla.dtype))
        k_s.append(ks_.astype(cache_att_k.dtype))
        v_s.append(vs_.astype(cache_att_v.dtype))
    return (xp, xs, jnp.stack(ret_p), jnp.stack(gla_p), jnp.stack(k_p), jnp.stack(v_p),
            jnp.stack(ret_s), jnp.stack(gla_s), jnp.stack(k_s), jnp.stack(v_s))
```

```python
import functools

import numpy as np
import jax
import jax.numpy as jnp
from jax import lax
from jax.experimental import pallas as pl
from jax.experimental.pallas import tpu as pltpu

F32 = jnp.float32
BF16 = jnp.bfloat16

D_MODEL = 1024
CHUNK = 64
BRANCH_WIDTH = 512
N_BRANCH = 3
RET_HEADS = 4
HEAD_DK = 128
GLA_RANK = 16
GLA_TAU = 16.0
ATT_HEADS = 8
ATT_HD = 64
BAND_PAST = 512
MAX_REL = 128
N_GROUPS = 4
EXPERTS_PER_GROUP = 4
N_EXPERTS = 16
D_EXPERT = 512
ROPE_BASE = 10000.0
EPS = 1e-6
GN_EPS = 1e-5
PAST_LEN = 1024
N_PROJ_BLOCKS = 11
PROJ_COLS = N_PROJ_BLOCKS * BRANCH_WIDTH

V7X_LANES = 128
V7X_VMEM_BYTES = 64 * 1024 * 1024
VMEM_LIMIT = 56 * 1024 * 1024
NEG_BIG = -1e30


def _cparams(sem, vmem=None):
    return pltpu.CompilerParams(dimension_semantics=sem, vmem_limit_bytes=vmem)


def _const_spec(shape):
    nd = len(shape)
    return pl.BlockSpec(shape, lambda *_: (0,) * nd)


def _dot(a, b):
    return jnp.dot(a, b, preferred_element_type=F32)


def _dot_nt(a, b):
    return lax.dot_general(a, b, (((1,), (1,)), ((), ())), preferred_element_type=F32)


def _dot_tn(a, b):
    return lax.dot_general(a, b, (((0,), (0,)), ((), ())), preferred_element_type=F32)


def _silu(x):
    return x * jax.nn.sigmoid(x)


def _norm_mod(x, g, sc, sh):
    ms = jnp.mean(x * x, axis=-1, keepdims=True)
    return x * lax.rsqrt(ms + EPS) * g * (1.0 + sc) + sh


def _ada_kernel(c_ref, w_ref, b_ref, o_ref):
    s = _silu(c_ref[...])
    o_ref[0] = _dot(s.astype(BF16), w_ref[0].astype(BF16)) + b_ref[0]


def _ada_mod(c_all, w_ada, b_ada):
    depth, d, n = w_ada.shape
    rows = c_all.shape[0]
    bn = 1536
    return pl.pallas_call(
        _ada_kernel,
        out_shape=jax.ShapeDtypeStruct((depth, rows, n), F32),
        grid=(depth, n // bn),
        in_specs=[
            pl.BlockSpec((rows, d), lambda l, j: (0, 0)),
            pl.BlockSpec((1, d, bn), lambda l, j: (l, 0, j)),
            pl.BlockSpec((1, 1, bn), lambda l, j: (l, 0, j)),
        ],
        out_specs=pl.BlockSpec((1, rows, bn), lambda l, j: (l, 0, j)),
        compiler_params=_cparams(("arbitrary", "arbitrary"), VMEM_LIMIT),
        name="ada_mod",
    )(c_all, w_ada, b_ada.reshape(depth, 1, n))


def _rope_heads(a, cosf, sinf):
    outs = []
    for h in range(RET_HEADS):
        ah = a[:, h * HEAD_DK:(h + 1) * HEAD_DK]
        outs.append(ah * cosf + pltpu.roll(ah, HEAD_DK // 2, 1) * sinf)
    return jnp.concatenate(outs, axis=1)


def _rms_heads64(a, gain):
    low = lax.broadcasted_iota(jnp.int32, (1, V7X_LANES), 1) < ATT_HD
    outs = []
    for c in range(a.shape[1] // V7X_LANES):
        ac = a[:, c * V7X_LANES:(c + 1) * V7X_LANES]
        sq = ac * ac
        lo = jnp.sum(jnp.where(low, sq, 0.0), axis=-1, keepdims=True)
        hi = jnp.sum(jnp.where(low, 0.0, sq), axis=-1, keepdims=True)
        ms = jnp.where(low, lo, hi) * (1.0 / ATT_HD)
        outs.append(ac * lax.rsqrt(ms + EPS))
    return jnp.concatenate(outs, axis=1) * gain


def _in_proj_kernel(x_ref, sc_ref, sh_ref, g_ref, w_ref, wga_ref, a2_ref, ab_ref, cos_ref, sin_ref,
                    gq_ref, gk_ref, proj_ref, la_ref):
    x = x_ref[0]
    hb = _norm_mod(x, g_ref[...], sc_ref[0], sh_ref[0]).astype(BF16)
    cosf = cos_ref[...]
    sinf = sin_ref[...]
    for j in range(N_PROJ_BLOCKS):
        cols = slice(j * BRANCH_WIDTH, (j + 1) * BRANCH_WIDTH)
        acc = _dot(hb, w_ref[:, cols])
        if j == 0:
            acc = _rope_heads(acc, cosf, sinf)
        elif j == 1:
            acc = _rope_heads(acc, cosf, sinf) * (HEAD_DK ** -0.5)
        elif j == 4:
            acc = acc * (HEAD_DK ** -0.5)
        elif j == 8:
            acc = _rms_heads64(acc, gq_ref[...]) * (ATT_HD ** -0.5)
        elif j == 9:
            acc = _rms_heads64(acc, gk_ref[...])
        proj_ref[0, :, cols] = acc.astype(BF16)
    ga = _dot(hb, wga_ref[...])
    z = _dot(ga.astype(BF16), a2_ref[...]) + ab_ref[...]
    la_ref[0] = jax.nn.log_sigmoid(z) * (1.0 / GLA_TAU)


def _in_proj(x, sc, sh, g_mix, w_in, w_ga, a2, a_bias, cosf, sinf, gq, gk, tm):
    g, t, d = x.shape
    rm = sc.shape[1]
    mod_spec = pl.BlockSpec((1, rm, d), (lambda b, i: (b, 0, 0)) if rm == 1 else (lambda b, i: (b, i, 0)))
    return pl.pallas_call(
        _in_proj_kernel,
        out_shape=(jax.ShapeDtypeStruct((g, t, PROJ_COLS), BF16),
                   jax.ShapeDtypeStruct((g, t, BRANCH_WIDTH), F32)),
        grid=(g, t // tm),
        in_specs=[
            pl.BlockSpec((1, tm, d), lambda b, i: (b, i, 0)),
            mod_spec, mod_spec,
            _const_spec((1, d)),
            _const_spec((d, PROJ_COLS)),
            _const_spec((d, V7X_LANES)),
            _const_spec((V7X_LANES, BRANCH_WIDTH)),
            _const_spec((1, BRANCH_WIDTH)),
            pl.BlockSpec((tm, HEAD_DK), lambda b, i: (i, 0)),
            pl.BlockSpec((tm, HEAD_DK), lambda b, i: (i, 0)),
            _const_spec((1, BRANCH_WIDTH)),
            _const_spec((1, BRANCH_WIDTH)),
        ],
        out_specs=(pl.BlockSpec((1, tm, PROJ_COLS), lambda b, i: (b, i, 0)),
                   pl.BlockSpec((1, tm, BRANCH_WIDTH), lambda b, i: (b, i, 0))),
        compiler_params=_cparams(("arbitrary", "arbitrary"), VMEM_LIMIT),
        name="in_proj",
    )(x, sc, sh, g_mix, w_in, w_ga, a2, a_bias, cosf, sinf, gq, gk)


def _retention_kernel(q_ref, k_ref, v_ref, g_ref, s0_ref, dm_ref, qd_ref, kd_ref, bd_ref, gn_ref,
                      o_ref, so_ref, st_ref, *, chunk, n_chunks):
    @pl.when(pl.program_id(1) == 0)
    def _():
        st_ref[...] = s0_ref[0]

    def body(c, carry):
        rows = pl.ds(pl.multiple_of(c * chunk, chunk), chunk)
        for h in range(RET_HEADS):
            cols = slice(h * HEAD_DK, (h + 1) * HEAD_DK)
            q = q_ref[0, rows, cols]
            k = k_ref[0, rows, cols]
            v = v_ref[0, rows, cols]
            scores = _dot_nt(q, k) * dm_ref[h]
            o = _dot(scores.astype(BF16), v)
            o = o + _dot((q.astype(F32) * qd_ref[h]).astype(BF16), st_ref[h].astype(BF16))
            st_ref[h] = st_ref[h] * bd_ref[h] + _dot_tn((k.astype(F32) * kd_ref[h]).astype(BF16), v)
            mu = jnp.mean(o, axis=-1, keepdims=True)
            oc = o - mu
            var = jnp.mean(oc * oc, axis=-1, keepdims=True)
            on = oc * lax.rsqrt(var + GN_EPS) * gn_ref[:, cols]
            o_ref[0, rows, cols] = (_silu(g_ref[0, rows, cols].astype(F32)) * on).astype(BF16)
        return carry

    lax.fori_loop(0, n_chunks, body, 0)

    @pl.when(pl.program_id(1) == pl.num_programs(1) - 1)
    def _():
        so_ref[0] = st_ref[...]


def _retention_tables(chunk):
    log_gamma = jnp.log(1.0 - jnp.exp2(-5.0 - jnp.arange(RET_HEADS, dtype=F32)))
    idx = jnp.arange(chunk, dtype=F32)
    diff = idx[:, None] - idx[None, :]
    dmask = jnp.where(diff >= 0, jnp.exp(log_gamma[:, None, None] * jnp.maximum(diff, 0.0)), 0.0)
    qd = jnp.exp(log_gamma[:, None] * (idx + 1.0))[:, :, None]
    kd = jnp.exp(log_gamma[:, None] * (chunk - 1.0 - idx))[:, :, None]
    bd = jnp.exp(log_gamma * chunk)[:, None, None]
    bc = lambda a, r: jnp.broadcast_to(a, (RET_HEADS, r, HEAD_DK)).astype(F32)
    return dmask.astype(F32), bc(qd, chunk), bc(kd, chunk), bc(bd, 1)


def _retention(proj, state0, gn, chunk, tc):
    b, t, _ = proj.shape
    dm, qd, kd, bd = _retention_tables(chunk)
    col = lambda j: pl.BlockSpec((1, tc, BRANCH_WIDTH), lambda i, s, j=j: (i, s, j))
    st_spec = pl.BlockSpec((1, RET_HEADS, HEAD_DK, HEAD_DK), lambda i, s: (i, 0, 0, 0))
    return pl.pallas_call(
        functools.partial(_retention_kernel, chunk=chunk, n_chunks=tc // chunk),
        out_shape=(jax.ShapeDtypeStruct((b, t, BRANCH_WIDTH), BF16),
                   jax.ShapeDtypeStruct((b, RET_HEADS, HEAD_DK, HEAD_DK), F32)),
        grid=(b, t // tc),
        in_specs=[col(0), col(1), col(2), col(3), st_spec,
                  _const_spec(dm.shape), _const_spec(qd.shape), _const_spec(kd.shape), _const_spec(bd.shape),
                  _const_spec((1, BRANCH_WIDTH))],
        out_specs=(pl.BlockSpec((1, tc, BRANCH_WIDTH), lambda i, s: (i, s, 0)), st_spec),
        scratch_shapes=[pltpu.VMEM((RET_HEADS, HEAD_DK, HEAD_DK), F32)],
        compiler_params=_cparams(("arbitrary", "arbitrary")),
        name="retention",
    )(proj, proj, proj, proj, state0, dm, qd, kd, bd, gn)


def _gla_kernel(q_ref, k_ref, v_ref, r_ref, la_ref, s0_ref, tri_ref, gn_ref, o_ref, so_ref, st_ref,
                *, chunk, n_chunks):
    @pl.when(pl.program_id(1) == 0)
    def _():
        for h in range(RET_HEADS):
            st_ref[h] = s0_ref[0, h].T

    tri = tri_ref[...]
    causal = (lax.broadcasted_iota(jnp.int32, (chunk, chunk), 0)
              >= lax.broadcasted_iota(jnp.int32, (chunk, chunk), 1))

    def body(c, carry):
        rows = pl.ds(pl.multiple_of(c * chunk, chunk), chunk)
        la = la_ref[0, rows, :]
        la_hi = la.astype(BF16)
        la_lo = (la - la_hi.astype(F32)).astype(BF16)
        b_all = _dot(tri, la_hi) + _dot(tri, la_lo)
        for h in range(RET_HEADS):
            cols = slice(h * HEAD_DK, (h + 1) * HEAD_DK)
            q = q_ref[0, rows, cols].astype(F32)
            k = k_ref[0, rows, cols].astype(F32)
            v = v_ref[0, rows, cols]
            bh = b_all[:, cols]
            bl = bh[chunk - 1:chunk, :]
            qh = (q * jnp.exp(bh)).astype(BF16)
            kh = (k * jnp.exp(-bh)).astype(BF16)
            scores = jnp.where(causal, _dot_nt(qh, kh), 0.0)
            o = _dot(scores.astype(BF16), v) + _dot_nt(qh, st_ref[h].astype(BF16))
            kd = (k * jnp.exp(bl - bh)).astype(BF16)
            st_ref[h] = st_ref[h] * jnp.exp(bl) + _dot_tn(v, kd)
            ms = jnp.mean(o * o, axis=-1, keepdims=True)
            on = o * lax.rsqrt(ms + EPS) * gn_ref[:, cols]
            o_ref[0, rows, cols] = (_silu(r_ref[0, rows, cols].astype(F32)) * on).astype(BF16)
        return carry

    lax.fori_loop(0, n_chunks, body, 0)

    @pl.when(pl.program_id(1) == pl.num_programs(1) - 1)
    def _():
        for h in range(RET_HEADS):
            so_ref[0, h] = st_ref[h].T


def _gla(proj, log_a, state0, gn, chunk, tc):
    b, t, _ = proj.shape
    tri = jnp.tril(jnp.ones((chunk, chunk), F32)).astype(BF16)
    col = lambda j: pl.BlockSpec((1, tc, BRANCH_WIDTH), lambda i, s, j=j: (i, s, j))
    st_spec = pl.BlockSpec((1, RET_HEADS, HEAD_DK, HEAD_DK), lambda i, s: (i, 0, 0, 0))
    return pl.pallas_call(
        functools.partial(_gla_kernel, chunk=chunk, n_chunks=tc // chunk),
        out_shape=(jax.ShapeDtypeStruct((b, t, BRANCH_WIDTH), BF16),
                   jax.ShapeDtypeStruct((b, RET_HEADS, HEAD_DK, HEAD_DK), F32)),
        grid=(b, t // tc),
        in_specs=[col(4), col(5), col(6), col(7),
                  pl.BlockSpec((1, tc, BRANCH_WIDTH), lambda i, s: (i, s, 0)),
                  st_spec, _const_spec((chunk, chunk)), _const_spec((1, BRANCH_WIDTH))],
        out_specs=(pl.BlockSpec((1, tc, BRANCH_WIDTH), lambda i, s: (i, s, 0)), st_spec),
        scratch_shapes=[pltpu.VMEM((RET_HEADS, HEAD_DK, HEAD_DK), F32)],
        compiler_params=_cparams(("arbitrary", "arbitrary")),
        name="gla",
    )(proj, proj, proj, proj, log_a, state0, tri, gn)


def _band_attn_kernel(q_ref, kp_ref, kc_ref, vp_ref, vc_ref, bias_ref, o_ref, kw_ref, vw_ref,
                      *, n_sub, rb, win, prev_rows, mask_start):
    cur_rows = kc_ref.shape[1]
    kw_ref[0:prev_rows, :] = kp_ref[0].astype(BF16)
    kw_ref[prev_rows:prev_rows + cur_rows, :] = kc_ref[0]
    vw_ref[0:prev_rows, :] = vp_ref[0].astype(BF16)
    vw_ref[prev_rows:prev_rows + cur_rows, :] = vc_ref[0]
    low = lax.broadcasted_iota(jnp.int32, (1, V7X_LANES), 1) < ATT_HD
    q0 = pl.program_id(1) * cur_rows

    def body(i, carry):
        start = pl.multiple_of(i * rb, rb)
        qrows = pl.ds(start, rb)
        wrows = pl.ds(start, win)
        if mask_start:
            valid = lax.broadcasted_iota(jnp.int32, (1, win), 1) >= prev_rows - q0 - start
        for p in range(ATT_HEADS // 2):
            cols = slice(p * V7X_LANES, (p + 1) * V7X_LANES)
            q2 = q_ref[0, qrows, cols]
            k2 = kw_ref[wrows, cols]
            v2 = vw_ref[wrows, cols]
            outs = []
            for half in range(2):
                qm = jnp.where(low, q2, 0) if half == 0 else jnp.where(low, 0, q2)
                s = _dot_nt(qm.astype(BF16), k2) + bias_ref[2 * p + half]
                if mask_start:
                    s = jnp.where(valid, s, NEG_BIG)
                m = jnp.max(s, axis=-1, keepdims=True)
                e = jnp.exp(s - m)
                den = jnp.sum(e, axis=-1, keepdims=True)
                outs.append(_dot(e.astype(BF16), v2) / den)
            o_ref[0, qrows, cols] = jnp.where(low, outs[0], outs[1]).astype(BF16)
        return carry

    lax.fori_loop(0, n_sub, body, 0)


def _rel_bias_table(rel_bias, rb, win, chunk):
    r = np.arange(rb)[:, None]
    j = np.arange(win)[None, :]
    rel = BAND_PAST + r - j
    c0 = (r // chunk) * chunk
    allowed = (j >= c0) & (j < c0 + BAND_PAST + chunk)
    idx = np.clip(rel, -MAX_REL, MAX_REL) + MAX_REL
    bias = rel_bias.astype(F32)[:, idx]
    return jnp.where(jnp.asarray(allowed)[None], bias, NEG_BIG)


def _band_attn_prompt(proj, rel_bias):
    b, t, _ = proj.shape
    qb, rb = 512, 128
    win = BAND_PAST + rb
    bias = _rel_bias_table(rel_bias, rb, win, CHUNK)
    cur = lambda j: pl.BlockSpec((1, qb, BRANCH_WIDTH), lambda i, s, j=j: (i, s, j))
    prev = lambda j: pl.BlockSpec((1, qb, BRANCH_WIDTH), lambda i, s, j=j: (i, jnp.maximum(s - 1, 0), j))
    return pl.pallas_call(
        functools.partial(_band_attn_kernel, n_sub=qb // rb, rb=rb, win=win, prev_rows=BAND_PAST,
                          mask_start=True),
        out_shape=jax.ShapeDtypeStruct((b, t, BRANCH_WIDTH), BF16),
        grid=(b, t // qb),
        in_specs=[cur(8), prev(9), cur(9), prev(10), cur(10), _const_spec(bias.shape)],
        out_specs=pl.BlockSpec((1, qb, BRANCH_WIDTH), lambda i, s: (i, s, 0)),
        scratch_shapes=[pltpu.VMEM((BAND_PAST + qb, BRANCH_WIDTH), BF16),
                        pltpu.VMEM((BAND_PAST + qb, BRANCH_WIDTH), BF16)],
        compiler_params=_cparams(("arbitrary", "arbitrary")),
        name="band_attn",
    )(proj, proj, proj, proj, proj, bias)


def _band_attn_sample(proj, cache_k, cache_v, rel_bias):
    b, t, _ = proj.shape
    win = BAND_PAST + t
    bias = _rel_bias_table(rel_bias, t, win, t)
    cur = lambda j: pl.BlockSpec((1, t, BRANCH_WIDTH), lambda i, s, j=j: (i, 0, j))
    cache = pl.BlockSpec((1, BAND_PAST, BRANCH_WIDTH), lambda i, s: (i, 0, 0))
    return pl.pallas_call(
        functools.partial(_band_attn_kernel, n_sub=1, rb=t, win=win, prev_rows=BAND_PAST, mask_start=False),
        out_shape=jax.ShapeDtypeStruct((b, t, BRANCH_WIDTH), BF16),
        grid=(b, 1),
        in_specs=[cur(8), cache, cur(9), cache, cur(10), _const_spec(bias.shape)],
        out_specs=pl.BlockSpec((1, t, BRANCH_WIDTH), lambda i, s: (i, 0, 0)),
        scratch_shapes=[pltpu.VMEM((win, BRANCH_WIDTH), BF16), pltpu.VMEM((win, BRANCH_WIDTH), BF16)],
        compiler_params=_cparams(("arbitrary", "arbitrary")),
        name="band_attn_sample",
    )(proj, cache_k, proj, cache_v, proj, bias)


def _merge_kernel(x_ref, or_ref, og_ref, oa_ref, scm_ref, shm_ref, gtm_ref, scf_ref, shf_ref,
                  gmix_ref, gffn_ref, wg_ref, bg_ref, wb_ref, wo_ref, xo_ref, h2_ref):
    x = x_ref[0]
    d = x.shape[1]
    hb = _norm_mod(x, gmix_ref[...], scm_ref[0], shm_ref[0]).astype(BF16)
    merged = None
    for n, o_ref in enumerate((or_ref, og_ref, oa_ref)):
        gate = jax.nn.sigmoid(_dot(hb, wg_ref[:, n * d:(n + 1) * d]) + bg_ref[:, n * d:(n + 1) * d])
        y = gate * _dot(o_ref[0], wb_ref[n])
        merged = y if merged is None else merged + y
    mix = _dot(merged.astype(BF16), wo_ref[...])
    xn = x + gtm_ref[0] * mix
    xo_ref[0] = xn
    h2_ref[0] = _norm_mod(xn, gffn_ref[...], scf_ref[0], shf_ref[0]).astype(BF16)


def _merge(x, o_r, o_g, o_a, mods, g_mix, g_ffn, w_gate, b_gate, w_branch, w_out, tm):
    g, t, d = x.shape
    rm = mods[0].shape[1]
    mod_spec = pl.BlockSpec((1, rm, d), (lambda b, i: (b, 0, 0)) if rm == 1 else (lambda b, i: (b, i, 0)))
    tok = lambda w: pl.BlockSpec((1, tm, w), lambda b, i: (b, i, 0))
    return pl.pallas_call(
        _merge_kernel,
        out_shape=(jax.ShapeDtypeStruct((g, t, d), F32), jax.ShapeDtypeStruct((g, t, d), BF16)),
        grid=(g, t // tm),
        in_specs=[tok(d), tok(BRANCH_WIDTH), tok(BRANCH_WIDTH), tok(BRANCH_WIDTH)] + [mod_spec] * 5 + [
            _const_spec((1, d)), _const_spec((1, d)),
            _const_spec((d, N_BRANCH * d)), _const_spec((1, N_BRANCH * d)),
            _const_spec((N_BRANCH, BRANCH_WIDTH, d)), _const_spec((d, d))],
        out_specs=(tok(d), tok(d)),
        compiler_params=_cparams(("arbitrary", "arbitrary"), VMEM_LIMIT),
        name="merge",
    )(x, o_r, o_g, o_a, *mods, g_mix, g_ffn, w_gate, b_gate, w_branch, w_out)


def _first_argmax(vals, n):
    row = lax.broadcasted_iota(jnp.int32, vals.shape, 0).astype(F32)
    m = jnp.max(vals, axis=0, keepdims=True)
    idx = jnp.min(jnp.where(vals == m, row, float(n)), axis=0, keepdims=True)
    return m, idx


def _route_kernel(h_ref, wr_ref, br_ref, up_ref, l16_ref, info_ref, tbl_ref, *, rows_per_block):
    tm = h_ref.shape[0]
    lg = _dot_nt(wr_ref[...], h_ref[...]) + br_ref[...]
    g = lg[0:N_GROUPS]
    gmax, grp = _first_argmax(g, N_GROUPS)
    p_group = 1.0 / jnp.sum(jnp.exp(g - gmax), axis=0, keepdims=True)
    esel = jnp.zeros((EXPERTS_PER_GROUP, tm), F32)
    for gi in range(N_GROUPS):
        blk = lg[8 + gi * EXPERTS_PER_GROUP:8 + (gi + 1) * EXPERTS_PER_GROUP]
        esel = esel + jnp.where(grp == float(gi), blk, 0.0)
    v1, i1 = _first_argmax(esel, EXPERTS_PER_GROUP)
    row4 = lax.broadcasted_iota(jnp.int32, esel.shape, 0).astype(F32)
    v2, i2 = _first_argmax(jnp.where(row4 == i1, -jnp.inf, esel), EXPERTS_PER_GROUP)
    e21 = jnp.exp(v2 - v1)
    w1 = p_group / (1.0 + e21)
    w2 = p_group * e21 / (1.0 + e21)
    e1 = grp * float(EXPERTS_PER_GROUP) + i1
    e2 = grp * float(EXPERTS_PER_GROUP) + i2
    row16 = lax.broadcasted_iota(jnp.int32, (N_EXPERTS, tm), 0).astype(F32)
    hit1 = row16 == e1
    hit2 = row16 == e2
    onehot = jnp.where(hit1 | hit2, 1.0, 0.0)
    prefix = _dot(onehot.astype(BF16), up_ref[...])
    cnt = jnp.sum(onehot, axis=1, keepdims=True)
    nblk = jnp.floor((cnt + float(rows_per_block - 1)) * (1.0 / rows_per_block))
    nblk_b = jnp.broadcast_to(nblk, (N_EXPERTS, V7X_LANES))
    offb = _dot(l16_ref[...], nblk_b.astype(BF16))[:, 0:1]
    base = offb * float(rows_per_block) + prefix
    info_ref[0:1, :] = jnp.sum(jnp.where(hit1, base, 0.0), axis=0, keepdims=True)
    info_ref[1:2, :] = jnp.sum(jnp.where(hit2, base, 0.0), axis=0, keepdims=True)
    info_ref[2:3, :] = w1
    info_ref[3:4, :] = w2
    info_ref[4:8, :] = jnp.zeros((4, tm), F32)
    tbl_ref[0] = nblk_b.astype(jnp.int32)


def _route(h2, wr_t, br_t, tm, rows_per_block):
    n, d = h2.shape
    nt = n // tm
    upper = jnp.triu(jnp.ones((tm, tm), F32), 1).astype(BF16)
    l16 = jnp.tril(jnp.ones((N_EXPERTS, N_EXPERTS), F32), -1).astype(BF16)
    return pl.pallas_call(
        functools.partial(_route_kernel, rows_per_block=rows_per_block),
        out_shape=(jax.ShapeDtypeStruct((8, n), F32),
                   jax.ShapeDtypeStruct((nt, N_EXPERTS, V7X_LANES), jnp.int32)),
        grid=(nt,),
        in_specs=[pl.BlockSpec((tm, d), lambda i: (i, 0)),
                  _const_spec((32, d)), _const_spec((32, 1)),
                  _const_spec((tm, tm)), _const_spec((N_EXPERTS, N_EXPERTS))],
        out_specs=(pl.BlockSpec((8, tm), lambda i: (0, i)),
                   pl.BlockSpec((1, N_EXPERTS, V7X_LANES), lambda i: (i, 0, 0))),
        compiler_params=_cparams(("arbitrary",)),
        name="route",
    )(h2, wr_t, br_t, upper, l16)


def _moe_kernel(nb_ref, h_ref, x_ref, gt_ref, irow_ref, icol_ref, wg_ref, wu_ref, wd_ref, o_ref,
                xs_ref, ds_ref, *, rows_per_block, gather_rows):
    i = pl.program_id(0)
    e = pl.program_id(1)
    tm = h_ref.shape[0]
    rpb = rows_per_block
    gr = gather_rows
    first_blk = jnp.int32(0)
    total_blk = jnp.int32(0)
    for ee in range(N_EXPERTS):
        n_ee = nb_ref[i * N_EXPERTS + ee]
        first_blk = first_blk + jnp.where(ee < e, n_ee, 0)
        total_blk = total_blk + n_ee
    n_gather = (total_blk * rpb + gr - 1) // gr

    @pl.when(e == 0)
    def _():
        pos1 = irow_ref[0:1, :]
        pos2 = irow_ref[1:2, :]
        h = h_ref[...]

        def gather(gb, carry):
            r0 = pl.multiple_of(gb * gr, gr)
            srow = (lax.broadcasted_iota(jnp.int32, (gr, 1), 0) + r0).astype(F32)
            sel = jnp.where((srow == pos1) | (srow == pos2), 1.0, 0.0).astype(BF16)
            xs_ref[pl.ds(r0, gr), :] = _dot(sel, h).astype(BF16)
            ds_ref[pl.ds(r0, gr), :] = jnp.zeros((gr, ds_ref.shape[1]), BF16)
            return carry

        lax.fori_loop(0, n_gather, gather, 0)
        o_ref[0] = jnp.zeros(o_ref.shape[1:], F32)

    def expert(bi, carry):
        r0 = pl.multiple_of((first_blk + bi) * rpb, rpb)
        xb = xs_ref[pl.ds(r0, rpb), :]
        a = _silu(_dot(xb, wg_ref[0])) * _dot(xb, wu_ref[0])
        ds_ref[pl.ds(r0, rpb), :] = _dot(a.astype(BF16), wd_ref[0]).astype(BF16)
        return carry

    lax.fori_loop(0, nb_ref[i * N_EXPERTS + e], expert, 0)

    @pl.when(e == N_EXPERTS - 1)
    def _():
        pos1 = icol_ref[:, 0:1]
        pos2 = icol_ref[:, 1:2]
        w1 = icol_ref[:, 2:3]
        w2 = icol_ref[:, 3:4]

        def combine(gb, carry):
            r0 = pl.multiple_of(gb * gr, gr)
            scol = (lax.broadcasted_iota(jnp.int32, (1, gr), 1) + r0).astype(F32)
            cmb = jnp.where(scol == pos1, w1, 0.0) + jnp.where(scol == pos2, w2, 0.0)
            o_ref[0] += _dot(cmb.astype(BF16), ds_ref[pl.ds(r0, gr), :])
            return carry

        lax.fori_loop(0, n_gather, combine, 0)
        o_ref[0] = x_ref[0] + gt_ref[0] * o_ref[0]


def _moe(nblk, h2, x, gt, info_row, info_col, w_eg, w_eu, w_ed, tm, rows_per_block, gather_rows):
    g, t, d = x.shape
    tpg = t // tm
    rm = gt.shape[1]
    s_max = -(-(2 * tm + N_EXPERTS * (rows_per_block - 1)) // gather_rows) * gather_rows
    gt_spec = pl.BlockSpec((1, rm, d), (lambda i, e, nb: (i // tpg, 0, 0)) if rm == 1
                           else (lambda i, e, nb: (i // tpg, i % tpg, 0)))
    tok3 = pl.BlockSpec((1, tm, d), lambda i, e, nb: (i // tpg, i % tpg, 0))
    grid_spec = pltpu.PrefetchScalarGridSpec(
        num_scalar_prefetch=1,
        grid=(g * tpg, N_EXPERTS),
        in_specs=[
            pl.BlockSpec((tm, d), lambda i, e, nb: (i, 0)),
            tok3, gt_spec,
            pl.BlockSpec((8, tm), lambda i, e, nb: (0, i)),
            pl.BlockSpec((tm, 8), lambda i, e, nb: (i, 0)),
            pl.BlockSpec((1, d, D_EXPERT), lambda i, e, nb: (e, 0, 0)),
            pl.BlockSpec((1, d, D_EXPERT), lambda i, e, nb: (e, 0, 0)),
            pl.BlockSpec((1, D_EXPERT, d), lambda i, e, nb: (e, 0, 0)),
        ],
        out_specs=tok3,
        scratch_shapes=[pltpu.VMEM((s_max, d), BF16), pltpu.VMEM((s_max, d), BF16)],
    )
    return pl.pallas_call(
        functools.partial(_moe_kernel, rows_per_block=rows_per_block, gather_rows=gather_rows),
        out_shape=jax.ShapeDtypeStruct((g, t, d), F32),
        grid_spec=grid_spec,
        compiler_params=_cparams(("arbitrary", "arbitrary"), VMEM_LIMIT),
        name="moe",
    )(nblk, h2, x, gt, info_row, info_col, w_eg, w_eu, w_ed)


def _rope_tables(pos):
    half = HEAD_DK // 2
    inv_freq = ROPE_BASE ** (-jnp.arange(half, dtype=F32) / half)
    ang = pos[:, None] * inv_freq[None, :]
    cos, sin = jnp.cos(ang), jnp.sin(ang)
    return jnp.concatenate([cos, cos], axis=1), jnp.concatenate([-sin, sin], axis=1)


def _layer(x, mods, lw, cfg, ret0, gla0, cache_k, cache_v, cosf, sinf):
    g, t, d = x.shape
    b, tseq = cfg["b"], cfg["tseq"]
    sh_m, sc_m, gt_m, sh_f, sc_f, gt_f = mods
    proj, log_a = _in_proj(x, sc_m, sh_m, lw["g_mix"], lw["w_in"], lw["w_ga"], lw["a2"], lw["a_bias"],
                           cosf, sinf, lw["gq"], lw["gk"], cfg["tm"])
    proj_s = proj.reshape(b, tseq, PROJ_COLS)
    log_a_s = log_a.reshape(b, tseq, BRANCH_WIDTH)
    o_r, ret_new = _retention(proj_s, ret0, lw["g_ret_gn"], cfg["chunk"], cfg["tc"])
    o_g, gla_new = _gla(proj_s, log_a_s, gla0, lw["g_gla_gn"], cfg["chunk"], cfg["tc"])
    if cache_k is None:
        o_a = _band_attn_prompt(proj_s, lw["rel_bias"])
    else:
        o_a = _band_attn_sample(proj_s, cache_k, cache_v, lw["rel_bias"])
    tok = lambda a: a.reshape(g, t, BRANCH_WIDTH)
    x1, h2 = _merge(x, tok(o_r), tok(o_g), tok(o_a), (sc_m, sh_m, gt_m, sc_f, sh_f), lw["g_mix"], lw["g_ffn"],
                    lw["w_gate"], lw["b_gate"], lw["w_branch"], lw["w_out"], cfg["tm"])
    h2f = h2.reshape(g * t, d)
    info_row, tbl = _route(h2f, lw["wr_t"], lw["br_t"], cfg["tme"], cfg["rpb"])
    nblk = tbl[:, :, 0].reshape(-1)
    x2 = _moe(nblk, h2f, x1, gt_f, info_row, info_row.T, lw["w_eg"], lw["w_eu"], lw["w_ed"],
              cfg["tme"], cfg["rpb"], cfg["gr"])
    k_new = proj_s[:, :, 9 * BRANCH_WIDTH:10 * BRANCH_WIDTH]
    v_new = proj_s[:, :, 10 * BRANCH_WIDTH:11 * BRANCH_WIDTH]
    return x2, ret_new, gla_new, k_new, v_new


def kernel(x_prompt, x_sample, state_ret, state_gla, cache_att_k, cache_att_v, c_prompt, c_sample,
           w_ada, b_ada, g_mix, w_in, gla_a2, gla_a_bias, g_ret_gn, g_gla_gn, g_q_att, g_k_att, rel_bias,
           w_branch, w_gate, b_gate, w_out, g_ffn, w_router_group, b_router_group, w_router_exp,
           b_router_exp, w_exp_gate, w_exp_up, w_exp_down):
    depth = w_ada.shape[0]
    bp, seq, d = x_prompt.shape
    bs, dseq, _ = x_sample.shape
    n_s = bs * dseq

    pad = (-(bp + bs)) % 8
    c_all = jnp.concatenate([c_prompt, c_sample, jnp.zeros((pad, d), F32)], axis=0)
    mod = _ada_mod(c_all, w_ada, b_ada)

    ga0 = 8 * BRANCH_WIDTH
    w_in_b = jnp.concatenate([w_in[:, :, :ga0], w_in[:, :, ga0 + GLA_RANK:]], axis=2).astype(BF16)
    w_ga = jnp.pad(w_in[:, :, ga0:ga0 + GLA_RANK], ((0, 0), (0, 0), (0, V7X_LANES - GLA_RANK))).astype(BF16)
    a2 = jnp.pad(gla_a2, ((0, 0), (0, V7X_LANES - GLA_RANK), (0, 0))).astype(BF16)
    wr_t = jnp.zeros((depth, 32, d), F32)
    wr_t = wr_t.at[:, 0:N_GROUPS].set(jnp.swapaxes(w_router_group, 1, 2))
    wr_t = wr_t.at[:, 8:8 + N_EXPERTS].set(jnp.swapaxes(w_router_exp, 1, 2)).astype(BF16)
    br_t = jnp.zeros((depth, 32, 1), F32)
    br_t = br_t.at[:, 0:N_GROUPS, 0].set(b_router_group).at[:, 8:8 + N_EXPERTS, 0].set(b_router_exp)
    w_gate_b, w_branch_b, w_out_b = w_gate.astype(BF16), w_branch.astype(BF16), w_out.astype(BF16)
    w_eg, w_eu, w_ed = w_exp_gate.astype(BF16), w_exp_up.astype(BF16), w_exp_down.astype(BF16)

    cos_p, sin_p = _rope_tables(jnp.arange(seq, dtype=F32))
    cos_s, sin_s = _rope_tables(PAST_LEN + jnp.arange(dseq, dtype=F32))
    cos_s, sin_s = jnp.tile(cos_s, (bs, 1)), jnp.tile(sin_s, (bs, 1))

    cfg_p = dict(b=bp, tseq=seq, tm=512, chunk=CHUNK, tc=512, tme=1024, rpb=128, gr=256)
    cfg_s = dict(b=bs, tseq=dseq, tm=n_s, chunk=min(dseq, CHUNK), tc=dseq, tme=n_s, rpb=32, gr=256)
    zero_state = jnp.zeros((bp, RET_HEADS, HEAD_DK, HEAD_DK), F32)

    xp = x_prompt
    xs = x_sample.reshape(1, n_s, d)
    outs = [[] for _ in range(8)]
    for l in range(depth):
        lw = dict(
            g_mix=g_mix[l][None], g_ffn=g_ffn[l][None], w_in=w_in_b[l], w_ga=w_ga[l], a2=a2[l],
            a_bias=gla_a_bias[l][None], gq=jnp.tile(g_q_att[l], ATT_HEADS)[None],
            gk=jnp.tile(g_k_att[l], ATT_HEADS)[None], g_ret_gn=g_ret_gn[l][None], g_gla_gn=g_gla_gn[l][None],
            rel_bias=rel_bias[l], w_gate=w_gate_b[l], b_gate=b_gate[l][None], w_branch=w_branch_b[l],
            w_out=w_out_b[l], wr_t=wr_t[l], br_t=br_t[l], w_eg=w_eg[l], w_eu=w_eu[l], w_ed=w_ed[l])
        mods_p = tuple(m[:, None, :] for m in jnp.split(mod[l, :bp], 6, axis=-1))
        mods_s = tuple(jnp.repeat(m, dseq, axis=0)[None] for m in jnp.split(mod[l, bp:bp + bs], 6, axis=-1))
        xp, rp, gp, kp, vp = _layer(xp, mods_p, lw, cfg_p, zero_state, zero_state, None, None, cos_p, sin_p)
        ck = cache_att_k[l].reshape(bs, BAND_PAST, BRANCH_WIDTH)
        cv = cache_att_v[l].reshape(bs, BAND_PAST, BRANCH_WIDTH)
        xs, rs, gs, ks, vs = _layer(xs, mods_s, lw, cfg_s, state_ret[l], state_gla[l], ck, cv, cos_s, sin_s)
        heads = lambda a: a.astype(F32).reshape(a.shape[0], a.shape[1], ATT_HEADS, ATT_HD)
        for lst, val in zip(outs, (rp, gp, heads(kp[:, -BAND_PAST:]), heads(vp[:, -BAND_PAST:]),
                                   rs, gs, heads(ks), heads(vs))):
            lst.append(val)
    return (xp, xs.reshape(bs, dseq, d)) + tuple(jnp.stack(o) for o in outs)
```

```python
import functools

import numpy as np
import jax
import jax.numpy as jnp
from jax import lax
from jax.experimental import pallas as pl
from jax.experimental.pallas import tpu as pltpu

F32 = jnp.float32
BF16 = jnp.bfloat16

D_MODEL = 1024
CHUNK = 64
BRANCH_WIDTH = 512
N_BRANCH = 3
RET_HEADS = 4
HEAD_DK = 128
GLA_RANK = 16
GLA_TAU = 16.0
ATT_HEADS = 8
ATT_HD = 64
BAND_PAST = 512
MAX_REL = 128
N_GROUPS = 4
EXPERTS_PER_GROUP = 4
N_EXPERTS = 16
D_EXPERT = 512
ROPE_BASE = 10000.0
EPS = 1e-6
GN_EPS = 1e-5
PAST_LEN = 1024
N_PROJ_BLOCKS = 11
PROJ_COLS = N_PROJ_BLOCKS * BRANCH_WIDTH

V7X_LANES = 128
V7X_VMEM_BYTES = 64 * 1024 * 1024
VMEM_LIMIT = 56 * 1024 * 1024
NEG_BIG = -1e30


def _cparams(sem, vmem=None):
    return pltpu.CompilerParams(dimension_semantics=sem, vmem_limit_bytes=vmem)


def _const_spec(shape):
    nd = len(shape)
    return pl.BlockSpec(shape, lambda *_: (0,) * nd)


def _dot(a, b):
    return jnp.dot(a, b, preferred_element_type=F32)


def _dot_nt(a, b):
    return lax.dot_general(a, b, (((1,), (1,)), ((), ())), preferred_element_type=F32)


def _dot_tn(a, b):
    return lax.dot_general(a, b, (((0,), (0,)), ((), ())), preferred_element_type=F32)


def _silu(x):
    return x * jax.nn.sigmoid(x)


def _norm_mod(x, g, sc, sh):
    ms = jnp.mean(x * x, axis=-1, keepdims=True)
    return x * lax.rsqrt(ms + EPS) * g * (1.0 + sc) + sh


def _ada_kernel(c_ref, w_ref, b_ref, o_ref):
    s = _silu(c_ref[...])
    o_ref[0] = _dot(s.astype(BF16), w_ref[0].astype(BF16)) + b_ref[0]


def _ada_mod(c_all, w_ada, b_ada):
    depth, d, n = w_ada.shape
    rows = c_all.shape[0]
    bn = 1536
    return pl.pallas_call(
        _ada_kernel,
        out_shape=jax.ShapeDtypeStruct((depth, rows, n), F32),
        grid=(depth, n // bn),
        in_specs=[
            pl.BlockSpec((rows, d), lambda l, j: (0, 0)),
            pl.BlockSpec((1, d, bn), lambda l, j: (l, 0, j)),
            pl.BlockSpec((1, 1, bn), lambda l, j: (l, 0, j)),
        ],
        out_specs=pl.BlockSpec((1, rows, bn), lambda l, j: (l, 0, j)),
        compiler_params=_cparams(("arbitrary", "arbitrary"), VMEM_LIMIT),
        name="ada_mod",
    )(c_all, w_ada, b_ada.reshape(depth, 1, n))


def _rope_heads(a, cosf, sinf):
    outs = []
    for h in range(RET_HEADS):
        ah = a[:, h * HEAD_DK:(h + 1) * HEAD_DK]
        outs.append(ah * cosf + pltpu.roll(ah, HEAD_DK // 2, 1) * sinf)
    return jnp.concatenate(outs, axis=1)


def _rms_heads64(a, gain):
    low = lax.broadcasted_iota(jnp.int32, (1, V7X_LANES), 1) < ATT_HD
    outs = []
    for c in range(a.shape[1] // V7X_LANES):
        ac = a[:, c * V7X_LANES:(c + 1) * V7X_LANES]
        sq = ac * ac
        lo = jnp.sum(jnp.where(low, sq, 0.0), axis=-1, keepdims=True)
        hi = jnp.sum(jnp.where(low, 0.0, sq), axis=-1, keepdims=True)
        ms = jnp.where(low, lo, hi) * (1.0 / ATT_HD)
        outs.append(ac * lax.rsqrt(ms + EPS))
    return jnp.concatenate(outs, axis=1) * gain


def _in_proj_kernel(x_ref, sc_ref, sh_ref, g_ref, w_ref, wga_ref, a2_ref, ab_ref, cos_ref, sin_ref,
                    gq_ref, gk_ref, proj_ref, la_ref):
    x = x_ref[0]
    hb = _norm_mod(x, g_ref[...], sc_ref[0], sh_ref[0]).astype(BF16)
    cosf = cos_ref[...]
    sinf = sin_ref[...]
    for j in range(N_PROJ_BLOCKS):
        cols = slice(j * BRANCH_WIDTH, (j + 1) * BRANCH_WIDTH)
        acc = _dot(hb, w_ref[:, cols])
        if j == 0:
            acc = _rope_heads(acc, cosf, sinf)
        elif j == 1:
            acc = _rope_heads(acc, cosf, sinf) * (HEAD_DK ** -0.5)
        elif j == 4:
            acc = acc * (HEAD_DK ** -0.5)
        elif j == 8:
            acc = _rms_heads64(acc, gq_ref[...]) * (ATT_HD ** -0.5)
        elif j == 9:
            acc = _rms_heads64(acc, gk_ref[...])
        proj_ref[0, :, cols] = acc.astype(BF16)
    ga = _dot(hb, wga_ref[...])
    z = _dot(ga.astype(BF16), a2_ref[...]) + ab_ref[...]
    la_ref[0] = jax.nn.log_sigmoid(z) * (1.0 / GLA_TAU)


def _in_proj(x, sc, sh, g_mix, w_in, w_ga, a2, a_bias, cosf, sinf, gq, gk, tm):
    g, t, d = x.shape
    rm = sc.shape[1]
    mod_spec = pl.BlockSpec((1, rm, d), (lambda b, i: (b, 0, 0)) if rm == 1 else (lambda b, i: (b, i, 0)))
    return pl.pallas_call(
        _in_proj_kernel,
        out_shape=(jax.ShapeDtypeStruct((g, t, PROJ_COLS), BF16),
                   jax.ShapeDtypeStruct((g, t, BRANCH_WIDTH), F32)),
        grid=(g, t // tm),
        in_specs=[
            pl.BlockSpec((1, tm, d), lambda b, i: (b, i, 0)),
            mod_spec, mod_spec,
            _const_spec((1, d)),
            _const_spec((d, PROJ_COLS)),
            _const_spec((d, V7X_LANES)),
            _const_spec((V7X_LANES, BRANCH_WIDTH)),
            _const_spec((1, BRANCH_WIDTH)),
            pl.BlockSpec((tm, HEAD_DK), lambda b, i: (i, 0)),
            pl.BlockSpec((tm, HEAD_DK), lambda b, i: (i, 0)),
            _const_spec((1, BRANCH_WIDTH)),
            _const_spec((1, BRANCH_WIDTH)),
        ],
        out_specs=(pl.BlockSpec((1, tm, PROJ_COLS), lambda b, i: (b, i, 0)),
                   pl.BlockSpec((1, tm, BRANCH_WIDTH), lambda b, i: (b, i, 0))),
        compiler_params=_cparams(("arbitrary", "arbitrary"), VMEM_LIMIT),
        name="in_proj",
    )(x, sc, sh, g_mix, w_in, w_ga, a2, a_bias, cosf, sinf, gq, gk)


def _retention_kernel(q_ref, k_ref, v_ref, g_ref, s0_ref, dm_ref, qd_ref, kd_ref, bd_ref, gn_ref,
                      o_ref, so_ref, st_ref, *, chunk, n_chunks):
    @pl.when(pl.program_id(1) == 0)
    def _():
        st_ref[...] = s0_ref[0]

    def body(c, carry):
        rows = pl.ds(pl.multiple_of(c * chunk, chunk), chunk)
        for h in range(RET_HEADS):
            cols = slice(h * HEAD_DK, (h + 1) * HEAD_DK)
            q = q_ref[0, rows, cols]
            k = k_ref[0, rows, cols]
            v = v_ref[0, rows, cols]
            scores = _dot_nt(q, k) * dm_ref[h]
            o = _dot(scores.astype(BF16), v)
            o = o + _dot((q.astype(F32) * qd_ref[h]).astype(BF16), st_ref[h].astype(BF16))
            st_ref[h] = st_ref[h] * bd_ref[h] + _dot_tn((k.astype(F32) * kd_ref[h]).astype(BF16), v)
            mu = jnp.mean(o, axis=-1, keepdims=True)
            oc = o - mu
            var = jnp.mean(oc * oc, axis=-1, keepdims=True)
            on = oc * lax.rsqrt(var + GN_EPS) * gn_ref[:, cols]
            o_ref[0, rows, cols] = (_silu(g_ref[0, rows, cols].astype(F32)) * on).astype(BF16)
        return carry

    lax.fori_loop(0, n_chunks, body, 0)

    @pl.when(pl.program_id(1) == pl.num_programs(1) - 1)
    def _():
        so_ref[0] = st_ref[...]


def _retention_tables(chunk):
    log_gamma = jnp.log(1.0 - jnp.exp2(-5.0 - jnp.arange(RET_HEADS, dtype=F32)))
    idx = jnp.arange(chunk, dtype=F32)
    diff = idx[:, None] - idx[None, :]
    dmask = jnp.where(diff >= 0, jnp.exp(log_gamma[:, None, None] * jnp.maximum(diff, 0.0)), 0.0)
    qd = jnp.exp(log_gamma[:, None] * (idx + 1.0))[:, :, None]
    kd = jnp.exp(log_gamma[:, None] * (chunk - 1.0 - idx))[:, :, None]
    bd = jnp.exp(log_gamma * chunk)[:, None, None]
    bc = lambda a, r: jnp.broadcast_to(a, (RET_HEADS, r, HEAD_DK)).astype(F32)
    return dmask.astype(F32), bc(qd, chunk), bc(kd, chunk), bc(bd, 1)


def _retention(proj, state0, gn, chunk, tc):
    b, t, _ = proj.shape
    dm, qd, kd, bd = _retention_tables(chunk)
    col = lambda j: pl.BlockSpec((1, tc, BRANCH_WIDTH), lambda i, s, j=j: (i, s, j))
    st_spec = pl.BlockSpec((1, RET_HEADS, HEAD_DK, HEAD_DK), lambda i, s: (i, 0, 0, 0))
    return pl.pallas_call(
        functools.partial(_retention_kernel, chunk=chunk, n_chunks=tc // chunk),
        out_shape=(jax.ShapeDtypeStruct((b, t, BRANCH_WIDTH), BF16),
                   jax.ShapeDtypeStruct((b, RET_HEADS, HEAD_DK, HEAD_DK), F32)),
        grid=(b, t // tc),
        in_specs=[col(0), col(1), col(2), col(3), st_spec,
                  _const_spec(dm.shape), _const_spec(qd.shape), _const_spec(kd.shape), _const_spec(bd.shape),
                  _const_spec((1, BRANCH_WIDTH))],
        out_specs=(pl.BlockSpec((1, tc, BRANCH_WIDTH), lambda i, s: (i, s, 0)), st_spec),
        scratch_shapes=[pltpu.VMEM((RET_HEADS, HEAD_DK, HEAD_DK), F32)],
        compiler_params=_cparams(("arbitrary", "arbitrary")),
        name="retention",
    )(proj, proj, proj, proj, state0, dm, qd, kd, bd, gn)


GLA_SAFE_LOG_DECAY = -60.0


def _gla_intra_direct(q, bh, rows_ref, chunk):
    trow = lax.broadcasted_iota(jnp.int32, (chunk, 1), 0)

    def step(s, acc):
        bs = rows_ref[0, pl.ds(s, 1), :]
        ks = rows_ref[1, pl.ds(s, 1), :]
        vs = rows_ref[2, pl.ds(s, 1), :]
        e = jnp.exp(jnp.minimum(bh - bs, 0.0))
        col = jnp.sum(q * ks * e, axis=-1, keepdims=True)
        return acc + jnp.where(trow >= s, col, 0.0) * vs

    return lax.fori_loop(0, chunk, step, jnp.zeros((chunk, HEAD_DK), F32))


def _gla_kernel(q_ref, k_ref, v_ref, r_ref, la_ref, s0_ref, tri_ref, gn_ref, o_ref, so_ref, st_ref, rows_ref,
                *, chunk, n_chunks):
    @pl.when(pl.program_id(1) == 0)
    def _():
        for h in range(RET_HEADS):
            st_ref[h] = s0_ref[0, h].T

    tri = tri_ref[...]
    causal = (lax.broadcasted_iota(jnp.int32, (chunk, chunk), 0)
              >= lax.broadcasted_iota(jnp.int32, (chunk, chunk), 1))

    def chunk_body(factored):
        def body(c, carry):
            rows = pl.ds(pl.multiple_of(c * chunk, chunk), chunk)
            la = la_ref[0, rows, :]
            la_hi = la.astype(BF16)
            la_lo = (la - la_hi.astype(F32)).astype(BF16)
            b_all = _dot(tri, la_hi) + _dot(tri, la_lo)
            for h in range(RET_HEADS):
                cols = slice(h * HEAD_DK, (h + 1) * HEAD_DK)
                q = q_ref[0, rows, cols].astype(F32)
                k = k_ref[0, rows, cols].astype(F32)
                v = v_ref[0, rows, cols]
                bh = b_all[:, cols]
                bl = bh[chunk - 1:chunk, :]
                qh = (q * jnp.exp(bh)).astype(BF16)
                if factored:
                    kh = (k * jnp.exp(-bh)).astype(BF16)
                    scores = jnp.where(causal, _dot_nt(qh, kh), 0.0)
                    o = _dot(scores.astype(BF16), v)
                else:
                    rows_ref[0] = bh
                    rows_ref[1] = k
                    rows_ref[2] = v.astype(F32)
                    o = _gla_intra_direct(q, bh, rows_ref, chunk)
                o = o + _dot_nt(qh, st_ref[h].astype(BF16))
                kd = (k * jnp.exp(bl - bh)).astype(BF16)
                st_ref[h] = st_ref[h] * jnp.exp(bl) + _dot_tn(v, kd)
                ms = jnp.mean(o * o, axis=-1, keepdims=True)
                on = o * lax.rsqrt(ms + EPS) * gn_ref[:, cols]
                o_ref[0, rows, cols] = (_silu(r_ref[0, rows, cols].astype(F32)) * on).astype(BF16)
            return carry
        return body

    lowest = None
    for c in range(n_chunks):
        tot = jnp.sum(la_ref[0, c * chunk:(c + 1) * chunk, :], axis=0, keepdims=True)
        lowest = tot if lowest is None else jnp.minimum(lowest, tot)
    safe = jnp.min(lowest) > GLA_SAFE_LOG_DECAY

    @pl.when(safe)
    def _():
        lax.fori_loop(0, n_chunks, chunk_body(True), 0)

    @pl.when(jnp.logical_not(safe))
    def _():
        lax.fori_loop(0, n_chunks, chunk_body(False), 0)

    @pl.when(pl.program_id(1) == pl.num_programs(1) - 1)
    def _():
        for h in range(RET_HEADS):
            so_ref[0, h] = st_ref[h].T


def _gla(proj, log_a, state0, gn, chunk, tc):
    b, t, _ = proj.shape
    tri = jnp.tril(jnp.ones((chunk, chunk), F32)).astype(BF16)
    col = lambda j: pl.BlockSpec((1, tc, BRANCH_WIDTH), lambda i, s, j=j: (i, s, j))
    st_spec = pl.BlockSpec((1, RET_HEADS, HEAD_DK, HEAD_DK), lambda i, s: (i, 0, 0, 0))
    return pl.pallas_call(
        functools.partial(_gla_kernel, chunk=chunk, n_chunks=tc // chunk),
        out_shape=(jax.ShapeDtypeStruct((b, t, BRANCH_WIDTH), BF16),
                   jax.ShapeDtypeStruct((b, RET_HEADS, HEAD_DK, HEAD_DK), F32)),
        grid=(b, t // tc),
        in_specs=[col(4), col(5), col(6), col(7),
                  pl.BlockSpec((1, tc, BRANCH_WIDTH), lambda i, s: (i, s, 0)),
                  st_spec, _const_spec((chunk, chunk)), _const_spec((1, BRANCH_WIDTH))],
        out_specs=(pl.BlockSpec((1, tc, BRANCH_WIDTH), lambda i, s: (i, s, 0)), st_spec),
        scratch_shapes=[pltpu.VMEM((RET_HEADS, HEAD_DK, HEAD_DK), F32),
                        pltpu.VMEM((3, chunk, HEAD_DK), F32)],
        compiler_params=_cparams(("arbitrary", "arbitrary")),
        name="gla",
    )(proj, proj, proj, proj, log_a, state0, tri, gn)


def _band_attn_kernel(q_ref, kp_ref, kc_ref, vp_ref, vc_ref, bias_ref, o_ref, kw_ref, vw_ref,
                      *, n_sub, rb, win, prev_rows, mask_start):
    cur_rows = kc_ref.shape[1]
    kw_ref[0:prev_rows, :] = kp_ref[0].astype(BF16)
    kw_ref[prev_rows:prev_rows + cur_rows, :] = kc_ref[0]
    vw_ref[0:prev_rows, :] = vp_ref[0].astype(BF16)
    vw_ref[prev_rows:prev_rows + cur_rows, :] = vc_ref[0]
    low = lax.broadcasted_iota(jnp.int32, (1, V7X_LANES), 1) < ATT_HD
    q0 = pl.program_id(1) * cur_rows

    def body(i, carry):
        start = pl.multiple_of(i * rb, rb)
        qrows = pl.ds(start, rb)
        wrows = pl.ds(start, win)
        if mask_start:
            valid = lax.broadcasted_iota(jnp.int32, (1, win), 1) >= prev_rows - q0 - start
        for p in range(ATT_HEADS // 2):
            cols = slice(p * V7X_LANES, (p + 1) * V7X_LANES)
            q2 = q_ref[0, qrows, cols]
            zero = jnp.zeros_like(q2)
            qst = jnp.concatenate([jnp.where(low, q2, zero), jnp.where(low, zero, q2)], axis=0)
            s = _dot_nt(qst, kw_ref[wrows, cols]) + bias_ref[p]
            if mask_start:
                s = jnp.where(valid, s, NEG_BIG)
            m = jnp.max(s, axis=-1, keepdims=True)
            e = jnp.exp(s - m)
            den = jnp.sum(e, axis=-1, keepdims=True)
            o2 = _dot(e.astype(BF16), vw_ref[wrows, cols]) / den
            o_ref[0, qrows, cols] = jnp.where(low, o2[:rb], o2[rb:]).astype(BF16)
        return carry

    lax.fori_loop(0, n_sub, body, 0)


def _rel_bias_table(rel_bias, rb, win, chunk):
    u = np.arange(-(rb - 1), win)
    ext = rel_bias.astype(F32)[:, np.clip(BAND_PAST - u, -MAX_REL, MAX_REL) + MAX_REL]
    bias = jnp.stack([ext[:, rb - 1 - r:rb - 1 - r + win] for r in range(rb)], axis=1)
    r = np.arange(rb)[:, None]
    j = np.arange(win)[None, :]
    c0 = (r // chunk) * chunk
    allowed = (j >= c0) & (j < c0 + BAND_PAST + chunk)
    bias = jnp.where(jnp.asarray(allowed)[None], bias, NEG_BIG)
    return bias.reshape(ATT_HEADS // 2, 2 * rb, win)


def _band_attn_prompt(proj, rel_bias):
    b, t, _ = proj.shape
    qb, rb = 512, 128
    win = BAND_PAST + rb
    bias = _rel_bias_table(rel_bias, rb, win, CHUNK)
    cur = lambda j: pl.BlockSpec((1, qb, BRANCH_WIDTH), lambda i, s, j=j: (i, s, j))
    prev = lambda j: pl.BlockSpec((1, qb, BRANCH_WIDTH), lambda i, s, j=j: (i, jnp.maximum(s - 1, 0), j))
    return pl.pallas_call(
        functools.partial(_band_attn_kernel, n_sub=qb // rb, rb=rb, win=win, prev_rows=BAND_PAST,
                          mask_start=True),
        out_shape=jax.ShapeDtypeStruct((b, t, BRANCH_WIDTH), BF16),
        grid=(b, t // qb),
        in_specs=[cur(8), prev(9), cur(9), prev(10), cur(10), _const_spec(bias.shape)],
        out_specs=pl.BlockSpec((1, qb, BRANCH_WIDTH), lambda i, s: (i, s, 0)),
        scratch_shapes=[pltpu.VMEM((BAND_PAST + qb, BRANCH_WIDTH), BF16),
                        pltpu.VMEM((BAND_PAST + qb, BRANCH_WIDTH), BF16)],
        compiler_params=_cparams(("arbitrary", "arbitrary")),
        name="band_attn",
    )(proj, proj, proj, proj, proj, bias)


def _band_attn_sample(proj, cache_k, cache_v, rel_bias):
    b, t, _ = proj.shape
    win = BAND_PAST + t
    bias = _rel_bias_table(rel_bias, t, win, t)
    cur = lambda j: pl.BlockSpec((1, t, BRANCH_WIDTH), lambda i, s, j=j: (i, 0, j))
    cache = pl.BlockSpec((1, BAND_PAST, BRANCH_WIDTH), lambda i, s: (i, 0, 0))
    return pl.pallas_call(
        functools.partial(_band_attn_kernel, n_sub=1, rb=t, win=win, prev_rows=BAND_PAST, mask_start=False),
        out_shape=jax.ShapeDtypeStruct((b, t, BRANCH_WIDTH), BF16),
        grid=(b, 1),
        in_specs=[cur(8), cache, cur(9), cache, cur(10), _const_spec(bias.shape)],
        out_specs=pl.BlockSpec((1, t, BRANCH_WIDTH), lambda i, s: (i, 0, 0)),
        scratch_shapes=[pltpu.VMEM((win, BRANCH_WIDTH), BF16), pltpu.VMEM((win, BRANCH_WIDTH), BF16)],
        compiler_params=_cparams(("arbitrary", "arbitrary")),
        name="band_attn_sample",
    )(proj, cache_k, proj, cache_v, proj, bias)


def _merge_kernel(x_ref, or_ref, og_ref, oa_ref, scm_ref, shm_ref, gtm_ref, scf_ref, shf_ref,
                  gmix_ref, gffn_ref, wg_ref, bg_ref, wb_ref, wo_ref, xo_ref, h2_ref):
    x = x_ref[0]
    d = x.shape[1]
    hb = _norm_mod(x, gmix_ref[...], scm_ref[0], shm_ref[0]).astype(BF16)
    merged = None
    for n, o_ref in enumerate((or_ref, og_ref, oa_ref)):
        gate = jax.nn.sigmoid(_dot(hb, wg_ref[:, n * d:(n + 1) * d]) + bg_ref[:, n * d:(n + 1) * d])
        y = gate * _dot(o_ref[0], wb_ref[n])
        merged = y if merged is None else merged + y
    mix = _dot(merged.astype(BF16), wo_ref[...])
    xn = x + gtm_ref[0] * mix
    xo_ref[0] = xn
    h2_ref[0] = _norm_mod(xn, gffn_ref[...], scf_ref[0], shf_ref[0]).astype(BF16)


def _merge(x, o_r, o_g, o_a, mods, g_mix, g_ffn, w_gate, b_gate, w_branch, w_out, tm):
    g, t, d = x.shape
    rm = mods[0].shape[1]
    mod_spec = pl.BlockSpec((1, rm, d), (lambda b, i: (b, 0, 0)) if rm == 1 else (lambda b, i: (b, i, 0)))
    tok = lambda w: pl.BlockSpec((1, tm, w), lambda b, i: (b, i, 0))
    return pl.pallas_call(
        _merge_kernel,
        out_shape=(jax.ShapeDtypeStruct((g, t, d), F32), jax.ShapeDtypeStruct((g, t, d), BF16)),
        grid=(g, t // tm),
        in_specs=[tok(d), tok(BRANCH_WIDTH), tok(BRANCH_WIDTH), tok(BRANCH_WIDTH)] + [mod_spec] * 5 + [
            _const_spec((1, d)), _const_spec((1, d)),
            _const_spec((d, N_BRANCH * d)), _const_spec((1, N_BRANCH * d)),
            _const_spec((N_BRANCH, BRANCH_WIDTH, d)), _const_spec((d, d))],
        out_specs=(tok(d), tok(d)),
        compiler_params=_cparams(("arbitrary", "arbitrary"), VMEM_LIMIT),
        name="merge",
    )(x, o_r, o_g, o_a, *mods, g_mix, g_ffn, w_gate, b_gate, w_branch, w_out)


def _first_argmax(vals, n):
    row = lax.broadcasted_iota(jnp.int32, vals.shape, 0).astype(F32)
    m = jnp.max(vals, axis=0, keepdims=True)
    idx = jnp.min(jnp.where(vals == m, row, float(n)), axis=0, keepdims=True)
    return m, idx


def _route_kernel(h_ref, wr_ref, br_ref, up_ref, l16_ref, info_ref, tbl_ref, *, rows_per_block):
    tm = h_ref.shape[0]
    lg = _dot_nt(wr_ref[...], h_ref[...]) + br_ref[...]
    g = lg[0:N_GROUPS]
    gmax, grp = _first_argmax(g, N_GROUPS)
    p_group = 1.0 / jnp.sum(jnp.exp(g - gmax), axis=0, keepdims=True)
    esel = jnp.zeros((EXPERTS_PER_GROUP, tm), F32)
    for gi in range(N_GROUPS):
        blk = lg[8 + gi * EXPERTS_PER_GROUP:8 + (gi + 1) * EXPERTS_PER_GROUP]
        esel = esel + jnp.where(grp == float(gi), blk, 0.0)
    v1, i1 = _first_argmax(esel, EXPERTS_PER_GROUP)
    row4 = lax.broadcasted_iota(jnp.int32, esel.shape, 0).astype(F32)
    v2, i2 = _first_argmax(jnp.where(row4 == i1, -jnp.inf, esel), EXPERTS_PER_GROUP)
    e21 = jnp.exp(v2 - v1)
    w1 = p_group / (1.0 + e21)
    w2 = p_group * e21 / (1.0 + e21)
    e1 = grp * float(EXPERTS_PER_GROUP) + i1
    e2 = grp * float(EXPERTS_PER_GROUP) + i2
    row16 = lax.broadcasted_iota(jnp.int32, (N_EXPERTS, tm), 0).astype(F32)
    hit1 = row16 == e1
    hit2 = row16 == e2
    onehot = jnp.where(hit1 | hit2, 1.0, 0.0)
    prefix = _dot(onehot.astype(BF16), up_ref[...])
    cnt = jnp.sum(onehot, axis=1, keepdims=True)
    nblk = jnp.floor((cnt + float(rows_per_block - 1)) * (1.0 / rows_per_block))
    nblk_b = jnp.broadcast_to(nblk, (N_EXPERTS, V7X_LANES))
    offb = _dot(l16_ref[...], nblk_b.astype(BF16))[:, 0:1]
    base = offb * float(rows_per_block) + prefix
    info_ref[0:1, :] = jnp.sum(jnp.where(hit1, base, 0.0), axis=0, keepdims=True)
    info_ref[1:2, :] = jnp.sum(jnp.where(hit2, base, 0.0), axis=0, keepdims=True)
    info_ref[2:3, :] = w1
    info_ref[3:4, :] = w2
    info_ref[4:8, :] = jnp.zeros((4, tm), F32)
    tbl_ref[0] = nblk_b.astype(jnp.int32)


def _route(h2, wr_t, br_t, tm, rows_per_block):
    n, d = h2.shape
    nt = n // tm
    upper = jnp.triu(jnp.ones((tm, tm), F32), 1).astype(BF16)
    l16 = jnp.tril(jnp.ones((N_EXPERTS, N_EXPERTS), F32), -1).astype(BF16)
    return pl.pallas_call(
        functools.partial(_route_kernel, rows_per_block=rows_per_block),
        out_shape=(jax.ShapeDtypeStruct((8, n), F32),
                   jax.ShapeDtypeStruct((nt, N_EXPERTS, V7X_LANES), jnp.int32)),
        grid=(nt,),
        in_specs=[pl.BlockSpec((tm, d), lambda i: (i, 0)),
                  _const_spec((32, d)), _const_spec((32, 1)),
                  _const_spec((tm, tm)), _const_spec((N_EXPERTS, N_EXPERTS))],
        out_specs=(pl.BlockSpec((8, tm), lambda i: (0, i)),
                   pl.BlockSpec((1, N_EXPERTS, V7X_LANES), lambda i: (i, 0, 0))),
        compiler_params=_cparams(("arbitrary",)),
        name="route",
    )(h2, wr_t, br_t, upper, l16)


SEG = 16
GATHER_ROWS = 256
EXPERT_ROWS = 512


def _segment_copies(cnt_ref, loff_ref, goff_ref, tile, n_bits, local_ref, global_ref, sem, to_global):
    for e in range(N_EXPERTS):
        cnt = cnt_ref[tile * N_EXPERTS + e]
        lo = loff_ref[tile * N_EXPERTS + e]
        go = goff_ref[tile * N_EXPERTS + e]
        for k in range(n_bits):
            size = SEG << k
            done = cnt & ((1 << k) - 1)
            loc = local_ref.at[pl.ds(pl.multiple_of((lo + done) * SEG, SEG), size)]
            glo = global_ref.at[pl.ds(pl.multiple_of((go + done) * SEG, SEG), size)]
            cp = pltpu.make_async_copy(loc, glo, sem) if to_global else pltpu.make_async_copy(glo, loc, sem)
            yield ((cnt >> k) & 1) == 1, cp


def _tile_units(cnt_ref, tile):
    total = jnp.int32(0)
    for e in range(N_EXPERTS):
        total = total + cnt_ref[tile * N_EXPERTS + e]
    return total


def _sort_kernel(cnt_ref, loff_ref, goff_ref, h_ref, irow_ref, xin_ref, xout_ref, xs_ref, sem, *, n_bits):
    del xin_ref
    i = pl.program_id(0)
    gr = GATHER_ROWS
    n_gather = (_tile_units(cnt_ref, i) * SEG + gr - 1) // gr
    pos1 = irow_ref[0:1, :]
    pos2 = irow_ref[1:2, :]
    h = h_ref[...]

    def gather(gb, carry):
        r0 = pl.multiple_of(gb * gr, gr)
        srow = (lax.broadcasted_iota(jnp.int32, (gr, 1), 0) + r0).astype(F32)
        sel = jnp.where((srow == pos1) | (srow == pos2), 1.0, 0.0).astype(BF16)
        xs_ref[pl.ds(r0, gr), :] = _dot(sel, h).astype(BF16)
        return carry

    lax.fori_loop(0, n_gather, gather, 0)
    for pred, cp in _segment_copies(cnt_ref, loff_ref, goff_ref, i, n_bits, xs_ref, xout_ref, sem, True):
        pl.when(pred)(cp.start)
    for pred, cp in _segment_copies(cnt_ref, loff_ref, goff_ref, i, n_bits, xs_ref, xout_ref, sem, True):
        pl.when(pred)(cp.wait)


def _experts_kernel(be_ref, nv_ref, x_ref, wg_ref, wu_ref, wd_ref, o_ref):
    del be_ref
    used = pl.program_id(0) < nv_ref[0]

    @pl.when(used)
    def _():
        xb = x_ref[...]
        a = _silu(_dot(xb, wg_ref[0])) * _dot(xb, wu_ref[0])
        o_ref[...] = _dot(a.astype(BF16), wd_ref[0]).astype(BF16)

    @pl.when(jnp.logical_not(used))
    def _():
        o_ref[...] = jnp.zeros(o_ref.shape, BF16)


def _combine_kernel(cnt_ref, loff_ref, goff_ref, ds_ref, icol_ref, x_ref, gt_ref, o_ref, dl_ref, sem,
                    *, n_bits, rb):
    i = pl.program_id(0)
    tm = x_ref.shape[1]
    s_loc = dl_ref.shape[0]
    for pred, cp in _segment_copies(cnt_ref, loff_ref, goff_ref, i, n_bits, dl_ref, ds_ref, sem, False):
        pl.when(pred)(cp.start)

    def zero_tail(u, carry):
        dl_ref[pl.ds(pl.multiple_of(u * SEG, SEG), SEG), :] = jnp.zeros((SEG, dl_ref.shape[1]), BF16)
        return carry

    lax.fori_loop(_tile_units(cnt_ref, i), s_loc // SEG, zero_tail, 0)
    for pred, cp in _segment_copies(cnt_ref, loff_ref, goff_ref, i, n_bits, dl_ref, ds_ref, sem, False):
        pl.when(pred)(cp.wait)
    scol = lax.broadcasted_iota(jnp.int32, (1, s_loc), 1).astype(F32)
    for r in range(tm // rb):
        rows = slice(r * rb, (r + 1) * rb)
        cmb = (jnp.where(scol == icol_ref[rows, 0:1], icol_ref[rows, 2:3], 0.0)
               + jnp.where(scol == icol_ref[rows, 1:2], icol_ref[rows, 3:4], 0.0))
        y = _dot(cmb.astype(BF16), dl_ref[...])
        gt = gt_ref[0] if gt_ref.shape[1] == 1 else gt_ref[0, rows, :]
        o_ref[0, rows, :] = x_ref[0, rows, :] + gt * y


def _sorted_rows(n_tokens, tm):
    n_tiles = n_tokens // tm
    rows = 2 * n_tokens + n_tiles * N_EXPERTS * (SEG - 1) + N_EXPERTS * (EXPERT_ROWS - 1)
    return -(-rows // EXPERT_ROWS) * EXPERT_ROWS


def _moe(cnt, h2, x, gt, info_row, info_col, w_eg, w_eu, w_ed, x_sorted, tm):
    g, t, d = x.shape
    n = g * t
    n_tiles = n // tm
    tpg = t // tm
    rm = gt.shape[1]
    n_blocks = x_sorted.shape[0] // EXPERT_ROWS
    n_bits = (tm // SEG).bit_length()
    s_loc = -(-(2 * tm + N_EXPERTS * (SEG - 1)) // GATHER_ROWS) * GATHER_ROWS
    upb = EXPERT_ROWS // SEG
    reg_blk = (jnp.sum(cnt, axis=0) + upb - 1) // upb
    blk_end = jnp.cumsum(reg_blk)
    goff = ((blk_end - reg_blk)[None, :] * upb + jnp.cumsum(cnt, axis=0) - cnt).reshape(-1)
    loff = (jnp.cumsum(cnt, axis=1) - cnt).reshape(-1)
    cntf = cnt.reshape(-1)
    n_valid = blk_end[-1:]
    blk_expert = jnp.minimum(
        jnp.sum(jnp.arange(n_blocks, dtype=jnp.int32)[:, None] >= blk_end[None, :], axis=1), N_EXPERTS - 1
    ).astype(jnp.int32)

    x_sorted = pl.pallas_call(
        functools.partial(_sort_kernel, n_bits=n_bits),
        out_shape=jax.ShapeDtypeStruct(x_sorted.shape, BF16),
        grid_spec=pltpu.PrefetchScalarGridSpec(
            num_scalar_prefetch=3,
            grid=(n_tiles,),
            in_specs=[pl.BlockSpec((tm, d), lambda i, *_: (i, 0)),
                      pl.BlockSpec((8, tm), lambda i, *_: (0, i)),
                      pl.BlockSpec(memory_space=pl.ANY)],
            out_specs=pl.BlockSpec(memory_space=pl.ANY),
            scratch_shapes=[pltpu.VMEM((s_loc, d), BF16), pltpu.SemaphoreType.DMA(())],
        ),
        input_output_aliases={5: 0},
        compiler_params=_cparams(("arbitrary",)),
        name="moe_sort",
    )(cntf, loff, goff, h2, info_row, x_sorted)

    blk = lambda b, be, nv: (jnp.minimum(b, nv[0] - 1), 0)
    d_sorted = pl.pallas_call(
        _experts_kernel,
        out_shape=jax.ShapeDtypeStruct(x_sorted.shape, BF16),
        grid_spec=pltpu.PrefetchScalarGridSpec(
            num_scalar_prefetch=2,
            grid=(n_blocks,),
            in_specs=[pl.BlockSpec((EXPERT_ROWS, d), blk),
                      pl.BlockSpec((1, d, D_EXPERT), lambda b, be, nv: (be[b], 0, 0)),
                      pl.BlockSpec((1, d, D_EXPERT), lambda b, be, nv: (be[b], 0, 0)),
                      pl.BlockSpec((1, D_EXPERT, d), lambda b, be, nv: (be[b], 0, 0))],
            out_specs=pl.BlockSpec((EXPERT_ROWS, d), lambda b, be, nv: (b, 0)),
        ),
        compiler_params=_cparams(("arbitrary",), VMEM_LIMIT),
        name="moe_experts",
    )(blk_expert, n_valid, x_sorted, w_eg, w_eu, w_ed)

    gt_spec = pl.BlockSpec((1, rm, d), (lambda i, *_: (i // tpg, 0, 0)) if rm == 1
                           else (lambda i, *_: (i // tpg, i % tpg, 0)))
    tok3 = pl.BlockSpec((1, tm, d), lambda i, *_: (i // tpg, i % tpg, 0))
    out = pl.pallas_call(
        functools.partial(_combine_kernel, n_bits=n_bits, rb=min(tm, 256)),
        out_shape=jax.ShapeDtypeStruct((g, t, d), F32),
        grid_spec=pltpu.PrefetchScalarGridSpec(
            num_scalar_prefetch=3,
            grid=(n_tiles,),
            in_specs=[pl.BlockSpec(memory_space=pl.ANY),
                      pl.BlockSpec((tm, 8), lambda i, *_: (i, 0)),
                      tok3, gt_spec],
            out_specs=tok3,
            scratch_shapes=[pltpu.VMEM((s_loc, d), BF16), pltpu.SemaphoreType.DMA(())],
        ),
        compiler_params=_cparams(("arbitrary",), VMEM_LIMIT),
        name="moe_combine",
    )(cntf, loff, goff, d_sorted, info_col, x, gt)
    return out, x_sorted


def _rope_tables(pos):
    half = HEAD_DK // 2
    inv_freq = ROPE_BASE ** (-jnp.arange(half, dtype=F32) / half)
    ang = pos[:, None] * inv_freq[None, :]
    cos, sin = jnp.cos(ang), jnp.sin(ang)
    return jnp.concatenate([cos, cos], axis=1), jnp.concatenate([-sin, sin], axis=1)


def _layer(x, mods, lw, cfg, ret0, gla0, cache_k, cache_v, cosf, sinf, x_sorted):
    g, t, d = x.shape
    b, tseq = cfg["b"], cfg["tseq"]
    sh_m, sc_m, gt_m, sh_f, sc_f, gt_f = mods
    proj, log_a = _in_proj(x, sc_m, sh_m, lw["g_mix"], lw["w_in"], lw["w_ga"], lw["a2"], lw["a_bias"],
                           cosf, sinf, lw["gq"], lw["gk"], cfg["tm"])
    proj_s = proj.reshape(b, tseq, PROJ_COLS)
    log_a_s = log_a.reshape(b, tseq, BRANCH_WIDTH)
    o_r, ret_new = _retention(proj_s, ret0, lw["g_ret_gn"], cfg["chunk"], cfg["tc"])
    o_g, gla_new = _gla(proj_s, log_a_s, gla0, lw["g_gla_gn"], cfg["chunk"], cfg["tc"])
    if cache_k is None:
        o_a = _band_attn_prompt(proj_s, lw["rel_bias"])
    else:
        o_a = _band_attn_sample(proj_s, cache_k, cache_v, lw["rel_bias"])
    tok = lambda a: a.reshape(g, t, BRANCH_WIDTH)
    x1, h2 = _merge(x, tok(o_r), tok(o_g), tok(o_a), (sc_m, sh_m, gt_m, sc_f, sh_f), lw["g_mix"], lw["g_ffn"],
                    lw["w_gate"], lw["b_gate"], lw["w_branch"], lw["w_out"], cfg["tm"])
    h2f = h2.reshape(g * t, d)
    info_row, tbl = _route(h2f, lw["wr_t"], lw["br_t"], cfg["tme"], SEG)
    x2, x_sorted = _moe(tbl[:, :, 0], h2f, x1, gt_f, info_row, info_row.T, lw["w_eg"], lw["w_eu"], lw["w_ed"],
                        x_sorted, cfg["tme"])
    tail = proj_s[:, -min(tseq, BAND_PAST):]
    k_new = tail[:, :, 9 * BRANCH_WIDTH:10 * BRANCH_WIDTH]
    v_new = tail[:, :, 10 * BRANCH_WIDTH:11 * BRANCH_WIDTH]
    return x2, ret_new, gla_new, k_new, v_new, x_sorted


def kernel(x_prompt, x_sample, state_ret, state_gla, cache_att_k, cache_att_v, c_prompt, c_sample,
           w_ada, b_ada, g_mix, w_in, gla_a2, gla_a_bias, g_ret_gn, g_gla_gn, g_q_att, g_k_att, rel_bias,
           w_branch, w_gate, b_gate, w_out, g_ffn, w_router_group, b_router_group, w_router_exp,
           b_router_exp, w_exp_gate, w_exp_up, w_exp_down):
    depth = w_ada.shape[0]
    bp, seq, d = x_prompt.shape
    bs, dseq, _ = x_sample.shape
    n_s = bs * dseq

    pad = (-(bp + bs)) % 8
    c_all = jnp.concatenate([c_prompt, c_sample, jnp.zeros((pad, d), F32)], axis=0)
    mod = _ada_mod(c_all, w_ada, b_ada)

    ga0 = 8 * BRANCH_WIDTH
    w_in_b = jnp.concatenate([w_in[:, :, :ga0], w_in[:, :, ga0 + GLA_RANK:]], axis=2).astype(BF16)
    w_ga = jnp.pad(w_in[:, :, ga0:ga0 + GLA_RANK], ((0, 0), (0, 0), (0, V7X_LANES - GLA_RANK))).astype(BF16)
    a2 = jnp.pad(gla_a2, ((0, 0), (0, V7X_LANES - GLA_RANK), (0, 0))).astype(BF16)
    wr_t = jnp.zeros((depth, 32, d), F32)
    wr_t = wr_t.at[:, 0:N_GROUPS].set(jnp.swapaxes(w_router_group, 1, 2))
    wr_t = wr_t.at[:, 8:8 + N_EXPERTS].set(jnp.swapaxes(w_router_exp, 1, 2)).astype(BF16)
    br_t = jnp.zeros((depth, 32, 1), F32)
    br_t = br_t.at[:, 0:N_GROUPS, 0].set(b_router_group).at[:, 8:8 + N_EXPERTS, 0].set(b_router_exp)
    w_gate_b, w_branch_b, w_out_b = w_gate.astype(BF16), w_branch.astype(BF16), w_out.astype(BF16)
    w_eg, w_eu, w_ed = w_exp_gate.astype(BF16), w_exp_up.astype(BF16), w_exp_down.astype(BF16)

    cos_p, sin_p = _rope_tables(jnp.arange(seq, dtype=F32))
    cos_s, sin_s = _rope_tables(PAST_LEN + jnp.arange(dseq, dtype=F32))
    cos_s, sin_s = jnp.tile(cos_s, (bs, 1)), jnp.tile(sin_s, (bs, 1))

    cfg_p = dict(b=bp, tseq=seq, tm=512, chunk=CHUNK, tc=512, tme=512)
    cfg_s = dict(b=bs, tseq=dseq, tm=n_s, chunk=min(dseq, CHUNK), tc=dseq, tme=n_s)
    zero_state = jnp.zeros((bp, RET_HEADS, HEAD_DK, HEAD_DK), F32)
    sorted_p = jnp.zeros((_sorted_rows(bp * seq, cfg_p["tme"]), d), BF16)
    sorted_s = jnp.zeros((_sorted_rows(n_s, cfg_s["tme"]), d), BF16)

    xp = x_prompt
    xs = x_sample.reshape(1, n_s, d)
    outs = [[] for _ in range(8)]
    for l in range(depth):
        lw = dict(
            g_mix=g_mix[l][None], g_ffn=g_ffn[l][None], w_in=w_in_b[l], w_ga=w_ga[l], a2=a2[l],
            a_bias=gla_a_bias[l][None], gq=jnp.tile(g_q_att[l], ATT_HEADS)[None],
            gk=jnp.tile(g_k_att[l], ATT_HEADS)[None], g_ret_gn=g_ret_gn[l][None], g_gla_gn=g_gla_gn[l][None],
            rel_bias=rel_bias[l], w_gate=w_gate_b[l], b_gate=b_gate[l][None], w_branch=w_branch_b[l],
            w_out=w_out_b[l], wr_t=wr_t[l], br_t=br_t[l], w_eg=w_eg[l], w_eu=w_eu[l], w_ed=w_ed[l])
        mods_p = tuple(m[:, None, :] for m in jnp.split(mod[l, :bp], 6, axis=-1))
        mods_s = tuple(jnp.repeat(m, dseq, axis=0)[None] for m in jnp.split(mod[l, bp:bp + bs], 6, axis=-1))
        xp, rp, gp, kp, vp, sorted_p = _layer(xp, mods_p, lw, cfg_p, zero_state, zero_state, None, None,
                                              cos_p, sin_p, sorted_p)
        ck = cache_att_k[l].reshape(bs, BAND_PAST, BRANCH_WIDTH)
        cv = cache_att_v[l].reshape(bs, BAND_PAST, BRANCH_WIDTH)
        xs, rs, gs, ks, vs, sorted_s = _layer(xs, mods_s, lw, cfg_s, state_ret[l], state_gla[l], ck, cv,
                                              cos_s, sin_s, sorted_s)
        heads = lambda a: a.astype(F32).reshape(a.shape[0], a.shape[1], ATT_HEADS, ATT_HD)
        for lst, val in zip(outs, (rp, gp, heads(kp), heads(vp), rs, gs, heads(ks), heads(vs))):
            lst.append(val)
    return (xp, xs.reshape(bs, dseq, d)) + tuple(jnp.stack(o) for o in outs)
```

```python
import functools

import numpy as np
import jax
import jax.numpy as jnp
from jax import lax
from jax.experimental import pallas as pl
from jax.experimental.pallas import tpu as pltpu

F32 = jnp.float32
BF16 = jnp.bfloat16

D_MODEL = 1024
CHUNK = 64
BRANCH_WIDTH = 512
N_BRANCH = 3
RET_HEADS = 4
HEAD_DK = 128
GLA_RANK = 16
GLA_TAU = 16.0
ATT_HEADS = 8
ATT_HD = 64
BAND_PAST = 512
MAX_REL = 128
N_GROUPS = 4
EXPERTS_PER_GROUP = 4
N_EXPERTS = 16
D_EXPERT = 512
ROPE_BASE = 10000.0
EPS = 1e-6
GN_EPS = 1e-5
PAST_LEN = 1024
N_PROJ_BLOCKS = 11
PROJ_COLS = N_PROJ_BLOCKS * BRANCH_WIDTH

V7X_LANES = 128
V7X_VMEM_BYTES = 64 * 1024 * 1024
VMEM_LIMIT = 56 * 1024 * 1024
NEG_BIG = -1e30


def _cparams(sem, vmem=None):
    return pltpu.CompilerParams(dimension_semantics=sem, vmem_limit_bytes=vmem)


def _const_spec(shape):
    nd = len(shape)
    return pl.BlockSpec(shape, lambda *_: (0,) * nd)


def _dot(a, b):
    return jnp.dot(a, b, preferred_element_type=F32)


def _dot_nt(a, b):
    return lax.dot_general(a, b, (((1,), (1,)), ((), ())), preferred_element_type=F32)


def _dot_tn(a, b):
    return lax.dot_general(a, b, (((0,), (0,)), ((), ())), preferred_element_type=F32)


def _silu(x):
    return x * jax.nn.sigmoid(x)


def _norm_mod(x, g, sc, sh):
    ms = jnp.mean(x * x, axis=-1, keepdims=True)
    return x * lax.rsqrt(ms + EPS) * g * (1.0 + sc) + sh


def _ada_kernel(c_ref, w_ref, b_ref, o_ref):
    s = _silu(c_ref[...])
    o_ref[0] = _dot(s.astype(BF16), w_ref[0].astype(BF16)) + b_ref[0]


def _ada_mod(c_all, w_ada, b_ada):
    depth, d, n = w_ada.shape
    rows = c_all.shape[0]
    bn = 1536
    return pl.pallas_call(
        _ada_kernel,
        out_shape=jax.ShapeDtypeStruct((depth, rows, n), F32),
        grid=(depth, n // bn),
        in_specs=[
            pl.BlockSpec((rows, d), lambda l, j: (0, 0)),
            pl.BlockSpec((1, d, bn), lambda l, j: (l, 0, j)),
            pl.BlockSpec((1, 1, bn), lambda l, j: (l, 0, j)),
        ],
        out_specs=pl.BlockSpec((1, rows, bn), lambda l, j: (l, 0, j)),
        compiler_params=_cparams(("arbitrary", "arbitrary"), VMEM_LIMIT),
        name="ada_mod",
    )(c_all, w_ada, b_ada.reshape(depth, 1, n))


def _rope_heads(a, cosf, sinf):
    outs = []
    for h in range(RET_HEADS):
        ah = a[:, h * HEAD_DK:(h + 1) * HEAD_DK]
        outs.append(ah * cosf + pltpu.roll(ah, HEAD_DK // 2, 1) * sinf)
    return jnp.concatenate(outs, axis=1)


def _rms_heads64(a, gain):
    low = lax.broadcasted_iota(jnp.int32, (1, V7X_LANES), 1) < ATT_HD
    outs = []
    for c in range(a.shape[1] // V7X_LANES):
        ac = a[:, c * V7X_LANES:(c + 1) * V7X_LANES]
        sq = ac * ac
        lo = jnp.sum(jnp.where(low, sq, 0.0), axis=-1, keepdims=True)
        hi = jnp.sum(jnp.where(low, 0.0, sq), axis=-1, keepdims=True)
        ms = jnp.where(low, lo, hi) * (1.0 / ATT_HD)
        outs.append(ac * lax.rsqrt(ms + EPS))
    return jnp.concatenate(outs, axis=1) * gain


def _in_proj_kernel(x_ref, sc_ref, sh_ref, g_ref, w_ref, wga_ref, a2_ref, ab_ref, cos_ref, sin_ref,
                    gq_ref, gk_ref, proj_ref, la_ref):
    x = x_ref[0]
    hb = _norm_mod(x, g_ref[...], sc_ref[0], sh_ref[0]).astype(BF16)
    cosf = cos_ref[...]
    sinf = sin_ref[...]
    for j in range(N_PROJ_BLOCKS):
        cols = slice(j * BRANCH_WIDTH, (j + 1) * BRANCH_WIDTH)
        acc = _dot(hb, w_ref[:, cols])
        if j == 0:
            acc = _rope_heads(acc, cosf, sinf)
        elif j == 1:
            acc = _rope_heads(acc, cosf, sinf) * (HEAD_DK ** -0.5)
        elif j == 4:
            acc = acc * (HEAD_DK ** -0.5)
        elif j == 8:
            acc = _rms_heads64(acc, gq_ref[...]) * (ATT_HD ** -0.5)
        elif j == 9:
            acc = _rms_heads64(acc, gk_ref[...])
        proj_ref[0, :, cols] = acc.astype(BF16)
    ga = _dot(hb, wga_ref[...])
    z = _dot(ga.astype(BF16), a2_ref[...]) + ab_ref[...]
    la_ref[0] = jax.nn.log_sigmoid(z) * (1.0 / GLA_TAU)


def _in_proj(x, sc, sh, g_mix, w_in, w_ga, a2, a_bias, cosf, sinf, gq, gk, tm):
    g, t, d = x.shape
    rm = sc.shape[1]
    mod_spec = pl.BlockSpec((1, rm, d), (lambda b, i: (b, 0, 0)) if rm == 1 else (lambda b, i: (b, i, 0)))
    return pl.pallas_call(
        _in_proj_kernel,
        out_shape=(jax.ShapeDtypeStruct((g, t, PROJ_COLS), BF16),
                   jax.ShapeDtypeStruct((g, t, BRANCH_WIDTH), F32)),
        grid=(g, t // tm),
        in_specs=[
            pl.BlockSpec((1, tm, d), lambda b, i: (b, i, 0)),
            mod_spec, mod_spec,
            _const_spec((1, d)),
            _const_spec((d, PROJ_COLS)),
            _const_spec((d, V7X_LANES)),
            _const_spec((V7X_LANES, BRANCH_WIDTH)),
            _const_spec((1, BRANCH_WIDTH)),
            pl.BlockSpec((tm, HEAD_DK), lambda b, i: (i, 0)),
            pl.BlockSpec((tm, HEAD_DK), lambda b, i: (i, 0)),
            _const_spec((1, BRANCH_WIDTH)),
            _const_spec((1, BRANCH_WIDTH)),
        ],
        out_specs=(pl.BlockSpec((1, tm, PROJ_COLS), lambda b, i: (b, i, 0)),
                   pl.BlockSpec((1, tm, BRANCH_WIDTH), lambda b, i: (b, i, 0))),
        compiler_params=_cparams(("arbitrary", "arbitrary"), VMEM_LIMIT),
        name="in_proj",
    )(x, sc, sh, g_mix, w_in, w_ga, a2, a_bias, cosf, sinf, gq, gk)


def _retention_kernel(q_ref, k_ref, v_ref, g_ref, s0_ref, dm_ref, qd_ref, kd_ref, bd_ref, gn_ref,
                      o_ref, so_ref, st_ref, *, chunk, n_chunks):
    @pl.when(pl.program_id(1) == 0)
    def _():
        st_ref[...] = s0_ref[0]

    def body(c, carry):
        rows = pl.ds(pl.multiple_of(c * chunk, chunk), chunk)
        for h in range(RET_HEADS):
            cols = slice(h * HEAD_DK, (h + 1) * HEAD_DK)
            q = q_ref[0, rows, cols]
            k = k_ref[0, rows, cols]
            v = v_ref[0, rows, cols]
            scores = _dot_nt(q, k) * dm_ref[h]
            o = _dot(scores.astype(BF16), v)
            o = o + _dot((q.astype(F32) * qd_ref[h]).astype(BF16), st_ref[h].astype(BF16))
            st_ref[h] = st_ref[h] * bd_ref[h] + _dot_tn((k.astype(F32) * kd_ref[h]).astype(BF16), v)
            mu = jnp.mean(o, axis=-1, keepdims=True)
            oc = o - mu
            var = jnp.mean(oc * oc, axis=-1, keepdims=True)
            on = oc * lax.rsqrt(var + GN_EPS) * gn_ref[:, cols]
            o_ref[0, rows, cols] = (_silu(g_ref[0, rows, cols].astype(F32)) * on).astype(BF16)
        return carry

    lax.fori_loop(0, n_chunks, body, 0)

    @pl.when(pl.program_id(1) == pl.num_programs(1) - 1)
    def _():
        so_ref[0] = st_ref[...]


def _retention_tables(chunk):
    log_gamma = jnp.log(1.0 - jnp.exp2(-5.0 - jnp.arange(RET_HEADS, dtype=F32)))
    idx = jnp.arange(chunk, dtype=F32)
    diff = idx[:, None] - idx[None, :]
    dmask = jnp.where(diff >= 0, jnp.exp(log_gamma[:, None, None] * jnp.maximum(diff, 0.0)), 0.0)
    qd = jnp.exp(log_gamma[:, None] * (idx + 1.0))[:, :, None]
    kd = jnp.exp(log_gamma[:, None] * (chunk - 1.0 - idx))[:, :, None]
    bd = jnp.exp(log_gamma * chunk)[:, None, None]
    bc = lambda a, r: jnp.broadcast_to(a, (RET_HEADS, r, HEAD_DK)).astype(F32)
    return dmask.astype(F32), bc(qd, chunk), bc(kd, chunk), bc(bd, 1)


def _retention(proj, state0, gn, chunk, tc):
    b, t, _ = proj.shape
    dm, qd, kd, bd = _retention_tables(chunk)
    col = lambda j: pl.BlockSpec((1, tc, BRANCH_WIDTH), lambda i, s, j=j: (i, s, j))
    st_spec = pl.BlockSpec((1, RET_HEADS, HEAD_DK, HEAD_DK), lambda i, s: (i, 0, 0, 0))
    return pl.pallas_call(
        functools.partial(_retention_kernel, chunk=chunk, n_chunks=tc // chunk),
        out_shape=(jax.ShapeDtypeStruct((b, t, BRANCH_WIDTH), BF16),
                   jax.ShapeDtypeStruct((b, RET_HEADS, HEAD_DK, HEAD_DK), F32)),
        grid=(b, t // tc),
        in_specs=[col(0), col(1), col(2), col(3), st_spec,
                  _const_spec(dm.shape), _const_spec(qd.shape), _const_spec(kd.shape), _const_spec(bd.shape),
                  _const_spec((1, BRANCH_WIDTH))],
        out_specs=(pl.BlockSpec((1, tc, BRANCH_WIDTH), lambda i, s: (i, s, 0)), st_spec),
        scratch_shapes=[pltpu.VMEM((RET_HEADS, HEAD_DK, HEAD_DK), F32)],
        compiler_params=_cparams(("arbitrary", "arbitrary")),
        name="retention",
    )(proj, proj, proj, proj, state0, dm, qd, kd, bd, gn)


GLA_SAFE_LOG_DECAY = -60.0


def _gla_intra_direct(q, bh, rows_ref, chunk):
    trow = lax.broadcasted_iota(jnp.int32, (chunk, 1), 0)

    def step(s, acc):
        bs = rows_ref[0, pl.ds(s, 1), :]
        ks = rows_ref[1, pl.ds(s, 1), :]
        vs = rows_ref[2, pl.ds(s, 1), :]
        e = jnp.exp(jnp.minimum(bh - bs, 0.0))
        col = jnp.sum(q * ks * e, axis=-1, keepdims=True)
        return acc + jnp.where(trow >= s, col, 0.0) * vs

    return lax.fori_loop(0, chunk, step, jnp.zeros((chunk, HEAD_DK), F32))


def _gla_kernel(q_ref, k_ref, v_ref, r_ref, la_ref, s0_ref, tri_ref, gn_ref, o_ref, so_ref, st_ref, rows_ref,
                *, chunk, n_chunks):
    @pl.when(pl.program_id(1) == 0)
    def _():
        for h in range(RET_HEADS):
            st_ref[h] = s0_ref[0, h].T

    tri = tri_ref[...]
    causal = (lax.broadcasted_iota(jnp.int32, (chunk, chunk), 0)
              >= lax.broadcasted_iota(jnp.int32, (chunk, chunk), 1))

    def chunk_body(factored):
        def body(c, carry):
            rows = pl.ds(pl.multiple_of(c * chunk, chunk), chunk)
            la = la_ref[0, rows, :]
            la_hi = la.astype(BF16)
            la_lo = (la - la_hi.astype(F32)).astype(BF16)
            b_all = _dot(tri, la_hi) + _dot(tri, la_lo)
            for h in range(RET_HEADS):
                cols = slice(h * HEAD_DK, (h + 1) * HEAD_DK)
                q = q_ref[0, rows, cols].astype(F32)
                k = k_ref[0, rows, cols].astype(F32)
                v = v_ref[0, rows, cols]
                bh = b_all[:, cols]
                bl = bh[chunk - 1:chunk, :]
                qh = (q * jnp.exp(bh)).astype(BF16)
                if factored:
                    kh = (k * jnp.exp(-bh)).astype(BF16)
                    scores = jnp.where(causal, _dot_nt(qh, kh), 0.0)
                    o = _dot(scores.astype(BF16), v)
                else:
                    rows_ref[0] = bh
                    rows_ref[1] = k
                    rows_ref[2] = v.astype(F32)
                    o = _gla_intra_direct(q, bh, rows_ref, chunk)
                o = o + _dot_nt(qh, st_ref[h].astype(BF16))
                kd = (k * jnp.exp(bl - bh)).astype(BF16)
                st_ref[h] = st_ref[h] * jnp.exp(bl) + _dot_tn(v, kd)
                ms = jnp.mean(o * o, axis=-1, keepdims=True)
                on = o * lax.rsqrt(ms + EPS) * gn_ref[:, cols]
                o_ref[0, rows, cols] = (_silu(r_ref[0, rows, cols].astype(F32)) * on).astype(BF16)
            return carry
        return body

    lowest = None
    for c in range(n_chunks):
        tot = jnp.sum(la_ref[0, c * chunk:(c + 1) * chunk, :], axis=0, keepdims=True)
        lowest = tot if lowest is None else jnp.minimum(lowest, tot)
    safe = jnp.min(lowest) > GLA_SAFE_LOG_DECAY

    @pl.when(safe)
    def _():
        lax.fori_loop(0, n_chunks, chunk_body(True), 0)

    @pl.when(jnp.logical_not(safe))
    def _():
        lax.fori_loop(0, n_chunks, chunk_body(False), 0)

    @pl.when(pl.program_id(1) == pl.num_programs(1) - 1)
    def _():
        for h in range(RET_HEADS):
            so_ref[0, h] = st_ref[h].T


def _gla(proj, log_a, state0, gn, chunk, tc):
    b, t, _ = proj.shape
    tri = jnp.tril(jnp.ones((chunk, chunk), F32)).astype(BF16)
    col = lambda j: pl.BlockSpec((1, tc, BRANCH_WIDTH), lambda i, s, j=j: (i, s, j))
    st_spec = pl.BlockSpec((1, RET_HEADS, HEAD_DK, HEAD_DK), lambda i, s: (i, 0, 0, 0))
    return pl.pallas_call(
        functools.partial(_gla_kernel, chunk=chunk, n_chunks=tc // chunk),
        out_shape=(jax.ShapeDtypeStruct((b, t, BRANCH_WIDTH), BF16),
                   jax.ShapeDtypeStruct((b, RET_HEADS, HEAD_DK, HEAD_DK), F32)),
        grid=(b, t // tc),
        in_specs=[col(4), col(5), col(6), col(7),
                  pl.BlockSpec((1, tc, BRANCH_WIDTH), lambda i, s: (i, s, 0)),
                  st_spec, _const_spec((chunk, chunk)), _const_spec((1, BRANCH_WIDTH))],
        out_specs=(pl.BlockSpec((1, tc, BRANCH_WIDTH), lambda i, s: (i, s, 0)), st_spec),
        scratch_shapes=[pltpu.VMEM((RET_HEADS, HEAD_DK, HEAD_DK), F32),
                        pltpu.VMEM((3, chunk, HEAD_DK), F32)],
        compiler_params=_cparams(("arbitrary", "arbitrary")),
        name="gla",
    )(proj, proj, proj, proj, log_a, state0, tri, gn)


def _band_attn_kernel(q_ref, kp_ref, kc_ref, vp_ref, vc_ref, bias_ref, o_ref, kw_ref, vw_ref,
                      *, n_sub, rb, win, prev_rows, mask_start):
    cur_rows = kc_ref.shape[1]
    kw_ref[0:prev_rows, :] = kp_ref[0].astype(BF16)
    kw_ref[prev_rows:prev_rows + cur_rows, :] = kc_ref[0]
    vw_ref[0:prev_rows, :] = vp_ref[0].astype(BF16)
    vw_ref[prev_rows:prev_rows + cur_rows, :] = vc_ref[0]
    low = lax.broadcasted_iota(jnp.int32, (1, V7X_LANES), 1) < ATT_HD
    q0 = pl.program_id(1) * cur_rows

    def body(i, carry):
        start = pl.multiple_of(i * rb, rb)
        qrows = pl.ds(start, rb)
        wrows = pl.ds(start, win)
        if mask_start:
            valid = lax.broadcasted_iota(jnp.int32, (1, win), 1) >= prev_rows - q0 - start
        for p in range(ATT_HEADS // 2):
            cols = slice(p * V7X_LANES, (p + 1) * V7X_LANES)
            q2 = q_ref[0, qrows, cols]
            zero = jnp.zeros_like(q2)
            qst = jnp.concatenate([jnp.where(low, q2, zero), jnp.where(low, zero, q2)], axis=0)
            s = _dot_nt(qst, kw_ref[wrows, cols]) + bias_ref[p]
            if mask_start:
                s = jnp.where(valid, s, NEG_BIG)
            m = jnp.max(s, axis=-1, keepdims=True)
            e = jnp.exp(s - m)
            den = jnp.sum(e, axis=-1, keepdims=True)
            o2 = _dot(e.astype(BF16), vw_ref[wrows, cols]) / den
            o_ref[0, qrows, cols] = jnp.where(low, o2[:rb], o2[rb:]).astype(BF16)
        return carry

    lax.fori_loop(0, n_sub, body, 0)


def _rel_bias_table(rel_bias, rb, win, chunk):
    period = win + rb
    u = np.arange(period)
    u = np.where(u < win, u, u - period)
    vec = rel_bias.astype(F32)[:, np.clip(BAND_PAST - u, -MAX_REL, MAX_REL) + MAX_REL]
    bias = jnp.tile(vec, (1, rb))[:, :rb * (period - 1)].reshape(-1, rb, period - 1)[:, :, :win]
    r = np.arange(rb)[:, None]
    j = np.arange(win)[None, :]
    c0 = (r // chunk) * chunk
    allowed = (j >= c0) & (j < c0 + BAND_PAST + chunk)
    bias = jnp.where(jnp.asarray(allowed)[None], bias, NEG_BIG)
    return bias.reshape(ATT_HEADS // 2, 2 * rb, win)


def _band_attn_prompt(proj, rel_bias):
    b, t, _ = proj.shape
    qb, rb = 512, 128
    win = BAND_PAST + rb
    bias = _rel_bias_table(rel_bias, rb, win, CHUNK)
    cur = lambda j: pl.BlockSpec((1, qb, BRANCH_WIDTH), lambda i, s, j=j: (i, s, j))
    prev = lambda j: pl.BlockSpec((1, qb, BRANCH_WIDTH), lambda i, s, j=j: (i, jnp.maximum(s - 1, 0), j))
    return pl.pallas_call(
        functools.partial(_band_attn_kernel, n_sub=qb // rb, rb=rb, win=win, prev_rows=BAND_PAST,
                          mask_start=True),
        out_shape=jax.ShapeDtypeStruct((b, t, BRANCH_WIDTH), BF16),
        grid=(b, t // qb),
        in_specs=[cur(8), prev(9), cur(9), prev(10), cur(10), _const_spec(bias.shape)],
        out_specs=pl.BlockSpec((1, qb, BRANCH_WIDTH), lambda i, s: (i, s, 0)),
        scratch_shapes=[pltpu.VMEM((BAND_PAST + qb, BRANCH_WIDTH), BF16),
                        pltpu.VMEM((BAND_PAST + qb, BRANCH_WIDTH), BF16)],
        compiler_params=_cparams(("arbitrary", "arbitrary")),
        name="band_attn",
    )(proj, proj, proj, proj, proj, bias)


def _band_attn_sample(proj, cache_k, cache_v, rel_bias):
    b, t, _ = proj.shape
    win = BAND_PAST + t
    bias = _rel_bias_table(rel_bias, t, win, t)
    cur = lambda j: pl.BlockSpec((1, t, BRANCH_WIDTH), lambda i, s, j=j: (i, 0, j))
    cache = pl.BlockSpec((1, BAND_PAST, BRANCH_WIDTH), lambda i, s: (i, 0, 0))
    return pl.pallas_call(
        functools.partial(_band_attn_kernel, n_sub=1, rb=t, win=win, prev_rows=BAND_PAST, mask_start=False),
        out_shape=jax.ShapeDtypeStruct((b, t, BRANCH_WIDTH), BF16),
        grid=(b, 1),
        in_specs=[cur(8), cache, cur(9), cache, cur(10), _const_spec(bias.shape)],
        out_specs=pl.BlockSpec((1, t, BRANCH_WIDTH), lambda i, s: (i, 0, 0)),
        scratch_shapes=[pltpu.VMEM((win, BRANCH_WIDTH), BF16), pltpu.VMEM((win, BRANCH_WIDTH), BF16)],
        compiler_params=_cparams(("arbitrary", "arbitrary")),
        name="band_attn_sample",
    )(proj, cache_k, proj, cache_v, proj, bias)


def _merge_kernel(x_ref, or_ref, og_ref, oa_ref, scm_ref, shm_ref, gtm_ref, scf_ref, shf_ref,
                  gmix_ref, gffn_ref, wg_ref, bg_ref, wb_ref, wo_ref, xo_ref, h2_ref):
    x = x_ref[0]
    d = x.shape[1]
    hb = _norm_mod(x, gmix_ref[...], scm_ref[0], shm_ref[0]).astype(BF16)
    merged = None
    for n, o_ref in enumerate((or_ref, og_ref, oa_ref)):
        gate = jax.nn.sigmoid(_dot(hb, wg_ref[:, n * d:(n + 1) * d]) + bg_ref[:, n * d:(n + 1) * d])
        y = gate * _dot(o_ref[0], wb_ref[n])
        merged = y if merged is None else merged + y
    mix = _dot(merged.astype(BF16), wo_ref[...])
    xn = x + gtm_ref[0] * mix
    xo_ref[0] = xn
    h2_ref[0] = _norm_mod(xn, gffn_ref[...], scf_ref[0], shf_ref[0]).astype(BF16)


def _merge(x, o_r, o_g, o_a, mods, g_mix, g_ffn, w_gate, b_gate, w_branch, w_out, tm):
    g, t, d = x.shape
    rm = mods[0].shape[1]
    mod_spec = pl.BlockSpec((1, rm, d), (lambda b, i: (b, 0, 0)) if rm == 1 else (lambda b, i: (b, i, 0)))
    tok = lambda w: pl.BlockSpec((1, tm, w), lambda b, i: (b, i, 0))
    return pl.pallas_call(
        _merge_kernel,
        out_shape=(jax.ShapeDtypeStruct((g, t, d), F32), jax.ShapeDtypeStruct((g, t, d), BF16)),
        grid=(g, t // tm),
        in_specs=[tok(d), tok(BRANCH_WIDTH), tok(BRANCH_WIDTH), tok(BRANCH_WIDTH)] + [mod_spec] * 5 + [
            _const_spec((1, d)), _const_spec((1, d)),
            _const_spec((d, N_BRANCH * d)), _const_spec((1, N_BRANCH * d)),
            _const_spec((N_BRANCH, BRANCH_WIDTH, d)), _const_spec((d, d))],
        out_specs=(tok(d), tok(d)),
        compiler_params=_cparams(("arbitrary", "arbitrary"), VMEM_LIMIT),
        name="merge",
    )(x, o_r, o_g, o_a, *mods, g_mix, g_ffn, w_gate, b_gate, w_branch, w_out)


def _first_argmax(vals, n):
    row = lax.broadcasted_iota(jnp.int32, vals.shape, 0).astype(F32)
    m = jnp.max(vals, axis=0, keepdims=True)
    idx = jnp.min(jnp.where(vals == m, row, float(n)), axis=0, keepdims=True)
    return m, idx


def _route_kernel(h_ref, wr_ref, br_ref, up_ref, l16_ref, info_ref, tbl_ref, *, rows_per_block):
    tm = h_ref.shape[0]
    lg = _dot_nt(wr_ref[...], h_ref[...]) + br_ref[...]
    g = lg[0:N_GROUPS]
    gmax, grp = _first_argmax(g, N_GROUPS)
    p_group = 1.0 / jnp.sum(jnp.exp(g - gmax), axis=0, keepdims=True)
    esel = jnp.zeros((EXPERTS_PER_GROUP, tm), F32)
    for gi in range(N_GROUPS):
        blk = lg[8 + gi * EXPERTS_PER_GROUP:8 + (gi + 1) * EXPERTS_PER_GROUP]
        esel = esel + jnp.where(grp == float(gi), blk, 0.0)
    v1, i1 = _first_argmax(esel, EXPERTS_PER_GROUP)
    row4 = lax.broadcasted_iota(jnp.int32, esel.shape, 0).astype(F32)
    v2, i2 = _first_argmax(jnp.where(row4 == i1, -jnp.inf, esel), EXPERTS_PER_GROUP)
    e21 = jnp.exp(v2 - v1)
    w1 = p_group / (1.0 + e21)
    w2 = p_group * e21 / (1.0 + e21)
    e1 = grp * float(EXPERTS_PER_GROUP) + i1
    e2 = grp * float(EXPERTS_PER_GROUP) + i2
    row16 = lax.broadcasted_iota(jnp.int32, (N_EXPERTS, tm), 0).astype(F32)
    hit1 = row16 == e1
    hit2 = row16 == e2
    onehot = jnp.where(hit1 | hit2, 1.0, 0.0)
    prefix = _dot(onehot.astype(BF16), up_ref[...])
    cnt = jnp.sum(onehot, axis=1, keepdims=True)
    nblk = jnp.floor((cnt + float(rows_per_block - 1)) * (1.0 / rows_per_block))
    nblk_b = jnp.broadcast_to(nblk, (N_EXPERTS, V7X_LANES))
    offb = _dot(l16_ref[...], nblk_b.astype(BF16))[:, 0:1]
    base = offb * float(rows_per_block) + prefix
    info_ref[0:1, :] = jnp.sum(jnp.where(hit1, base, 0.0), axis=0, keepdims=True)
    info_ref[1:2, :] = jnp.sum(jnp.where(hit2, base, 0.0), axis=0, keepdims=True)
    info_ref[2:3, :] = w1
    info_ref[3:4, :] = w2
    info_ref[4:8, :] = jnp.zeros((4, tm), F32)
    tbl_ref[0] = nblk_b.astype(jnp.int32)


def _route(h2, wr_t, br_t, tm, rows_per_block):
    n, d = h2.shape
    nt = n // tm
    upper = jnp.triu(jnp.ones((tm, tm), F32), 1).astype(BF16)
    l16 = jnp.tril(jnp.ones((N_EXPERTS, N_EXPERTS), F32), -1).astype(BF16)
    return pl.pallas_call(
        functools.partial(_route_kernel, rows_per_block=rows_per_block),
        out_shape=(jax.ShapeDtypeStruct((8, n), F32),
                   jax.ShapeDtypeStruct((nt, N_EXPERTS, V7X_LANES), jnp.int32)),
        grid=(nt,),
        in_specs=[pl.BlockSpec((tm, d), lambda i: (i, 0)),
                  _const_spec((32, d)), _const_spec((32, 1)),
                  _const_spec((tm, tm)), _const_spec((N_EXPERTS, N_EXPERTS))],
        out_specs=(pl.BlockSpec((8, tm), lambda i: (0, i)),
                   pl.BlockSpec((1, N_EXPERTS, V7X_LANES), lambda i: (i, 0, 0))),
        compiler_params=_cparams(("arbitrary",)),
        name="route",
    )(h2, wr_t, br_t, upper, l16)


SEG = 16
GATHER_ROWS = 256
MAX_EXPERT_ROWS = 512


def _expert_rows(n_tokens):
    mean_rows = 2 * n_tokens // N_EXPERTS
    return int(min(MAX_EXPERT_ROWS, max(4 * SEG, 1 << (mean_rows // 4).bit_length())))


def _segment_copies(cnt_ref, loff_ref, goff_ref, tile, local_ref, global_ref, sem, to_global):
    for e in range(N_EXPERTS):
        cnt = cnt_ref[tile * N_EXPERTS + e]
        rows = pl.multiple_of(cnt * SEG, SEG)
        loc = local_ref.at[pl.ds(pl.multiple_of(loff_ref[tile * N_EXPERTS + e] * SEG, SEG), rows)]
        glo = global_ref.at[pl.ds(pl.multiple_of(goff_ref[tile * N_EXPERTS + e] * SEG, SEG), rows)]
        cp = pltpu.make_async_copy(loc, glo, sem) if to_global else pltpu.make_async_copy(glo, loc, sem)
        yield cnt > 0, cp


def _tile_units(cnt_ref, tile):
    total = jnp.int32(0)
    for e in range(N_EXPERTS):
        total = total + cnt_ref[tile * N_EXPERTS + e]
    return total


def _sort_kernel(cnt_ref, loff_ref, goff_ref, h_ref, irow_ref, xin_ref, xout_ref, xs2_ref, sem2):
    del xin_ref
    i = pl.program_id(0)
    last = pl.num_programs(0) - 1
    slot = lax.rem(i, 2)
    xs_ref = xs2_ref.at[slot]
    sem = sem2.at[slot]

    def wait_tile(tile, s):
        for pred, cp in _segment_copies(cnt_ref, loff_ref, goff_ref, tile, xs2_ref.at[s], xout_ref, sem2.at[s], True):
            pl.when(pred)(cp.wait)

    @pl.when(i >= 2)
    def _():
        wait_tile(i - 2, slot)

    gr = GATHER_ROWS
    n_gather = (_tile_units(cnt_ref, i) * SEG + gr - 1) // gr
    pos1 = irow_ref[0:1, :]
    pos2 = irow_ref[1:2, :]
    h = h_ref[...]

    def gather(gb, carry):
        r0 = pl.multiple_of(gb * gr, gr)
        srow = (lax.broadcasted_iota(jnp.int32, (gr, 1), 0) + r0).astype(F32)
        sel = jnp.where((srow == pos1) | (srow == pos2), 1.0, 0.0).astype(BF16)
        xs_ref[pl.ds(r0, gr), :] = _dot(sel, h).astype(BF16)
        return carry

    lax.fori_loop(0, n_gather, gather, 0)
    for pred, cp in _segment_copies(cnt_ref, loff_ref, goff_ref, i, xs_ref, xout_ref, sem, True):
        pl.when(pred)(cp.start)

    @pl.when(i == last)
    def _():
        @pl.when(i >= 1)
        def _():
            wait_tile(i - 1, 1 - slot)

        wait_tile(i, slot)


def _experts_kernel(be_ref, nv_ref, x_ref, wg_ref, wu_ref, wd_ref, o_ref):
    del be_ref
    used = pl.program_id(0) < nv_ref[0]

    @pl.when(used)
    def _():
        xb = x_ref[...]
        a = _silu(_dot(xb, wg_ref[0])) * _dot(xb, wu_ref[0])
        o_ref[...] = _dot(a.astype(BF16), wd_ref[0]).astype(BF16)

    @pl.when(jnp.logical_not(used))
    def _():
        o_ref[...] = jnp.zeros(o_ref.shape, BF16)


def _combine_kernel(cnt_ref, loff_ref, goff_ref, ds_ref, icol_ref, x_ref, gt_ref, o_ref, dl2_ref, sem2, *, rb):
    i = pl.program_id(0)
    tm = x_ref.shape[1]
    s_loc = dl2_ref.shape[1]
    slot = lax.rem(i, 2)

    def fetch_tile(tile, s):
        for pred, cp in _segment_copies(cnt_ref, loff_ref, goff_ref, tile, dl2_ref.at[s], ds_ref, sem2.at[s], False):
            pl.when(pred)(cp.start)

        def zero_tail(u, carry):
            dl2_ref[s, pl.ds(pl.multiple_of(u * SEG, SEG), SEG), :] = jnp.zeros((SEG, dl2_ref.shape[2]), BF16)
            return carry

        lax.fori_loop(_tile_units(cnt_ref, tile), s_loc // SEG, zero_tail, 0)

    @pl.when(i == 0)
    def _():
        fetch_tile(0, 0)

    @pl.when(i + 1 < pl.num_programs(0))
    def _():
        fetch_tile(i + 1, 1 - slot)

    for pred, cp in _segment_copies(cnt_ref, loff_ref, goff_ref, i, dl2_ref.at[slot], ds_ref, sem2.at[slot], False):
        pl.when(pred)(cp.wait)
    scol = lax.broadcasted_iota(jnp.int32, (1, s_loc), 1).astype(F32)
    for r in range(tm // rb):
        rows = slice(r * rb, (r + 1) * rb)
        cmb = (jnp.where(scol == icol_ref[rows, 0:1], icol_ref[rows, 2:3], 0.0)
               + jnp.where(scol == icol_ref[rows, 1:2], icol_ref[rows, 3:4], 0.0))
        y = _dot(cmb.astype(BF16), dl2_ref[slot])
        gt = gt_ref[0] if gt_ref.shape[1] == 1 else gt_ref[0, rows, :]
        o_ref[0, rows, :] = x_ref[0, rows, :] + gt * y


def _sorted_rows(n_tokens, tm):
    n_tiles = n_tokens // tm
    er = _expert_rows(n_tokens)
    rows = 2 * n_tokens + n_tiles * N_EXPERTS * (SEG - 1) + N_EXPERTS * (er - 1)
    return -(-rows // er) * er


def _moe(cnt, h2, x, gt, info_row, info_col, w_eg, w_eu, w_ed, x_sorted, tm):
    g, t, d = x.shape
    n = g * t
    n_tiles = n // tm
    tpg = t // tm
    rm = gt.shape[1]
    er = _expert_rows(n)
    n_blocks = x_sorted.shape[0] // er
    s_loc = -(-(2 * tm + N_EXPERTS * (SEG - 1)) // GATHER_ROWS) * GATHER_ROWS
    upb = er // SEG
    reg_blk = (jnp.sum(cnt, axis=0) + upb - 1) // upb
    blk_end = jnp.cumsum(reg_blk)
    goff = ((blk_end - reg_blk)[None, :] * upb + jnp.cumsum(cnt, axis=0) - cnt).reshape(-1)
    loff = (jnp.cumsum(cnt, axis=1) - cnt).reshape(-1)
    cntf = cnt.reshape(-1)
    n_valid = blk_end[-1:]
    blk_expert = jnp.minimum(
        jnp.sum(jnp.arange(n_blocks, dtype=jnp.int32)[:, None] >= blk_end[None, :], axis=1), N_EXPERTS - 1
    ).astype(jnp.int32)

    x_sorted = pl.pallas_call(
        _sort_kernel,
        out_shape=jax.ShapeDtypeStruct(x_sorted.shape, BF16),
        grid_spec=pltpu.PrefetchScalarGridSpec(
            num_scalar_prefetch=3,
            grid=(n_tiles,),
            in_specs=[pl.BlockSpec((tm, d), lambda i, *_: (i, 0)),
                      pl.BlockSpec((8, tm), lambda i, *_: (0, i)),
                      pl.BlockSpec(memory_space=pl.ANY)],
            out_specs=pl.BlockSpec(memory_space=pl.ANY),
            scratch_shapes=[pltpu.VMEM((2, s_loc, d), BF16), pltpu.SemaphoreType.DMA((2,))],
        ),
        input_output_aliases={5: 0},
        compiler_params=_cparams(("arbitrary",)),
        name="moe_sort",
    )(cntf, loff, goff, h2, info_row, x_sorted)

    blk = lambda b, be, nv: (jnp.minimum(b, nv[0] - 1), 0)
    d_sorted = pl.pallas_call(
        _experts_kernel,
        out_shape=jax.ShapeDtypeStruct(x_sorted.shape, BF16),
        grid_spec=pltpu.PrefetchScalarGridSpec(
            num_scalar_prefetch=2,
            grid=(n_blocks,),
            in_specs=[pl.BlockSpec((er, d), blk),
                      pl.BlockSpec((1, d, D_EXPERT), lambda b, be, nv: (be[b], 0, 0)),
                      pl.BlockSpec((1, d, D_EXPERT), lambda b, be, nv: (be[b], 0, 0)),
                      pl.BlockSpec((1, D_EXPERT, d), lambda b, be, nv: (be[b], 0, 0))],
            out_specs=pl.BlockSpec((er, d), lambda b, be, nv: (b, 0)),
        ),
        compiler_params=_cparams(("arbitrary",), VMEM_LIMIT),
        name="moe_experts",
    )(blk_expert, n_valid, x_sorted, w_eg, w_eu, w_ed)

    gt_spec = pl.BlockSpec((1, rm, d), (lambda i, *_: (i // tpg, 0, 0)) if rm == 1
                           else (lambda i, *_: (i // tpg, i % tpg, 0)))
    tok3 = pl.BlockSpec((1, tm, d), lambda i, *_: (i // tpg, i % tpg, 0))
    out = pl.pallas_call(
        functools.partial(_combine_kernel, rb=min(tm, 256)),
        out_shape=jax.ShapeDtypeStruct((g, t, d), F32),
        grid_spec=pltpu.PrefetchScalarGridSpec(
            num_scalar_prefetch=3,
            grid=(n_tiles,),
            in_specs=[pl.BlockSpec(memory_space=pl.ANY),
                      pl.BlockSpec((tm, 8), lambda i, *_: (i, 0)),
                      tok3, gt_spec],
            out_specs=tok3,
            scratch_shapes=[pltpu.VMEM((2, s_loc, d), BF16), pltpu.SemaphoreType.DMA((2,))],
        ),
        compiler_params=_cparams(("arbitrary",), VMEM_LIMIT),
        name="moe_combine",
    )(cntf, loff, goff, d_sorted, info_col, x, gt)
    return out, x_sorted


def _rope_tables(pos):
    half = HEAD_DK // 2
    inv_freq = ROPE_BASE ** (-jnp.arange(half, dtype=F32) / half)
    ang = pos[:, None] * inv_freq[None, :]
    cos, sin = jnp.cos(ang), jnp.sin(ang)
    return jnp.concatenate([cos, cos], axis=1), jnp.concatenate([-sin, sin], axis=1)


def _layer(x, mods, lw, cfg, ret0, gla0, cache_k, cache_v, cosf, sinf, x_sorted):
    g, t, d = x.shape
    b, tseq = cfg["b"], cfg["tseq"]
    sh_m, sc_m, gt_m, sh_f, sc_f, gt_f = mods
    proj, log_a = _in_proj(x, sc_m, sh_m, lw["g_mix"], lw["w_in"], lw["w_ga"], lw["a2"], lw["a_bias"],
                           cosf, sinf, lw["gq"], lw["gk"], cfg["tm"])
    proj_s = proj.reshape(b, tseq, PROJ_COLS)
    log_a_s = log_a.reshape(b, tseq, BRANCH_WIDTH)
    o_r, ret_new = _retention(proj_s, ret0, lw["g_ret_gn"], cfg["chunk"], cfg["tc"])
    o_g, gla_new = _gla(proj_s, log_a_s, gla0, lw["g_gla_gn"], cfg["chunk"], cfg["tc"])
    if cache_k is None:
        o_a = _band_attn_prompt(proj_s, lw["rel_bias"])
    else:
        o_a = _band_attn_sample(proj_s, cache_k, cache_v, lw["rel_bias"])
    tok = lambda a: a.reshape(g, t, BRANCH_WIDTH)
    x1, h2 = _merge(x, tok(o_r), tok(o_g), tok(o_a), (sc_m, sh_m, gt_m, sc_f, sh_f), lw["g_mix"], lw["g_ffn"],
                    lw["w_gate"], lw["b_gate"], lw["w_branch"], lw["w_out"], cfg["tm"])
    h2f = h2.reshape(g * t, d)
    info_row, tbl = _route(h2f, lw["wr_t"], lw["br_t"], cfg["tme"], SEG)
    x2, x_sorted = _moe(tbl[:, :, 0], h2f, x1, gt_f, info_row, info_row.T, lw["w_eg"], lw["w_eu"], lw["w_ed"],
                        x_sorted, cfg["tme"])
    tail = proj_s[:, -min(tseq, BAND_PAST):]
    k_new = tail[:, :, 9 * BRANCH_WIDTH:10 * BRANCH_WIDTH]
    v_new = tail[:, :, 10 * BRANCH_WIDTH:11 * BRANCH_WIDTH]
    return x2, ret_new, gla_new, k_new, v_new, x_sorted


def kernel(x_prompt, x_sample, state_ret, state_gla, cache_att_k, cache_att_v, c_prompt, c_sample,
           w_ada, b_ada, g_mix, w_in, gla_a2, gla_a_bias, g_ret_gn, g_gla_gn, g_q_att, g_k_att, rel_bias,
           w_branch, w_gate, b_gate, w_out, g_ffn, w_router_group, b_router_group, w_router_exp,
           b_router_exp, w_exp_gate, w_exp_up, w_exp_down):
    depth = w_ada.shape[0]
    bp, seq, d = x_prompt.shape
    bs, dseq, _ = x_sample.shape
    n_s = bs * dseq

    pad = (-(bp + bs)) % 8
    c_all = jnp.concatenate([c_prompt, c_sample, jnp.zeros((pad, d), F32)], axis=0)
    mod = _ada_mod(c_all, w_ada, b_ada)

    ga0 = 8 * BRANCH_WIDTH
    w_in_b = jnp.concatenate([w_in[:, :, :ga0], w_in[:, :, ga0 + GLA_RANK:]], axis=2).astype(BF16)
    w_ga = jnp.pad(w_in[:, :, ga0:ga0 + GLA_RANK], ((0, 0), (0, 0), (0, V7X_LANES - GLA_RANK))).astype(BF16)
    a2 = jnp.pad(gla_a2, ((0, 0), (0, V7X_LANES - GLA_RANK), (0, 0))).astype(BF16)
    wr_t = jnp.zeros((depth, 32, d), F32)
    wr_t = wr_t.at[:, 0:N_GROUPS].set(jnp.swapaxes(w_router_group, 1, 2))
    wr_t = wr_t.at[:, 8:8 + N_EXPERTS].set(jnp.swapaxes(w_router_exp, 1, 2)).astype(BF16)
    br_t = jnp.zeros((depth, 32, 1), F32)
    br_t = br_t.at[:, 0:N_GROUPS, 0].set(b_router_group).at[:, 8:8 + N_EXPERTS, 0].set(b_router_exp)
    w_gate_b, w_branch_b, w_out_b = w_gate.astype(BF16), w_branch.astype(BF16), w_out.astype(BF16)
    w_eg, w_eu, w_ed = w_exp_gate.astype(BF16), w_exp_up.astype(BF16), w_exp_down.astype(BF16)

    cos_p, sin_p = _rope_tables(jnp.arange(seq, dtype=F32))
    cos_s, sin_s = _rope_tables(PAST_LEN + jnp.arange(dseq, dtype=F32))
    cos_s, sin_s = jnp.tile(cos_s, (bs, 1)), jnp.tile(sin_s, (bs, 1))

    cfg_p = dict(b=bp, tseq=seq, tm=512, chunk=4 * CHUNK, tc=512, tme=512)
    cfg_s = dict(b=bs, tseq=dseq, tm=n_s, chunk=min(dseq, CHUNK), tc=dseq, tme=n_s)
    zero_state = jnp.zeros((bp, RET_HEADS, HEAD_DK, HEAD_DK), F32)
    sorted_p = jnp.zeros((_sorted_rows(bp * seq, cfg_p["tme"]), d), BF16)
    sorted_s = jnp.zeros((_sorted_rows(n_s, cfg_s["tme"]), d), BF16)

    xp = x_prompt
    xs = x_sample.reshape(1, n_s, d)
    outs = [[] for _ in range(8)]
    for l in range(depth):
        lw = dict(
            g_mix=g_mix[l][None], g_ffn=g_ffn[l][None], w_in=w_in_b[l], w_ga=w_ga[l], a2=a2[l],
            a_bias=gla_a_bias[l][None], gq=jnp.tile(g_q_att[l], ATT_HEADS)[None],
            gk=jnp.tile(g_k_att[l], ATT_HEADS)[None], g_ret_gn=g_ret_gn[l][None], g_gla_gn=g_gla_gn[l][None],
            rel_bias=rel_bias[l], w_gate=w_gate_b[l], b_gate=b_gate[l][None], w_branch=w_branch_b[l],
            w_out=w_out_b[l], wr_t=wr_t[l], br_t=br_t[l], w_eg=w_eg[l], w_eu=w_eu[l], w_ed=w_ed[l])
        mods_p = tuple(m[:, None, :] for m in jnp.split(mod[l, :bp], 6, axis=-1))
        mods_s = tuple(jnp.repeat(m, dseq, axis=0)[None] for m in jnp.split(mod[l, bp:bp + bs], 6, axis=-1))
        xp, rp, gp, kp, vp, sorted_p = _layer(xp, mods_p, lw, cfg_p, zero_state, zero_state, None, None,
                                              cos_p, sin_p, sorted_p)
        ck = cache_att_k[l].reshape(bs, BAND_PAST, BRANCH_WIDTH)
        cv = cache_att_v[l].reshape(bs, BAND_PAST, BRANCH_WIDTH)
        xs, rs, gs, ks, vs, sorted_s = _layer(xs, mods_s, lw, cfg_s, state_ret[l], state_gla[l], ck, cv,
                                              cos_s, sin_s, sorted_s)
        heads = lambda a: a.astype(F32).reshape(a.shape[0], a.shape[1], ATT_HEADS, ATT_HD)
        for lst, val in zip(outs, (rp, gp, heads(kp), heads(vp), rs, gs, heads(ks), heads(vs))):
            lst.append(val)
    return (xp, xs.reshape(bs, dseq, d)) + tuple(jnp.stack(o) for o in outs)
```

```python
import functools

import numpy as np
import jax
import jax.numpy as jnp
from jax import lax
from jax.experimental import pallas as pl
from jax.experimental.pallas import tpu as pltpu

F32 = jnp.float32
BF16 = jnp.bfloat16

D_MODEL = 1024
CHUNK = 64
BRANCH_WIDTH = 512
N_BRANCH = 3
RET_HEADS = 4
HEAD_DK = 128
GLA_RANK = 16
GLA_TAU = 16.0
ATT_HEADS = 8
ATT_HD = 64
BAND_PAST = 512
MAX_REL = 128
N_GROUPS = 4
EXPERTS_PER_GROUP = 4
N_EXPERTS = 16
D_EXPERT = 512
ROPE_BASE = 10000.0
EPS = 1e-6
GN_EPS = 1e-5
PAST_LEN = 1024
N_PROJ_BLOCKS = 11
PROJ_COLS = N_PROJ_BLOCKS * BRANCH_WIDTH

V7X_LANES = 128
V7X_VMEM_BYTES = 64 * 1024 * 1024
VMEM_LIMIT = 56 * 1024 * 1024
NEG_BIG = -1e30


def _cparams(sem, vmem=None):
    return pltpu.CompilerParams(dimension_semantics=sem, vmem_limit_bytes=vmem)


def _const_spec(shape):
    nd = len(shape)
    return pl.BlockSpec(shape, lambda *_: (0,) * nd)


def _dot(a, b):
    return jnp.dot(a, b, preferred_element_type=F32)


def _dot_nt(a, b):
    return lax.dot_general(a, b, (((1,), (1,)), ((), ())), preferred_element_type=F32)


def _dot_tn(a, b):
    return lax.dot_general(a, b, (((0,), (0,)), ((), ())), preferred_element_type=F32)


def _silu(x):
    return x * jax.nn.sigmoid(x)


def _norm_mod(x, g, sc, sh):
    ms = jnp.mean(x * x, axis=-1, keepdims=True)
    return x * lax.rsqrt(ms + EPS) * g * (1.0 + sc) + sh


def _ada_kernel(c_ref, w_ref, b_ref, o_ref):
    s = _silu(c_ref[...])
    o_ref[0] = _dot(s.astype(BF16), w_ref[0].astype(BF16)) + b_ref[0]


def _ada_mod(c_all, w_ada, b_ada):
    depth, d, n = w_ada.shape
    rows = c_all.shape[0]
    bn = 1536
    return pl.pallas_call(
        _ada_kernel,
        out_shape=jax.ShapeDtypeStruct((depth, rows, n), F32),
        grid=(depth, n // bn),
        in_specs=[
            pl.BlockSpec((rows, d), lambda l, j: (0, 0)),
            pl.BlockSpec((1, d, bn), lambda l, j: (l, 0, j)),
            pl.BlockSpec((1, 1, bn), lambda l, j: (l, 0, j)),
        ],
        out_specs=pl.BlockSpec((1, rows, bn), lambda l, j: (l, 0, j)),
        compiler_params=_cparams(("arbitrary", "arbitrary"), VMEM_LIMIT),
        name="ada_mod",
    )(c_all, w_ada, b_ada.reshape(depth, 1, n))


def _rope_heads(a, cosf, sinf):
    outs = []
    for h in range(RET_HEADS):
        ah = a[:, h * HEAD_DK:(h + 1) * HEAD_DK]
        outs.append(ah * cosf + pltpu.roll(ah, HEAD_DK // 2, 1) * sinf)
    return jnp.concatenate(outs, axis=1)


def _rms_heads64(a, gain):
    low = lax.broadcasted_iota(jnp.int32, (1, V7X_LANES), 1) < ATT_HD
    outs = []
    for c in range(a.shape[1] // V7X_LANES):
        ac = a[:, c * V7X_LANES:(c + 1) * V7X_LANES]
        sq = ac * ac
        lo = jnp.sum(jnp.where(low, sq, 0.0), axis=-1, keepdims=True)
        hi = jnp.sum(jnp.where(low, 0.0, sq), axis=-1, keepdims=True)
        ms = jnp.where(low, lo, hi) * (1.0 / ATT_HD)
        outs.append(ac * lax.rsqrt(ms + EPS))
    return jnp.concatenate(outs, axis=1) * gain


def _in_proj_kernel(x_ref, sc_ref, sh_ref, g_ref, w_ref, wga_ref, a2_ref, ab_ref, cos_ref, sin_ref,
                    gq_ref, gk_ref, proj_ref, la_ref):
    x = x_ref[0]
    hb = _norm_mod(x, g_ref[...], sc_ref[0], sh_ref[0]).astype(BF16)
    cosf = cos_ref[...]
    sinf = sin_ref[...]
    for j in range(N_PROJ_BLOCKS):
        cols = slice(j * BRANCH_WIDTH, (j + 1) * BRANCH_WIDTH)
        acc = _dot(hb, w_ref[:, cols])
        if j == 0:
            acc = _rope_heads(acc, cosf, sinf)
        elif j == 1:
            acc = _rope_heads(acc, cosf, sinf) * (HEAD_DK ** -0.5)
        elif j == 4:
            acc = acc * (HEAD_DK ** -0.5)
        elif j == 8:
            acc = _rms_heads64(acc, gq_ref[...]) * (ATT_HD ** -0.5)
        elif j == 9:
            acc = _rms_heads64(acc, gk_ref[...])
        proj_ref[0, :, cols] = acc.astype(BF16)
    ga = _dot(hb, wga_ref[...])
    z = _dot(ga.astype(BF16), a2_ref[...]) + ab_ref[...]
    la_ref[0] = jax.nn.log_sigmoid(z) * (1.0 / GLA_TAU)


def _in_proj(x, sc, sh, g_mix, w_in, w_ga, a2, a_bias, cosf, sinf, gq, gk, tm):
    g, t, d = x.shape
    rm = sc.shape[1]
    mod_spec = pl.BlockSpec((1, rm, d), (lambda b, i: (b, 0, 0)) if rm == 1 else (lambda b, i: (b, i, 0)))
    return pl.pallas_call(
        _in_proj_kernel,
        out_shape=(jax.ShapeDtypeStruct((g, t, PROJ_COLS), BF16),
                   jax.ShapeDtypeStruct((g, t, BRANCH_WIDTH), F32)),
        grid=(g, t // tm),
        in_specs=[
            pl.BlockSpec((1, tm, d), lambda b, i: (b, i, 0)),
            mod_spec, mod_spec,
            _const_spec((1, d)),
            _const_spec((d, PROJ_COLS)),
            _const_spec((d, V7X_LANES)),
            _const_spec((V7X_LANES, BRANCH_WIDTH)),
            _const_spec((1, BRANCH_WIDTH)),
            pl.BlockSpec((tm, HEAD_DK), lambda b, i: (i, 0)),
            pl.BlockSpec((tm, HEAD_DK), lambda b, i: (i, 0)),
            _const_spec((1, BRANCH_WIDTH)),
            _const_spec((1, BRANCH_WIDTH)),
        ],
        out_specs=(pl.BlockSpec((1, tm, PROJ_COLS), lambda b, i: (b, i, 0)),
                   pl.BlockSpec((1, tm, BRANCH_WIDTH), lambda b, i: (b, i, 0))),
        compiler_params=_cparams(("arbitrary", "arbitrary"), VMEM_LIMIT),
        name="in_proj",
    )(x, sc, sh, g_mix, w_in, w_ga, a2, a_bias, cosf, sinf, gq, gk)


def _retention_kernel(q_ref, k_ref, v_ref, g_ref, s0_ref, dm_ref, qd_ref, kd_ref, bd_ref, gn_ref,
                      o_ref, so_ref, st_ref, *, chunk, n_chunks):
    @pl.when(pl.program_id(1) == 0)
    def _():
        st_ref[...] = s0_ref[0]

    def body(c, carry):
        rows = pl.ds(pl.multiple_of(c * chunk, chunk), chunk)
        for h in range(RET_HEADS):
            cols = slice(h * HEAD_DK, (h + 1) * HEAD_DK)
            q = q_ref[0, rows, cols]
            k = k_ref[0, rows, cols]
            v = v_ref[0, rows, cols]
            scores = _dot_nt(q, k) * dm_ref[h]
            o = _dot(scores.astype(BF16), v)
            o = o + _dot((q.astype(F32) * qd_ref[h]).astype(BF16), st_ref[h].astype(BF16))
            st_ref[h] = st_ref[h] * bd_ref[h] + _dot_tn((k.astype(F32) * kd_ref[h]).astype(BF16), v)
            mu = jnp.mean(o, axis=-1, keepdims=True)
            oc = o - mu
            var = jnp.mean(oc * oc, axis=-1, keepdims=True)
            on = oc * lax.rsqrt(var + GN_EPS) * gn_ref[:, cols]
            o_ref[0, rows, cols] = (_silu(g_ref[0, rows, cols].astype(F32)) * on).astype(BF16)
        return carry

    lax.fori_loop(0, n_chunks, body, 0)

    @pl.when(pl.program_id(1) == pl.num_programs(1) - 1)
    def _():
        so_ref[0] = st_ref[...]


def _retention_tables(chunk):
    log_gamma = jnp.log(1.0 - jnp.exp2(-5.0 - jnp.arange(RET_HEADS, dtype=F32)))
    idx = jnp.arange(chunk, dtype=F32)
    diff = idx[:, None] - idx[None, :]
    dmask = jnp.where(diff >= 0, jnp.exp(log_gamma[:, None, None] * jnp.maximum(diff, 0.0)), 0.0)
    qd = jnp.exp(log_gamma[:, None] * (idx + 1.0))[:, :, None]
    kd = jnp.exp(log_gamma[:, None] * (chunk - 1.0 - idx))[:, :, None]
    bd = jnp.exp(log_gamma * chunk)[:, None, None]
    bc = lambda a, r: jnp.broadcast_to(a, (RET_HEADS, r, HEAD_DK)).astype(F32)
    return dmask.astype(F32), bc(qd, chunk), bc(kd, chunk), bc(bd, 1)


def _retention(proj, state0, gn, chunk, tc):
    b, t, _ = proj.shape
    dm, qd, kd, bd = _retention_tables(chunk)
    col = lambda j: pl.BlockSpec((1, tc, BRANCH_WIDTH), lambda i, s, j=j: (i, s, j))
    st_spec = pl.BlockSpec((1, RET_HEADS, HEAD_DK, HEAD_DK), lambda i, s: (i, 0, 0, 0))
    return pl.pallas_call(
        functools.partial(_retention_kernel, chunk=chunk, n_chunks=tc // chunk),
        out_shape=(jax.ShapeDtypeStruct((b, t, BRANCH_WIDTH), BF16),
                   jax.ShapeDtypeStruct((b, RET_HEADS, HEAD_DK, HEAD_DK), F32)),
        grid=(b, t // tc),
        in_specs=[col(0), col(1), col(2), col(3), st_spec,
                  _const_spec(dm.shape), _const_spec(qd.shape), _const_spec(kd.shape), _const_spec(bd.shape),
                  _const_spec((1, BRANCH_WIDTH))],
        out_specs=(pl.BlockSpec((1, tc, BRANCH_WIDTH), lambda i, s: (i, s, 0)), st_spec),
        scratch_shapes=[pltpu.VMEM((RET_HEADS, HEAD_DK, HEAD_DK), F32)],
        compiler_params=_cparams(("arbitrary", "arbitrary")),
        name="retention",
    )(proj, proj, proj, proj, state0, dm, qd, kd, bd, gn)


GLA_SAFE_LOG_DECAY = -60.0


def _gla_intra_direct(q, bh, rows_ref, chunk):
    trow = lax.broadcasted_iota(jnp.int32, (chunk, 1), 0)

    def step(s, acc):
        bs = rows_ref[0, pl.ds(s, 1), :]
        ks = rows_ref[1, pl.ds(s, 1), :]
        vs = rows_ref[2, pl.ds(s, 1), :]
        e = jnp.exp(jnp.minimum(bh - bs, 0.0))
        col = jnp.sum(q * ks * e, axis=-1, keepdims=True)
        return acc + jnp.where(trow >= s, col, 0.0) * vs

    return lax.fori_loop(0, chunk, step, jnp.zeros((chunk, HEAD_DK), F32))


def _gla_kernel(q_ref, k_ref, v_ref, r_ref, la_ref, s0_ref, tri_ref, gn_ref, o_ref, so_ref, st_ref, rows_ref,
                *, chunk, n_chunks):
    @pl.when(pl.program_id(1) == 0)
    def _():
        for h in range(RET_HEADS):
            st_ref[h] = s0_ref[0, h].T

    tri = tri_ref[...]
    causal = (lax.broadcasted_iota(jnp.int32, (chunk, chunk), 0)
              >= lax.broadcasted_iota(jnp.int32, (chunk, chunk), 1))

    def chunk_body(factored):
        def body(c, carry):
            rows = pl.ds(pl.multiple_of(c * chunk, chunk), chunk)
            la = la_ref[0, rows, :]
            la_hi = la.astype(BF16)
            la_lo = (la - la_hi.astype(F32)).astype(BF16)
            b_all = _dot(tri, la_hi) + _dot(tri, la_lo)
            for h in range(RET_HEADS):
                cols = slice(h * HEAD_DK, (h + 1) * HEAD_DK)
                q = q_ref[0, rows, cols].astype(F32)
                k = k_ref[0, rows, cols].astype(F32)
                v = v_ref[0, rows, cols]
                bh = b_all[:, cols]
                bl = bh[chunk - 1:chunk, :]
                qh = (q * jnp.exp(bh)).astype(BF16)
                if factored:
                    kh = (k * jnp.exp(-bh)).astype(BF16)
                    scores = jnp.where(causal, _dot_nt(qh, kh), 0.0)
                    o = _dot(scores.astype(BF16), v)
                else:
                    rows_ref[0] = bh
                    rows_ref[1] = k
                    rows_ref[2] = v.astype(F32)
                    o = _gla_intra_direct(q, bh, rows_ref, chunk)
                o = o + _dot_nt(qh, st_ref[h].astype(BF16))
                kd = (k * jnp.exp(bl - bh)).astype(BF16)
                st_ref[h] = st_ref[h] * jnp.exp(bl) + _dot_tn(v, kd)
                ms = jnp.mean(o * o, axis=-1, keepdims=True)
                on = o * lax.rsqrt(ms + EPS) * gn_ref[:, cols]
                o_ref[0, rows, cols] = (_silu(r_ref[0, rows, cols].astype(F32)) * on).astype(BF16)
            return carry
        return body

    lowest = None
    for c in range(n_chunks):
        tot = jnp.sum(la_ref[0, c * chunk:(c + 1) * chunk, :], axis=0, keepdims=True)
        lowest = tot if lowest is None else jnp.minimum(lowest, tot)
    safe = jnp.min(lowest) > GLA_SAFE_LOG_DECAY

    @pl.when(safe)
    def _():
        lax.fori_loop(0, n_chunks, chunk_body(True), 0)

    @pl.when(jnp.logical_not(safe))
    def _():
        lax.fori_loop(0, n_chunks, chunk_body(False), 0)

    @pl.when(pl.program_id(1) == pl.num_programs(1) - 1)
    def _():
        for h in range(RET_HEADS):
            so_ref[0, h] = st_ref[h].T


def _gla(proj, log_a, state0, gn, chunk, tc):
    b, t, _ = proj.shape
    tri = jnp.tril(jnp.ones((chunk, chunk), F32)).astype(BF16)
    col = lambda j: pl.BlockSpec((1, tc, BRANCH_WIDTH), lambda i, s, j=j: (i, s, j))
    st_spec = pl.BlockSpec((1, RET_HEADS, HEAD_DK, HEAD_DK), lambda i, s: (i, 0, 0, 0))
    return pl.pallas_call(
        functools.partial(_gla_kernel, chunk=chunk, n_chunks=tc // chunk),
        out_shape=(jax.ShapeDtypeStruct((b, t, BRANCH_WIDTH), BF16),
                   jax.ShapeDtypeStruct((b, RET_HEADS, HEAD_DK, HEAD_DK), F32)),
        grid=(b, t // tc),
        in_specs=[col(4), col(5), col(6), col(7),
                  pl.BlockSpec((1, tc, BRANCH_WIDTH), lambda i, s: (i, s, 0)),
                  st_spec, _const_spec((chunk, chunk)), _const_spec((1, BRANCH_WIDTH))],
        out_specs=(pl.BlockSpec((1, tc, BRANCH_WIDTH), lambda i, s: (i, s, 0)), st_spec),
        scratch_shapes=[pltpu.VMEM((RET_HEADS, HEAD_DK, HEAD_DK), F32),
                        pltpu.VMEM((3, chunk, HEAD_DK), F32)],
        compiler_params=_cparams(("arbitrary", "arbitrary")),
        name="gla",
    )(proj, proj, proj, proj, log_a, state0, tri, gn)


def _attn_stage_scratch(rb, win):
    pairs = ATT_HEADS // 2
    return [pltpu.VMEM((2, pairs, 2 * rb, win), F32), pltpu.VMEM((2, pairs, 2 * rb, win), BF16),
            pltpu.VMEM((2, pairs, 2 * rb, V7X_LANES), F32)]


def _band_attn_kernel(q_ref, kp_ref, kc_ref, vp_ref, vc_ref, bias_ref, o_ref, kw_ref, vw_ref,
                      s_ref, e_ref, den_ref,
                      *, n_sub, rb, win, prev_rows, mask_start):
    cur_rows = kc_ref.shape[1]
    kw_ref[0:prev_rows, :] = kp_ref[0].astype(BF16)
    kw_ref[prev_rows:prev_rows + cur_rows, :] = kc_ref[0]
    vw_ref[0:prev_rows, :] = vp_ref[0].astype(BF16)
    vw_ref[prev_rows:prev_rows + cur_rows, :] = vc_ref[0]
    low = lax.broadcasted_iota(jnp.int32, (1, V7X_LANES), 1) < ATT_HD
    q0 = pl.program_id(1) * cur_rows
    pairs = ATT_HEADS // 2

    def scores(i):
        for p in range(pairs):
            cols = slice(p * V7X_LANES, (p + 1) * V7X_LANES)
            q2 = q_ref[0, i * rb:(i + 1) * rb, cols]
            zero = jnp.zeros_like(q2)
            qst = jnp.concatenate([jnp.where(low, q2, zero), jnp.where(low, zero, q2)], axis=0)
            s_ref[i % 2, p] = _dot_nt(qst, kw_ref[i * rb:i * rb + win, cols])

    def softmax(i, masked):
        if masked:
            valid = lax.broadcasted_iota(jnp.int32, (1, win), 1) >= prev_rows - q0 - i * rb
        for p in range(pairs):
            s = s_ref[i % 2, p] + bias_ref[p]
            if masked:
                s = jnp.where(valid, s, NEG_BIG)
            m = jnp.max(s, axis=-1, keepdims=True)
            e = jnp.exp(s - m)
            den_ref[i % 2, p] = jnp.broadcast_to(jnp.sum(e, axis=-1, keepdims=True), den_ref.shape[2:])
            e_ref[i % 2, p] = e.astype(BF16)

    def values(i):
        for p in range(pairs):
            cols = slice(p * V7X_LANES, (p + 1) * V7X_LANES)
            o2 = _dot(e_ref[i % 2, p], vw_ref[i * rb:i * rb + win, cols]) / den_ref[i % 2, p]
            o_ref[0, i * rb:(i + 1) * rb, cols] = jnp.where(low, o2[:rb], o2[rb:]).astype(BF16)

    def pipeline(masked):
        for step in range(n_sub + 2):
            if step < n_sub:
                scores(step)
            if 0 <= step - 1 < n_sub:
                softmax(step - 1, masked)
            if 0 <= step - 2 < n_sub:
                values(step - 2)

    if mask_start:
        pl.when(q0 < prev_rows)(functools.partial(pipeline, True))
        pl.when(q0 >= prev_rows)(functools.partial(pipeline, False))
    else:
        pipeline(False)


def _rel_bias_table(rel_bias, rb, win, chunk):
    period = win + rb
    u = np.arange(period)
    u = np.where(u < win, u, u - period)
    vec = rel_bias.astype(F32)[:, np.clip(BAND_PAST - u, -MAX_REL, MAX_REL) + MAX_REL]
    bias = jnp.tile(vec, (1, rb))[:, :rb * (period - 1)].reshape(-1, rb, period - 1)[:, :, :win]
    r = np.arange(rb)[:, None]
    j = np.arange(win)[None, :]
    c0 = (r // chunk) * chunk
    allowed = (j >= c0) & (j < c0 + BAND_PAST + chunk)
    bias = jnp.where(jnp.asarray(allowed)[None], bias, NEG_BIG)
    return bias.reshape(ATT_HEADS // 2, 2 * rb, win)


def _band_attn_prompt(proj, rel_bias):
    b, t, _ = proj.shape
    qb, rb = 512, 128
    win = BAND_PAST + rb
    bias = _rel_bias_table(rel_bias, rb, win, CHUNK)
    cur = lambda j: pl.BlockSpec((1, qb, BRANCH_WIDTH), lambda i, s, j=j: (i, s, j))
    prev = lambda j: pl.BlockSpec((1, qb, BRANCH_WIDTH), lambda i, s, j=j: (i, jnp.maximum(s - 1, 0), j))
    return pl.pallas_call(
        functools.partial(_band_attn_kernel, n_sub=qb // rb, rb=rb, win=win, prev_rows=BAND_PAST,
                          mask_start=True),
        out_shape=jax.ShapeDtypeStruct((b, t, BRANCH_WIDTH), BF16),
        grid=(b, t // qb),
        in_specs=[cur(8), prev(9), cur(9), prev(10), cur(10), _const_spec(bias.shape)],
        out_specs=pl.BlockSpec((1, qb, BRANCH_WIDTH), lambda i, s: (i, s, 0)),
        scratch_shapes=[pltpu.VMEM((BAND_PAST + qb, BRANCH_WIDTH), BF16),
                        pltpu.VMEM((BAND_PAST + qb, BRANCH_WIDTH), BF16)] + _attn_stage_scratch(rb, win),
        compiler_params=_cparams(("arbitrary", "arbitrary")),
        name="band_attn",
    )(proj, proj, proj, proj, proj, bias)


def _band_attn_sample(proj, cache_k, cache_v, rel_bias):
    b, t, _ = proj.shape
    win = BAND_PAST + t
    bias = _rel_bias_table(rel_bias, t, win, t)
    cur = lambda j: pl.BlockSpec((1, t, BRANCH_WIDTH), lambda i, s, j=j: (i, 0, j))
    cache = pl.BlockSpec((1, BAND_PAST, BRANCH_WIDTH), lambda i, s: (i, 0, 0))
    return pl.pallas_call(
        functools.partial(_band_attn_kernel, n_sub=1, rb=t, win=win, prev_rows=BAND_PAST, mask_start=False),
        out_shape=jax.ShapeDtypeStruct((b, t, BRANCH_WIDTH), BF16),
        grid=(b, 1),
        in_specs=[cur(8), cache, cur(9), cache, cur(10), _const_spec(bias.shape)],
        out_specs=pl.BlockSpec((1, t, BRANCH_WIDTH), lambda i, s: (i, 0, 0)),
        scratch_shapes=[pltpu.VMEM((win, BRANCH_WIDTH), BF16), pltpu.VMEM((win, BRANCH_WIDTH), BF16)]
        + _attn_stage_scratch(t, win),
        compiler_params=_cparams(("arbitrary", "arbitrary")),
        name="band_attn_sample",
    )(proj, cache_k, proj, cache_v, proj, bias)


def _merge_kernel(x_ref, or_ref, og_ref, oa_ref, scm_ref, shm_ref, gtm_ref, scf_ref, shf_ref,
                  gmix_ref, gffn_ref, wg_ref, bg_ref, wb_ref, wo_ref, xo_ref, h2_ref):
    x = x_ref[0]
    d = x.shape[1]
    hb = _norm_mod(x, gmix_ref[...], scm_ref[0], shm_ref[0]).astype(BF16)
    merged = None
    for n, o_ref in enumerate((or_ref, og_ref, oa_ref)):
        gate = jax.nn.sigmoid(_dot(hb, wg_ref[:, n * d:(n + 1) * d]) + bg_ref[:, n * d:(n + 1) * d])
        y = gate * _dot(o_ref[0], wb_ref[n])
        merged = y if merged is None else merged + y
    mix = _dot(merged.astype(BF16), wo_ref[...])
    xn = x + gtm_ref[0] * mix
    xo_ref[0] = xn
    h2_ref[0] = _norm_mod(xn, gffn_ref[...], scf_ref[0], shf_ref[0]).astype(BF16)


def _merge(x, o_r, o_g, o_a, mods, g_mix, g_ffn, w_gate, b_gate, w_branch, w_out, tm):
    g, t, d = x.shape
    rm = mods[0].shape[1]
    mod_spec = pl.BlockSpec((1, rm, d), (lambda b, i: (b, 0, 0)) if rm == 1 else (lambda b, i: (b, i, 0)))
    tok = lambda w: pl.BlockSpec((1, tm, w), lambda b, i: (b, i, 0))
    return pl.pallas_call(
        _merge_kernel,
        out_shape=(jax.ShapeDtypeStruct((g, t, d), F32), jax.ShapeDtypeStruct((g, t, d), BF16)),
        grid=(g, t // tm),
        in_specs=[tok(d), tok(BRANCH_WIDTH), tok(BRANCH_WIDTH), tok(BRANCH_WIDTH)] + [mod_spec] * 5 + [
            _const_spec((1, d)), _const_spec((1, d)),
            _const_spec((d, N_BRANCH * d)), _const_spec((1, N_BRANCH * d)),
            _const_spec((N_BRANCH, BRANCH_WIDTH, d)), _const_spec((d, d))],
        out_specs=(tok(d), tok(d)),
        compiler_params=_cparams(("arbitrary", "arbitrary"), VMEM_LIMIT),
        name="merge",
    )(x, o_r, o_g, o_a, *mods, g_mix, g_ffn, w_gate, b_gate, w_branch, w_out)


def _first_argmax(vals, n):
    row = lax.broadcasted_iota(jnp.int32, vals.shape, 0).astype(F32)
    m = jnp.max(vals, axis=0, keepdims=True)
    idx = jnp.min(jnp.where(vals == m, row, float(n)), axis=0, keepdims=True)
    return m, idx


def _route_kernel(h_ref, wr_ref, br_ref, up_ref, l16_ref, info_ref, tbl_ref, *, rows_per_block):
    tm = h_ref.shape[0]
    lg = _dot_nt(wr_ref[...], h_ref[...]) + br_ref[...]
    g = lg[0:N_GROUPS]
    gmax, grp = _first_argmax(g, N_GROUPS)
    p_group = 1.0 / jnp.sum(jnp.exp(g - gmax), axis=0, keepdims=True)
    esel = jnp.zeros((EXPERTS_PER_GROUP, tm), F32)
    for gi in range(N_GROUPS):
        blk = lg[8 + gi * EXPERTS_PER_GROUP:8 + (gi + 1) * EXPERTS_PER_GROUP]
        esel = esel + jnp.where(grp == float(gi), blk, 0.0)
    v1, i1 = _first_argmax(esel, EXPERTS_PER_GROUP)
    row4 = lax.broadcasted_iota(jnp.int32, esel.shape, 0).astype(F32)
    v2, i2 = _first_argmax(jnp.where(row4 == i1, -jnp.inf, esel), EXPERTS_PER_GROUP)
    e21 = jnp.exp(v2 - v1)
    w1 = p_group / (1.0 + e21)
    w2 = p_group * e21 / (1.0 + e21)
    e1 = grp * float(EXPERTS_PER_GROUP) + i1
    e2 = grp * float(EXPERTS_PER_GROUP) + i2
    row16 = lax.broadcasted_iota(jnp.int32, (N_EXPERTS, tm), 0).astype(F32)
    hit1 = row16 == e1
    hit2 = row16 == e2
    onehot = jnp.where(hit1 | hit2, 1.0, 0.0)
    prefix = _dot(onehot.astype(BF16), up_ref[...])
    cnt = jnp.sum(onehot, axis=1, keepdims=True)
    nblk = jnp.floor((cnt + float(rows_per_block - 1)) * (1.0 / rows_per_block))
    nblk_b = jnp.broadcast_to(nblk, (N_EXPERTS, V7X_LANES))
    offb = _dot(l16_ref[...], nblk_b.astype(BF16))[:, 0:1]
    base = offb * float(rows_per_block) + prefix
    info_ref[0:1, :] = jnp.sum(jnp.where(hit1, base, 0.0), axis=0, keepdims=True)
    info_ref[1:2, :] = jnp.sum(jnp.where(hit2, base, 0.0), axis=0, keepdims=True)
    info_ref[2:3, :] = w1
    info_ref[3:4, :] = w2
    info_ref[4:8, :] = jnp.zeros((4, tm), F32)
    tbl_ref[0] = nblk_b.astype(jnp.int32)


def _route(h2, wr_t, br_t, tm, rows_per_block):
    n, d = h2.shape
    nt = n // tm
    upper = jnp.triu(jnp.ones((tm, tm), F32), 1).astype(BF16)
    l16 = jnp.tril(jnp.ones((N_EXPERTS, N_EXPERTS), F32), -1).astype(BF16)
    return pl.pallas_call(
        functools.partial(_route_kernel, rows_per_block=rows_per_block),
        out_shape=(jax.ShapeDtypeStruct((8, n), F32),
                   jax.ShapeDtypeStruct((nt, N_EXPERTS, V7X_LANES), jnp.int32)),
        grid=(nt,),
        in_specs=[pl.BlockSpec((tm, d), lambda i: (i, 0)),
                  _const_spec((32, d)), _const_spec((32, 1)),
                  _const_spec((tm, tm)), _const_spec((N_EXPERTS, N_EXPERTS))],
        out_specs=(pl.BlockSpec((8, tm), lambda i: (0, i)),
                   pl.BlockSpec((1, N_EXPERTS, V7X_LANES), lambda i: (i, 0, 0))),
        compiler_params=_cparams(("arbitrary",)),
        name="route",
    )(h2, wr_t, br_t, upper, l16)


SEG = 16
GATHER_ROWS = 256
MAX_EXPERT_ROWS = 512


def _expert_rows(n_tokens):
    mean_rows = 2 * n_tokens // N_EXPERTS
    return int(min(MAX_EXPERT_ROWS, max(4 * SEG, 1 << (mean_rows // 4).bit_length())))


def _segment_copies(cnt_ref, loff_ref, goff_ref, tile, local_ref, global_ref, sem, to_global):
    for e in range(N_EXPERTS):
        cnt = cnt_ref[tile * N_EXPERTS + e]
        rows = pl.multiple_of(cnt * SEG, SEG)
        loc = local_ref.at[pl.ds(pl.multiple_of(loff_ref[tile * N_EXPERTS + e] * SEG, SEG), rows)]
        glo = global_ref.at[pl.ds(pl.multiple_of(goff_ref[tile * N_EXPERTS + e] * SEG, SEG), rows)]
        cp = pltpu.make_async_copy(loc, glo, sem) if to_global else pltpu.make_async_copy(glo, loc, sem)
        yield cnt > 0, cp


def _tile_units(cnt_ref, tile):
    total = jnp.int32(0)
    for e in range(N_EXPERTS):
        total = total + cnt_ref[tile * N_EXPERTS + e]
    return total


def _sort_kernel(cnt_ref, loff_ref, goff_ref, h_ref, irow_ref, xin_ref, xout_ref, xs2_ref, sem2):
    del xin_ref
    i = pl.program_id(0)
    last = pl.num_programs(0) - 1
    slot = lax.rem(i, 2)
    xs_ref = xs2_ref.at[slot]
    sem = sem2.at[slot]

    def wait_tile(tile, s):
        for pred, cp in _segment_copies(cnt_ref, loff_ref, goff_ref, tile, xs2_ref.at[s], xout_ref, sem2.at[s], True):
            pl.when(pred)(cp.wait)

    @pl.when(i >= 2)
    def _():
        wait_tile(i - 2, slot)

    gr = GATHER_ROWS
    n_gather = (_tile_units(cnt_ref, i) * SEG + gr - 1) // gr
    pos1 = irow_ref[0:1, :]
    pos2 = irow_ref[1:2, :]
    h = h_ref[...]

    def gather(gb, carry):
        r0 = pl.multiple_of(gb * gr, gr)
        srow = (lax.broadcasted_iota(jnp.int32, (gr, 1), 0) + r0).astype(F32)
        sel = jnp.where((srow == pos1) | (srow == pos2), 1.0, 0.0).astype(BF16)
        xs_ref[pl.ds(r0, gr), :] = _dot(sel, h).astype(BF16)
        return carry

    lax.fori_loop(0, n_gather, gather, 0)
    for pred, cp in _segment_copies(cnt_ref, loff_ref, goff_ref, i, xs_ref, xout_ref, sem, True):
        pl.when(pred)(cp.start)

    @pl.when(i == last)
    def _():
        @pl.when(i >= 1)
        def _():
            wait_tile(i - 1, 1 - slot)

        wait_tile(i, slot)


def _experts_kernel(be_ref, nv_ref, x_ref, wg_ref, wu_ref, wd_ref, o_ref):
    del be_ref
    used = pl.program_id(0) < nv_ref[0]

    @pl.when(used)
    def _():
        xb = x_ref[...]
        a = _silu(_dot(xb, wg_ref[0])) * _dot(xb, wu_ref[0])
        o_ref[...] = _dot(a.astype(BF16), wd_ref[0]).astype(BF16)

    @pl.when(jnp.logical_not(used))
    def _():
        o_ref[...] = jnp.zeros(o_ref.shape, BF16)


def _combine_kernel(cnt_ref, loff_ref, goff_ref, ds_ref, icol_ref, x_ref, gt_ref, o_ref, dl2_ref, sem2, *, rb):
    i = pl.program_id(0)
    tm = x_ref.shape[1]
    s_loc = dl2_ref.shape[1]
    slot = lax.rem(i, 2)

    def fetch_tile(tile, s):
        for pred, cp in _segment_copies(cnt_ref, loff_ref, goff_ref, tile, dl2_ref.at[s], ds_ref, sem2.at[s], False):
            pl.when(pred)(cp.start)

        def zero_tail(u, carry):
            dl2_ref[s, pl.ds(pl.multiple_of(u * SEG, SEG), SEG), :] = jnp.zeros((SEG, dl2_ref.shape[2]), BF16)
            return carry

        lax.fori_loop(_tile_units(cnt_ref, tile), s_loc // SEG, zero_tail, 0)

    @pl.when(i == 0)
    def _():
        fetch_tile(0, 0)

    @pl.when(i + 1 < pl.num_programs(0))
    def _():
        fetch_tile(i + 1, 1 - slot)

    for pred, cp in _segment_copies(cnt_ref, loff_ref, goff_ref, i, dl2_ref.at[slot], ds_ref, sem2.at[slot], False):
        pl.when(pred)(cp.wait)
    scol = lax.broadcasted_iota(jnp.int32, (1, s_loc), 1).astype(F32)
    for r in range(tm // rb):
        rows = slice(r * rb, (r + 1) * rb)
        cmb = (jnp.where(scol == icol_ref[rows, 0:1], icol_ref[rows, 2:3], 0.0)
               + jnp.where(scol == icol_ref[rows, 1:2], icol_ref[rows, 3:4], 0.0))
        y = _dot(cmb.astype(BF16), dl2_ref[slot])
        gt = gt_ref[0] if gt_ref.shape[1] == 1 else gt_ref[0, rows, :]
        o_ref[0, rows, :] = x_ref[0, rows, :] + gt * y


def _sorted_rows(n_tokens, tm):
    n_tiles = n_tokens // tm
    er = _expert_rows(n_tokens)
    rows = 2 * n_tokens + n_tiles * N_EXPERTS * (SEG - 1) + N_EXPERTS * (er - 1)
    return -(-rows // er) * er


def _moe(cnt, h2, x, gt, info_row, info_col, w_eg, w_eu, w_ed, x_sorted, tm):
    g, t, d = x.shape
    n = g * t
    n_tiles = n // tm
    tpg = t // tm
    rm = gt.shape[1]
    er = _expert_rows(n)
    n_blocks = x_sorted.shape[0] // er
    s_loc = -(-(2 * tm + N_EXPERTS * (SEG - 1)) // GATHER_ROWS) * GATHER_ROWS
    upb = er // SEG
    reg_blk = (jnp.sum(cnt, axis=0) + upb - 1) // upb
    blk_end = jnp.cumsum(reg_blk)
    goff = ((blk_end - reg_blk)[None, :] * upb + jnp.cumsum(cnt, axis=0) - cnt).reshape(-1)
    loff = (jnp.cumsum(cnt, axis=1) - cnt).reshape(-1)
    cntf = cnt.reshape(-1)
    n_valid = blk_end[-1:]
    blk_expert = jnp.minimum(
        jnp.sum(jnp.arange(n_blocks, dtype=jnp.int32)[:, None] >= blk_end[None, :], axis=1), N_EXPERTS - 1
    ).astype(jnp.int32)

    x_sorted = pl.pallas_call(
        _sort_kernel,
        out_shape=jax.ShapeDtypeStruct(x_sorted.shape, BF16),
        grid_spec=pltpu.PrefetchScalarGridSpec(
            num_scalar_prefetch=3,
            grid=(n_tiles,),
            in_specs=[pl.BlockSpec((tm, d), lambda i, *_: (i, 0)),
                      pl.BlockSpec((8, tm), lambda i, *_: (0, i)),
                      pl.BlockSpec(memory_space=pl.ANY)],
            out_specs=pl.BlockSpec(memory_space=pl.ANY),
            scratch_shapes=[pltpu.VMEM((2, s_loc, d), BF16), pltpu.SemaphoreType.DMA((2,))],
        ),
        input_output_aliases={5: 0},
        compiler_params=_cparams(("arbitrary",)),
        name="moe_sort",
    )(cntf, loff, goff, h2, info_row, x_sorted)

    blk = lambda b, be, nv: (jnp.minimum(b, nv[0] - 1), 0)
    d_sorted = pl.pallas_call(
        _experts_kernel,
        out_shape=jax.ShapeDtypeStruct(x_sorted.shape, BF16),
        grid_spec=pltpu.PrefetchScalarGridSpec(
            num_scalar_prefetch=2,
            grid=(n_blocks,),
            in_specs=[pl.BlockSpec((er, d), blk),
                      pl.BlockSpec((1, d, D_EXPERT), lambda b, be, nv: (be[b], 0, 0)),
                      pl.BlockSpec((1, d, D_EXPERT), lambda b, be, nv: (be[b], 0, 0)),
                      pl.BlockSpec((1, D_EXPERT, d), lambda b, be, nv: (be[b], 0, 0))],
            out_specs=pl.BlockSpec((er, d), lambda b, be, nv: (b, 0)),
        ),
        compiler_params=_cparams(("arbitrary",), VMEM_LIMIT),
        name="moe_experts",
    )(blk_expert, n_valid, x_sorted, w_eg, w_eu, w_ed)

    gt_spec = pl.BlockSpec((1, rm, d), (lambda i, *_: (i // tpg, 0, 0)) if rm == 1
                           else (lambda i, *_: (i // tpg, i % tpg, 0)))
    tok3 = pl.BlockSpec((1, tm, d), lambda i, *_: (i // tpg, i % tpg, 0))
    out = pl.pallas_call(
        functools.partial(_combine_kernel, rb=min(tm, 256)),
        out_shape=jax.ShapeDtypeStruct((g, t, d), F32),
        grid_spec=pltpu.PrefetchScalarGridSpec(
            num_scalar_prefetch=3,
            grid=(n_tiles,),
            in_specs=[pl.BlockSpec(memory_space=pl.ANY),
                      pl.BlockSpec((tm, 8), lambda i, *_: (i, 0)),
                      tok3, gt_spec],
            out_specs=tok3,
            scratch_shapes=[pltpu.VMEM((2, s_loc, d), BF16), pltpu.SemaphoreType.DMA((2,))],
        ),
        compiler_params=_cparams(("arbitrary",), VMEM_LIMIT),
        name="moe_combine",
    )(cntf, loff, goff, d_sorted, info_col, x, gt)
    return out, x_sorted


def _rope_tables(pos):
    half = HEAD_DK // 2
    inv_freq = ROPE_BASE ** (-jnp.arange(half, dtype=F32) / half)
    ang = pos[:, None] * inv_freq[None, :]
    cos, sin = jnp.cos(ang), jnp.sin(ang)
    return jnp.concatenate([cos, cos], axis=1), jnp.concatenate([-sin, sin], axis=1)


def _layer(x, mods, lw, cfg, ret0, gla0, cache_k, cache_v, cosf, sinf, x_sorted):
    g, t, d = x.shape
    b, tseq = cfg["b"], cfg["tseq"]
    sh_m, sc_m, gt_m, sh_f, sc_f, gt_f = mods
    proj, log_a = _in_proj(x, sc_m, sh_m, lw["g_mix"], lw["w_in"], lw["w_ga"], lw["a2"], lw["a_bias"],
                           cosf, sinf, lw["gq"], lw["gk"], cfg["tm"])
    proj_s = proj.reshape(b, tseq, PROJ_COLS)
    log_a_s = log_a.reshape(b, tseq, BRANCH_WIDTH)
    o_r, ret_new = _retention(proj_s, ret0, lw["g_ret_gn"], cfg["chunk"], cfg["tc"])
    o_g, gla_new = _gla(proj_s, log_a_s, gla0, lw["g_gla_gn"], cfg["chunk"], cfg["tc"])
    if cache_k is None:
        o_a = _band_attn_prompt(proj_s, lw["rel_bias"])
    else:
        o_a = _band_attn_sample(proj_s, cache_k, cache_v, lw["rel_bias"])
    tok = lambda a: a.reshape(g, t, BRANCH_WIDTH)
    x1, h2 = _merge(x, tok(o_r), tok(o_g), tok(o_a), (sc_m, sh_m, gt_m, sc_f, sh_f), lw["g_mix"], lw["g_ffn"],
                    lw["w_gate"], lw["b_gate"], lw["w_branch"], lw["w_out"], cfg["tm"])
    h2f = h2.reshape(g * t, d)
    info_row, tbl = _route(h2f, lw["wr_t"], lw["br_t"], cfg["tme"], SEG)
    x2, x_sorted = _moe(tbl[:, :, 0], h2f, x1, gt_f, info_row, info_row.T, lw["w_eg"], lw["w_eu"], lw["w_ed"],
                        x_sorted, cfg["tme"])
    tail = proj_s[:, -min(tseq, BAND_PAST):]
    k_new = tail[:, :, 9 * BRANCH_WIDTH:10 * BRANCH_WIDTH]
    v_new = tail[:, :, 10 * BRANCH_WIDTH:11 * BRANCH_WIDTH]
    return x2, ret_new, gla_new, k_new, v_new, x_sorted


def kernel(x_prompt, x_sample, state_ret, state_gla, cache_att_k, cache_att_v, c_prompt, c_sample,
           w_ada, b_ada, g_mix, w_in, gla_a2, gla_a_bias, g_ret_gn, g_gla_gn, g_q_att, g_k_att, rel_bias,
           w_branch, w_gate, b_gate, w_out, g_ffn, w_router_group, b_router_group, w_router_exp,
           b_router_exp, w_exp_gate, w_exp_up, w_exp_down):
    depth = w_ada.shape[0]
    bp, seq, d = x_prompt.shape
    bs, dseq, _ = x_sample.shape
    n_s = bs * dseq

    pad = (-(bp + bs)) % 8
    c_all = jnp.concatenate([c_prompt, c_sample, jnp.zeros((pad, d), F32)], axis=0)
    mod = _ada_mod(c_all, w_ada, b_ada)

    ga0 = 8 * BRANCH_WIDTH
    w_in_b = jnp.concatenate([w_in[:, :, :ga0], w_in[:, :, ga0 + GLA_RANK:]], axis=2).astype(BF16)
    w_ga = jnp.pad(w_in[:, :, ga0:ga0 + GLA_RANK], ((0, 0), (0, 0), (0, V7X_LANES - GLA_RANK))).astype(BF16)
    a2 = jnp.pad(gla_a2, ((0, 0), (0, V7X_LANES - GLA_RANK), (0, 0))).astype(BF16)
    wr_t = jnp.zeros((depth, 32, d), F32)
    wr_t = wr_t.at[:, 0:N_GROUPS].set(jnp.swapaxes(w_router_group, 1, 2))
    wr_t = wr_t.at[:, 8:8 + N_EXPERTS].set(jnp.swapaxes(w_router_exp, 1, 2)).astype(BF16)
    br_t = jnp.zeros((depth, 32, 1), F32)
    br_t = br_t.at[:, 0:N_GROUPS, 0].set(b_router_group).at[:, 8:8 + N_EXPERTS, 0].set(b_router_exp)
    w_gate_b, w_branch_b, w_out_b = w_gate.astype(BF16), w_branch.astype(BF16), w_out.astype(BF16)
    w_eg, w_eu, w_ed = w_exp_gate.astype(BF16), w_exp_up.astype(BF16), w_exp_down.astype(BF16)

    cos_p, sin_p = _rope_tables(jnp.arange(seq, dtype=F32))
    cos_s, sin_s = _rope_tables(PAST_LEN + jnp.arange(dseq, dtype=F32))
    cos_s, sin_s = jnp.tile(cos_s, (bs, 1)), jnp.tile(sin_s, (bs, 1))

    cfg_p = dict(b=bp, tseq=seq, tm=512, chunk=4 * CHUNK, tc=512, tme=512)
    cfg_s = dict(b=bs, tseq=dseq, tm=n_s, chunk=min(dseq, CHUNK), tc=dseq, tme=n_s)
    zero_state = jnp.zeros((bp, RET_HEADS, HEAD_DK, HEAD_DK), F32)
    sorted_p = jnp.zeros((_sorted_rows(bp * seq, cfg_p["tme"]), d), BF16)
    sorted_s = jnp.zeros((_sorted_rows(n_s, cfg_s["tme"]), d), BF16)

    xp = x_prompt
    xs = x_sample.reshape(1, n_s, d)
    outs = [[] for _ in range(8)]
    for l in range(depth):
        lw = dict(
            g_mix=g_mix[l][None], g_ffn=g_ffn[l][None], w_in=w_in_b[l], w_ga=w_ga[l], a2=a2[l],
            a_bias=gla_a_bias[l][None], gq=jnp.tile(g_q_att[l], ATT_HEADS)[None],
            gk=jnp.tile(g_k_att[l], ATT_HEADS)[None], g_ret_gn=g_ret_gn[l][None], g_gla_gn=g_gla_gn[l][None],
            rel_bias=rel_bias[l], w_gate=w_gate_b[l], b_gate=b_gate[l][None], w_branch=w_branch_b[l],
            w_out=w_out_b[l], wr_t=wr_t[l], br_t=br_t[l], w_eg=w_eg[l], w_eu=w_eu[l], w_ed=w_ed[l])
        mods_p = tuple(m[:, None, :] for m in jnp.split(mod[l, :bp], 6, axis=-1))
        mods_s = tuple(jnp.repeat(m, dseq, axis=0)[None] for m in jnp.split(mod[l, bp:bp + bs], 6, axis=-1))
        xp, rp, gp, kp, vp, sorted_p = _layer(xp, mods_p, lw, cfg_p, zero_state, zero_state, None, None,
                                              cos_p, sin_p, sorted_p)
        ck = cache_att_k[l].reshape(bs, BAND_PAST, BRANCH_WIDTH)
        cv = cache_att_v[l].reshape(bs, BAND_PAST, BRANCH_WIDTH)
        xs, rs, gs, ks, vs, sorted_s = _layer(xs, mods_s, lw, cfg_s, state_ret[l], state_gla[l], ck, cv,
                                              cos_s, sin_s, sorted_s)
        heads = lambda a: a.astype(F32).reshape(a.shape[0], a.shape[1], ATT_HEADS, ATT_HD)
        for lst, val in zip(outs, (rp, gp, heads(kp), heads(vp), rs, gs, heads(ks), heads(vs))):
            lst.append(val)
    return (xp, xs.reshape(bs, dseq, d)) + tuple(jnp.stack(o) for o in outs)
```

```python
import functools

import numpy as np
import jax
import jax.numpy as jnp
from jax import lax
from jax.experimental import pallas as pl
from jax.experimental.pallas import tpu as pltpu

F32 = jnp.float32
BF16 = jnp.bfloat16

D_MODEL = 1024
CHUNK = 64
BRANCH_WIDTH = 512
N_BRANCH = 3
RET_HEADS = 4
HEAD_DK = 128
GLA_RANK = 16
GLA_TAU = 16.0
ATT_HEADS = 8
ATT_HD = 64
BAND_PAST = 512
MAX_REL = 128
N_GROUPS = 4
EXPERTS_PER_GROUP = 4
N_EXPERTS = 16
D_EXPERT = 512
ROPE_BASE = 10000.0
EPS = 1e-6
GN_EPS = 1e-5
PAST_LEN = 1024
N_PROJ_BLOCKS = 11
PROJ_COLS = N_PROJ_BLOCKS * BRANCH_WIDTH

V7X_LANES = 128
V7X_VMEM_BYTES = 64 * 1024 * 1024
VMEM_LIMIT = 56 * 1024 * 1024
NEG_BIG = -1e30


def _cparams(sem, vmem=None):
    return pltpu.CompilerParams(dimension_semantics=sem, vmem_limit_bytes=vmem)


def _const_spec(shape):
    nd = len(shape)
    return pl.BlockSpec(shape, lambda *_: (0,) * nd, pipeline_mode=pl.Buffered(1))


def _layer_spec(shape, layer):
    nd = len(shape)
    return pl.BlockSpec((None,) + tuple(shape), lambda *_: (layer,) + (0,) * nd, pipeline_mode=pl.Buffered(1))


def _dot(a, b):
    return jnp.dot(a, b, preferred_element_type=F32)


def _dot_nt(a, b):
    return lax.dot_general(a, b, (((1,), (1,)), ((), ())), preferred_element_type=F32)


def _dot_tn(a, b):
    return lax.dot_general(a, b, (((0,), (0,)), ((), ())), preferred_element_type=F32)


def _silu(x):
    return x * jax.nn.sigmoid(x)


def _norm_mod(x, g, sc, sh):
    ms = jnp.mean(x * x, axis=-1, keepdims=True)
    return x * lax.rsqrt(ms + EPS) * g * (1.0 + sc) + sh


def _ada_kernel(c_ref, w_ref, b_ref, o_ref):
    s = _silu(c_ref[...])
    o_ref[0] = _dot(s.astype(BF16), w_ref[0].astype(BF16)) + b_ref[0]


def _ada_mod(c_all, w_ada, b_ada):
    depth, d, n = w_ada.shape
    rows = c_all.shape[0]
    bn = 1536
    return pl.pallas_call(
        _ada_kernel,
        out_shape=jax.ShapeDtypeStruct((depth, rows, n), F32),
        grid=(depth, n // bn),
        in_specs=[
            pl.BlockSpec((rows, d), lambda l, j: (0, 0)),
            pl.BlockSpec((1, d, bn), lambda l, j: (l, 0, j)),
            pl.BlockSpec((1, 1, bn), lambda l, j: (l, 0, j)),
        ],
        out_specs=pl.BlockSpec((1, rows, bn), lambda l, j: (l, 0, j)),
        compiler_params=_cparams(("arbitrary", "arbitrary"), VMEM_LIMIT),
        name="ada_mod",
    )(c_all, w_ada, b_ada.reshape(depth, 1, n))


def _rope_heads(a, cosf, sinf):
    outs = []
    for h in range(RET_HEADS):
        ah = a[:, h * HEAD_DK:(h + 1) * HEAD_DK]
        outs.append(ah * cosf + pltpu.roll(ah, HEAD_DK // 2, 1) * sinf)
    return jnp.concatenate(outs, axis=1)


def _rms_heads64(a, gain):
    low = lax.broadcasted_iota(jnp.int32, (1, V7X_LANES), 1) < ATT_HD
    outs = []
    for c in range(a.shape[1] // V7X_LANES):
        ac = a[:, c * V7X_LANES:(c + 1) * V7X_LANES]
        sq = ac * ac
        lo = jnp.sum(jnp.where(low, sq, 0.0), axis=-1, keepdims=True)
        hi = jnp.sum(jnp.where(low, 0.0, sq), axis=-1, keepdims=True)
        ms = jnp.where(low, lo, hi) * (1.0 / ATT_HD)
        outs.append(ac * lax.rsqrt(ms + EPS))
    return jnp.concatenate(outs, axis=1) * gain


def _in_proj_kernel(x_ref, sc_ref, sh_ref, g_ref, w_ref, wga_ref, a2_ref, ab_ref, cos_ref, sin_ref,
                    gq_ref, gk_ref, proj_ref, la_ref):
    x = x_ref[0]
    hb = _norm_mod(x, g_ref[...], sc_ref[0], sh_ref[0]).astype(BF16)
    cosf = cos_ref[...]
    sinf = sin_ref[...]
    ga = _dot(hb, wga_ref[...])
    for j in range(N_PROJ_BLOCKS):
        if j == N_PROJ_BLOCKS // 2:
            z = _dot(ga.astype(BF16), a2_ref[...]) + ab_ref[...]
            la_ref[0] = jax.nn.log_sigmoid(z) * (1.0 / GLA_TAU)
        cols = slice(j * BRANCH_WIDTH, (j + 1) * BRANCH_WIDTH)
        acc = _dot(hb, w_ref[:, cols])
        if j == 0:
            acc = _rope_heads(acc, cosf, sinf)
        elif j == 1:
            acc = _rope_heads(acc, cosf, sinf) * (HEAD_DK ** -0.5)
        elif j == 4:
            acc = acc * (HEAD_DK ** -0.5)
        elif j == 8:
            acc = _rms_heads64(acc, gq_ref[...]) * (ATT_HD ** -0.5)
        elif j == 9:
            acc = _rms_heads64(acc, gk_ref[...])
        proj_ref[0, :, cols] = acc.astype(BF16)


def _in_proj(x, sc, sh, g_mix, w_in, w_ga, a2, a_bias, cosf, sinf, gq, gk, tm, layer):
    g, t, d = x.shape
    rm = sc.shape[1]
    mod_spec = pl.BlockSpec((1, rm, d), (lambda b, i: (b, 0, 0)) if rm == 1 else (lambda b, i: (b, i, 0)))
    return pl.pallas_call(
        _in_proj_kernel,
        out_shape=(jax.ShapeDtypeStruct((g, t, PROJ_COLS), BF16),
                   jax.ShapeDtypeStruct((g, t, BRANCH_WIDTH), F32)),
        grid=(g, t // tm),
        in_specs=[
            pl.BlockSpec((1, tm, d), lambda b, i: (b, i, 0)),
            mod_spec, mod_spec,
            _const_spec((1, d)),
            _layer_spec((d, PROJ_COLS), layer),
            _const_spec((d, V7X_LANES)),
            _const_spec((V7X_LANES, BRANCH_WIDTH)),
            _const_spec((1, BRANCH_WIDTH)),
            pl.BlockSpec((tm, HEAD_DK), lambda b, i: (i, 0)),
            pl.BlockSpec((tm, HEAD_DK), lambda b, i: (i, 0)),
            _const_spec((1, BRANCH_WIDTH)),
            _const_spec((1, BRANCH_WIDTH)),
        ],
        out_specs=(pl.BlockSpec((1, tm, PROJ_COLS), lambda b, i: (b, i, 0)),
                   pl.BlockSpec((1, tm, BRANCH_WIDTH), lambda b, i: (b, i, 0))),
        compiler_params=_cparams(("arbitrary", "arbitrary"), VMEM_LIMIT),
        name="in_proj",
    )(x, sc, sh, g_mix, w_in, w_ga, a2, a_bias, cosf, sinf, gq, gk)


def _retention_kernel(q_ref, k_ref, v_ref, g_ref, s0_ref, dm_ref, qd_ref, kd_ref, bd_ref, gn_ref,
                      o_ref, so_ref, st_ref, *, chunk, n_chunks):
    @pl.when(pl.program_id(1) == 0)
    def _():
        st_ref[...] = s0_ref[0]

    def body(c, carry):
        rows = pl.ds(pl.multiple_of(c * chunk, chunk), chunk)
        for h in range(RET_HEADS):
            cols = slice(h * HEAD_DK, (h + 1) * HEAD_DK)
            q = q_ref[0, rows, cols]
            k = k_ref[0, rows, cols]
            v = v_ref[0, rows, cols]
            scores = _dot_nt(q, k) * dm_ref[h]
            o = _dot(scores.astype(BF16), v)
            o = o + _dot((q.astype(F32) * qd_ref[h]).astype(BF16), st_ref[h].astype(BF16))
            st_ref[h] = st_ref[h] * bd_ref[h] + _dot_tn((k.astype(F32) * kd_ref[h]).astype(BF16), v)
            mu = jnp.mean(o, axis=-1, keepdims=True)
            oc = o - mu
            var = jnp.mean(oc * oc, axis=-1, keepdims=True)
            on = oc * lax.rsqrt(var + GN_EPS) * gn_ref[:, cols]
            o_ref[0, rows, cols] = (_silu(g_ref[0, rows, cols].astype(F32)) * on).astype(BF16)
        return carry

    lax.fori_loop(0, n_chunks, body, 0)

    @pl.when(pl.program_id(1) == pl.num_programs(1) - 1)
    def _():
        so_ref[0] = st_ref[...]


def _retention_tables(chunk):
    log_gamma = jnp.log(1.0 - jnp.exp2(-5.0 - jnp.arange(RET_HEADS, dtype=F32)))
    idx = jnp.arange(chunk, dtype=F32)
    diff = idx[:, None] - idx[None, :]
    dmask = jnp.where(diff >= 0, jnp.exp(log_gamma[:, None, None] * jnp.maximum(diff, 0.0)), 0.0)
    qd = jnp.exp(log_gamma[:, None] * (idx + 1.0))[:, :, None]
    kd = jnp.exp(log_gamma[:, None] * (chunk - 1.0 - idx))[:, :, None]
    bd = jnp.exp(log_gamma * chunk)[:, None, None]
    bc = lambda a, r: jnp.broadcast_to(a, (RET_HEADS, r, HEAD_DK)).astype(F32)
    return dmask.astype(F32), bc(qd, chunk), bc(kd, chunk), bc(bd, 1)


def _retention(proj, state0, gn, chunk, tc):
    b, t, _ = proj.shape
    dm, qd, kd, bd = _retention_tables(chunk)
    col = lambda j: pl.BlockSpec((1, tc, BRANCH_WIDTH), lambda i, s, j=j: (i, s, j))
    st_spec = pl.BlockSpec((1, RET_HEADS, HEAD_DK, HEAD_DK), lambda i, s: (i, 0, 0, 0))
    return pl.pallas_call(
        functools.partial(_retention_kernel, chunk=chunk, n_chunks=tc // chunk),
        out_shape=(jax.ShapeDtypeStruct((b, t, BRANCH_WIDTH), BF16),
                   jax.ShapeDtypeStruct((b, RET_HEADS, HEAD_DK, HEAD_DK), F32)),
        grid=(b, t // tc),
        in_specs=[col(0), col(1), col(2), col(3), st_spec,
                  _const_spec(dm.shape), _const_spec(qd.shape), _const_spec(kd.shape), _const_spec(bd.shape),
                  _const_spec((1, BRANCH_WIDTH))],
        out_specs=(pl.BlockSpec((1, tc, BRANCH_WIDTH), lambda i, s: (i, s, 0)), st_spec),
        scratch_shapes=[pltpu.VMEM((RET_HEADS, HEAD_DK, HEAD_DK), F32)],
        compiler_params=_cparams(("arbitrary", "arbitrary")),
        name="retention",
    )(proj, proj, proj, proj, state0, dm, qd, kd, bd, gn)


GLA_SAFE_LOG_DECAY = -60.0


def _gla_intra_direct(q, bh, rows_ref, chunk):
    trow = lax.broadcasted_iota(jnp.int32, (chunk, 1), 0)

    def step(s, acc):
        bs = rows_ref[0, pl.ds(s, 1), :]
        ks = rows_ref[1, pl.ds(s, 1), :]
        vs = rows_ref[2, pl.ds(s, 1), :]
        e = jnp.exp(jnp.minimum(bh - bs, 0.0))
        col = jnp.sum(q * ks * e, axis=-1, keepdims=True)
        return acc + jnp.where(trow >= s, col, 0.0) * vs

    return lax.fori_loop(0, chunk, step, jnp.zeros((chunk, HEAD_DK), F32))


def _gla_kernel(q_ref, k_ref, v_ref, r_ref, la_ref, s0_ref, tri_ref, gn_ref, o_ref, so_ref, st_ref, rows_ref,
                *, chunk, n_chunks):
    @pl.when(pl.program_id(1) == 0)
    def _():
        for h in range(RET_HEADS):
            st_ref[h] = s0_ref[0, h].T

    tri = tri_ref[...]
    causal = (lax.broadcasted_iota(jnp.int32, (chunk, chunk), 0)
              >= lax.broadcasted_iota(jnp.int32, (chunk, chunk), 1))

    def chunk_body(factored):
        def body(c, carry):
            rows = pl.ds(pl.multiple_of(c * chunk, chunk), chunk)
            la = la_ref[0, rows, :]
            la_hi = la.astype(BF16)
            la_lo = (la - la_hi.astype(F32)).astype(BF16)
            b_all = _dot(tri, la_hi) + _dot(tri, la_lo)
            for h in range(RET_HEADS):
                cols = slice(h * HEAD_DK, (h + 1) * HEAD_DK)
                q = q_ref[0, rows, cols].astype(F32)
                k = k_ref[0, rows, cols].astype(F32)
                v = v_ref[0, rows, cols]
                bh = b_all[:, cols]
                bl = bh[chunk - 1:chunk, :]
                qh = (q * jnp.exp(bh)).astype(BF16)
                if factored:
                    kh = (k * jnp.exp(-bh)).astype(BF16)
                    scores = jnp.where(causal, _dot_nt(qh, kh), 0.0)
                    o = _dot(scores.astype(BF16), v)
                else:
                    rows_ref[0] = bh
                    rows_ref[1] = k
                    rows_ref[2] = v.astype(F32)
                    o = _gla_intra_direct(q, bh, rows_ref, chunk)
                o = o + _dot_nt(qh, st_ref[h].astype(BF16))
                kd = (k * jnp.exp(bl - bh)).astype(BF16)
                st_ref[h] = st_ref[h] * jnp.exp(bl) + _dot_tn(v, kd)
                ms = jnp.mean(o * o, axis=-1, keepdims=True)
                on = o * lax.rsqrt(ms + EPS) * gn_ref[:, cols]
                o_ref[0, rows, cols] = (_silu(r_ref[0, rows, cols].astype(F32)) * on).astype(BF16)
            return carry
        return body

    lowest = None
    for c in range(n_chunks):
        tot = jnp.sum(la_ref[0, c * chunk:(c + 1) * chunk, :], axis=0, keepdims=True)
        lowest = tot if lowest is None else jnp.minimum(lowest, tot)
    safe = jnp.min(lowest) > GLA_SAFE_LOG_DECAY

    @pl.when(safe)
    def _():
        lax.fori_loop(0, n_chunks, chunk_body(True), 0)

    @pl.when(jnp.logical_not(safe))
    def _():
        lax.fori_loop(0, n_chunks, chunk_body(False), 0)

    @pl.when(pl.program_id(1) == pl.num_programs(1) - 1)
    def _():
        for h in range(RET_HEADS):
            so_ref[0, h] = st_ref[h].T


def _gla(proj, log_a, state0, gn, chunk, tc):
    b, t, _ = proj.shape
    tri = jnp.tril(jnp.ones((chunk, chunk), F32)).astype(BF16)
    col = lambda j: pl.BlockSpec((1, tc, BRANCH_WIDTH), lambda i, s, j=j: (i, s, j))
    st_spec = pl.BlockSpec((1, RET_HEADS, HEAD_DK, HEAD_DK), lambda i, s: (i, 0, 0, 0))
    return pl.pallas_call(
        functools.partial(_gla_kernel, chunk=chunk, n_chunks=tc // chunk),
        out_shape=(jax.ShapeDtypeStruct((b, t, BRANCH_WIDTH), BF16),
                   jax.ShapeDtypeStruct((b, RET_HEADS, HEAD_DK, HEAD_DK), F32)),
        grid=(b, t // tc),
        in_specs=[col(4), col(5), col(6), col(7),
                  pl.BlockSpec((1, tc, BRANCH_WIDTH), lambda i, s: (i, s, 0)),
                  st_spec, _const_spec((chunk, chunk)), _const_spec((1, BRANCH_WIDTH))],
        out_specs=(pl.BlockSpec((1, tc, BRANCH_WIDTH), lambda i, s: (i, s, 0)), st_spec),
        scratch_shapes=[pltpu.VMEM((RET_HEADS, HEAD_DK, HEAD_DK), F32),
                        pltpu.VMEM((3, chunk, HEAD_DK), F32)],
        compiler_params=_cparams(("arbitrary", "arbitrary")),
        name="gla",
    )(proj, proj, proj, proj, log_a, state0, tri, gn)


def _attn_stage_scratch(rb, win):
    pairs = ATT_HEADS // 2
    return [pltpu.VMEM((2, pairs, 2 * rb, win), F32), pltpu.VMEM((2, pairs, 2 * rb, win), BF16),
            pltpu.VMEM((2, pairs, 2 * rb, V7X_LANES), F32)]


def _band_attn_kernel(q_ref, kp_ref, kc_ref, vp_ref, vc_ref, bias_ref, o_ref, kw_ref, vw_ref,
                      s_ref, e_ref, den_ref,
                      *, n_sub, rb, win, prev_rows, mask_start):
    cur_rows = kc_ref.shape[1]
    kw_ref[0:prev_rows, :] = kp_ref[0].astype(BF16)
    kw_ref[prev_rows:prev_rows + cur_rows, :] = kc_ref[0]
    vw_ref[0:prev_rows, :] = vp_ref[0].astype(BF16)
    vw_ref[prev_rows:prev_rows + cur_rows, :] = vc_ref[0]
    low = lax.broadcasted_iota(jnp.int32, (1, V7X_LANES), 1) < ATT_HD
    q0 = pl.program_id(1) * cur_rows
    pairs = ATT_HEADS // 2

    def scores(i):
        for p in range(pairs):
            cols = slice(p * V7X_LANES, (p + 1) * V7X_LANES)
            q2 = q_ref[0, i * rb:(i + 1) * rb, cols]
            zero = jnp.zeros_like(q2)
            qst = jnp.concatenate([jnp.where(low, q2, zero), jnp.where(low, zero, q2)], axis=0)
            s_ref[i % 2, p] = _dot_nt(qst, kw_ref[i * rb:i * rb + win, cols])

    def softmax(i, masked):
        if masked:
            valid = lax.broadcasted_iota(jnp.int32, (1, win), 1) >= prev_rows - q0 - i * rb
        for p in range(pairs):
            s = s_ref[i % 2, p] + bias_ref[p]
            if masked:
                s = jnp.where(valid, s, NEG_BIG)
            m = jnp.max(s, axis=-1, keepdims=True)
            e = jnp.exp(s - m)
            den_ref[i % 2, p] = jnp.broadcast_to(jnp.sum(e, axis=-1, keepdims=True), den_ref.shape[2:])
            e_ref[i % 2, p] = e.astype(BF16)

    def values(i):
        for p in range(pairs):
            cols = slice(p * V7X_LANES, (p + 1) * V7X_LANES)
            o2 = _dot(e_ref[i % 2, p], vw_ref[i * rb:i * rb + win, cols]) / den_ref[i % 2, p]
            o_ref[0, i * rb:(i + 1) * rb, cols] = jnp.where(low, o2[:rb], o2[rb:]).astype(BF16)

    def pipeline(masked):
        for step in range(n_sub + 2):
            if step < n_sub:
                scores(step)
            if 0 <= step - 1 < n_sub:
                softmax(step - 1, masked)
            if 0 <= step - 2 < n_sub:
                values(step - 2)

    if mask_start:
        pl.when(q0 < prev_rows)(functools.partial(pipeline, True))
        pl.when(q0 >= prev_rows)(functools.partial(pipeline, False))
    else:
        pipeline(False)


def _rel_bias_table(rel_bias, rb, win, chunk):
    period = win + rb
    u = np.arange(period)
    u = np.where(u < win, u, u - period)
    vec = rel_bias.astype(F32)[:, np.clip(BAND_PAST - u, -MAX_REL, MAX_REL) + MAX_REL]
    bias = jnp.tile(vec, (1, rb))[:, :rb * (period - 1)].reshape(-1, rb, period - 1)[:, :, :win]
    r = np.arange(rb)[:, None]
    j = np.arange(win)[None, :]
    c0 = (r // chunk) * chunk
    allowed = (j >= c0) & (j < c0 + BAND_PAST + chunk)
    bias = jnp.where(jnp.asarray(allowed)[None], bias, NEG_BIG)
    return bias.reshape(ATT_HEADS // 2, 2 * rb, win)


def _band_attn_prompt(proj, rel_bias):
    b, t, _ = proj.shape
    qb, rb = 512, 128
    win = BAND_PAST + rb
    bias = _rel_bias_table(rel_bias, rb, win, CHUNK)
    cur = lambda j: pl.BlockSpec((1, qb, BRANCH_WIDTH), lambda i, s, j=j: (i, s, j))
    prev = lambda j: pl.BlockSpec((1, qb, BRANCH_WIDTH), lambda i, s, j=j: (i, jnp.maximum(s - 1, 0), j))
    return pl.pallas_call(
        functools.partial(_band_attn_kernel, n_sub=qb // rb, rb=rb, win=win, prev_rows=BAND_PAST,
                          mask_start=True),
        out_shape=jax.ShapeDtypeStruct((b, t, BRANCH_WIDTH), BF16),
        grid=(b, t // qb),
        in_specs=[cur(8), prev(9), cur(9), prev(10), cur(10), _const_spec(bias.shape)],
        out_specs=pl.BlockSpec((1, qb, BRANCH_WIDTH), lambda i, s: (i, s, 0)),
        scratch_shapes=[pltpu.VMEM((BAND_PAST + qb, BRANCH_WIDTH), BF16),
                        pltpu.VMEM((BAND_PAST + qb, BRANCH_WIDTH), BF16)] + _attn_stage_scratch(rb, win),
        compiler_params=_cparams(("arbitrary", "arbitrary")),
        name="band_attn",
    )(proj, proj, proj, proj, proj, bias)


def _band_attn_sample_kernel(q_ref, kn_ref, vn_ref, kc_ref, vc_ref, bias_ref, o_ref):
    t = q_ref.shape[1]
    past = kc_ref.shape[-1]
    low = lax.broadcasted_iota(jnp.int32, (1, V7X_LANES), 1) < ATT_HD
    for p in range(ATT_HEADS // 2):
        cols = slice(p * V7X_LANES, (p + 1) * V7X_LANES)
        q2 = q_ref[0, :, cols]
        zero = jnp.zeros_like(q2)
        qst = jnp.concatenate([jnp.where(low, q2, zero), jnp.where(low, zero, q2)], axis=0)
        kt = kc_ref[0, 2 * p:2 * p + 2].reshape(V7X_LANES, past).astype(BF16)
        vt = vc_ref[0, 2 * p:2 * p + 2].reshape(V7X_LANES, past).astype(BF16)
        s = jnp.concatenate([_dot(qst, kt), _dot_nt(qst, kn_ref[0, :, cols])], axis=1) + bias_ref[p]
        m = jnp.max(s, axis=-1, keepdims=True)
        e = jnp.exp(s - m)
        den = jnp.sum(e, axis=-1, keepdims=True)
        eb = e.astype(BF16)
        o2 = (_dot_nt(eb[:, :past], vt) + _dot(eb[:, past:], vn_ref[0, :, cols])) / den
        o_ref[0, :, cols] = jnp.where(low, o2[:t], o2[t:]).astype(BF16)


def _band_attn_sample(proj, cache_kt, cache_vt, rel_bias, layer):
    b, t, _ = proj.shape
    win = BAND_PAST + t
    bias = _rel_bias_table(rel_bias, t, win, t)
    cur = lambda j: pl.BlockSpec((1, t, BRANCH_WIDTH), lambda i, j=j: (i, 0, j))
    cache = pl.BlockSpec((None, 1, ATT_HEADS, ATT_HD, BAND_PAST), lambda i: (layer, i, 0, 0, 0))
    return pl.pallas_call(
        _band_attn_sample_kernel,
        out_shape=jax.ShapeDtypeStruct((b, t, BRANCH_WIDTH), BF16),
        grid=(b,),
        in_specs=[cur(8), cur(9), cur(10), cache, cache, _const_spec(bias.shape)],
        out_specs=pl.BlockSpec((1, t, BRANCH_WIDTH), lambda i: (i, 0, 0)),
        compiler_params=_cparams(("arbitrary",)),
        name="band_attn_sample",
    )(proj, proj, proj, cache_kt, cache_vt, bias)


def _merge_kernel(x_ref, or_ref, og_ref, oa_ref, scm_ref, shm_ref, gtm_ref, scf_ref, shf_ref,
                  gmix_ref, gffn_ref, wg_ref, bg_ref, wb_ref, wo_ref, xo_ref, h2_ref):
    x = x_ref[0]
    d = x.shape[1]
    hb = _norm_mod(x, gmix_ref[...], scm_ref[0], shm_ref[0]).astype(BF16)
    merged = None
    for n, o_ref in enumerate((or_ref, og_ref, oa_ref)):
        gate = jax.nn.sigmoid(_dot(hb, wg_ref[:, n * d:(n + 1) * d]) + bg_ref[:, n * d:(n + 1) * d])
        y = gate * _dot(o_ref[0], wb_ref[n])
        merged = y if merged is None else merged + y
    mix = _dot(merged.astype(BF16), wo_ref[...])
    xn = x + gtm_ref[0] * mix
    xo_ref[0] = xn
    h2_ref[0] = _norm_mod(xn, gffn_ref[...], scf_ref[0], shf_ref[0]).astype(BF16)


def _merge(x, o_r, o_g, o_a, mods, g_mix, g_ffn, w_gate, b_gate, w_branch, w_out, tm, layer):
    g, t, d = x.shape
    rm = mods[0].shape[1]
    mod_spec = pl.BlockSpec((1, rm, d), (lambda b, i: (b, 0, 0)) if rm == 1 else (lambda b, i: (b, i, 0)))
    tok = lambda w: pl.BlockSpec((1, tm, w), lambda b, i: (b, i, 0))
    return pl.pallas_call(
        _merge_kernel,
        out_shape=(jax.ShapeDtypeStruct((g, t, d), F32), jax.ShapeDtypeStruct((g, t, d), BF16)),
        grid=(g, t // tm),
        in_specs=[tok(d), tok(BRANCH_WIDTH), tok(BRANCH_WIDTH), tok(BRANCH_WIDTH)] + [mod_spec] * 5 + [
            _const_spec((1, d)), _const_spec((1, d)),
            _layer_spec((d, N_BRANCH * d), layer), _const_spec((1, N_BRANCH * d)),
            _layer_spec((N_BRANCH, BRANCH_WIDTH, d), layer), _layer_spec((d, d), layer)],
        out_specs=(tok(d), tok(d)),
        compiler_params=_cparams(("arbitrary", "arbitrary"), VMEM_LIMIT),
        name="merge",
    )(x, o_r, o_g, o_a, *mods, g_mix, g_ffn, w_gate, b_gate, w_branch, w_out)


def _first_argmax(vals, n):
    row = lax.broadcasted_iota(jnp.int32, vals.shape, 0).astype(F32)
    m = jnp.max(vals, axis=0, keepdims=True)
    idx = jnp.min(jnp.where(vals == m, row, float(n)), axis=0, keepdims=True)
    return m, idx


def _route_kernel(h_ref, wr_ref, br_ref, up_ref, l16_ref, info_ref, tbl_ref, *, rows_per_block):
    tm = h_ref.shape[0]
    lg = _dot_nt(wr_ref[...], h_ref[...]) + br_ref[...]
    g = lg[0:N_GROUPS]
    gmax, grp = _first_argmax(g, N_GROUPS)
    p_group = 1.0 / jnp.sum(jnp.exp(g - gmax), axis=0, keepdims=True)
    esel = jnp.zeros((EXPERTS_PER_GROUP, tm), F32)
    for gi in range(N_GROUPS):
        blk = lg[8 + gi * EXPERTS_PER_GROUP:8 + (gi + 1) * EXPERTS_PER_GROUP]
        esel = esel + jnp.where(grp == float(gi), blk, 0.0)
    v1, i1 = _first_argmax(esel, EXPERTS_PER_GROUP)
    row4 = lax.broadcasted_iota(jnp.int32, esel.shape, 0).astype(F32)
    v2, i2 = _first_argmax(jnp.where(row4 == i1, -jnp.inf, esel), EXPERTS_PER_GROUP)
    e21 = jnp.exp(v2 - v1)
    w1 = p_group / (1.0 + e21)
    w2 = p_group * e21 / (1.0 + e21)
    e1 = grp * float(EXPERTS_PER_GROUP) + i1
    e2 = grp * float(EXPERTS_PER_GROUP) + i2
    row16 = lax.broadcasted_iota(jnp.int32, (N_EXPERTS, tm), 0).astype(F32)
    hit1 = row16 == e1
    hit2 = row16 == e2
    onehot = jnp.where(hit1 | hit2, 1.0, 0.0)
    prefix = _dot(onehot.astype(BF16), up_ref[...])
    cnt = jnp.sum(onehot, axis=1, keepdims=True)
    nblk = jnp.floor((cnt + float(rows_per_block - 1)) * (1.0 / rows_per_block))
    nblk_b = jnp.broadcast_to(nblk, (N_EXPERTS, V7X_LANES))
    offb = _dot(l16_ref[...], nblk_b.astype(BF16))[:, 0:1]
    base = offb * float(rows_per_block) + prefix
    info_ref[0:1, :] = jnp.sum(jnp.where(hit1, base, 0.0), axis=0, keepdims=True)
    info_ref[1:2, :] = jnp.sum(jnp.where(hit2, base, 0.0), axis=0, keepdims=True)
    info_ref[2:3, :] = w1
    info_ref[3:4, :] = w2
    info_ref[4:8, :] = jnp.zeros((4, tm), F32)
    tbl_ref[0] = nblk_b.astype(jnp.int32)


def _route(h2, wr_t, br_t, tm, rows_per_block):
    n, d = h2.shape
    nt = n // tm
    upper = jnp.triu(jnp.ones((tm, tm), F32), 1).astype(BF16)
    l16 = jnp.tril(jnp.ones((N_EXPERTS, N_EXPERTS), F32), -1).astype(BF16)
    return pl.pallas_call(
        functools.partial(_route_kernel, rows_per_block=rows_per_block),
        out_shape=(jax.ShapeDtypeStruct((8, n), F32),
                   jax.ShapeDtypeStruct((nt, N_EXPERTS, V7X_LANES), jnp.int32)),
        grid=(nt,),
        in_specs=[pl.BlockSpec((tm, d), lambda i: (i, 0)),
                  _const_spec((32, d)), _const_spec((32, 1)),
                  _const_spec((tm, tm)), _const_spec((N_EXPERTS, N_EXPERTS))],
        out_specs=(pl.BlockSpec((8, tm), lambda i: (0, i)),
                   pl.BlockSpec((1, N_EXPERTS, V7X_LANES), lambda i: (i, 0, 0))),
        compiler_params=_cparams(("arbitrary",)),
        name="route",
    )(h2, wr_t, br_t, upper, l16)


SEG = 16
GATHER_ROWS = 256
MAX_EXPERT_ROWS = 512


def _expert_rows(n_tokens):
    mean_rows = 2 * n_tokens // N_EXPERTS
    return int(min(MAX_EXPERT_ROWS, max(4 * SEG, 1 << (mean_rows // 4).bit_length())))


def _segment_copies(cnt_ref, loff_ref, goff_ref, tile, local_ref, global_ref, sem, to_global):
    for e in range(N_EXPERTS):
        cnt = cnt_ref[tile * N_EXPERTS + e]
        rows = pl.multiple_of(cnt * SEG, SEG)
        loc = local_ref.at[pl.ds(pl.multiple_of(loff_ref[tile * N_EXPERTS + e] * SEG, SEG), rows)]
        glo = global_ref.at[pl.ds(pl.multiple_of(goff_ref[tile * N_EXPERTS + e] * SEG, SEG), rows)]
        cp = pltpu.make_async_copy(loc, glo, sem) if to_global else pltpu.make_async_copy(glo, loc, sem)
        yield cnt > 0, cp


def _tile_units(cnt_ref, tile):
    total = jnp.int32(0)
    for e in range(N_EXPERTS):
        total = total + cnt_ref[tile * N_EXPERTS + e]
    return total


def _sort_kernel(cnt_ref, loff_ref, goff_ref, h_ref, irow_ref, xin_ref, xout_ref, xs2_ref, sem2):
    del xin_ref
    i = pl.program_id(0)
    last = pl.num_programs(0) - 1
    slot = lax.rem(i, 2)
    xs_ref = xs2_ref.at[slot]
    sem = sem2.at[slot]

    def wait_tile(tile, s):
        for pred, cp in _segment_copies(cnt_ref, loff_ref, goff_ref, tile, xs2_ref.at[s], xout_ref, sem2.at[s], True):
            pl.when(pred)(cp.wait)

    @pl.when(i >= 2)
    def _():
        wait_tile(i - 2, slot)

    gr = GATHER_ROWS
    n_gather = (_tile_units(cnt_ref, i) * SEG + gr - 1) // gr
    pos1 = irow_ref[0:1, :]
    pos2 = irow_ref[1:2, :]
    h = h_ref[...]

    def gather(gb, carry):
        r0 = pl.multiple_of(gb * gr, gr)
        srow = (lax.broadcasted_iota(jnp.int32, (gr, 1), 0) + r0).astype(F32)
        sel = jnp.where((srow == pos1) | (srow == pos2), 1.0, 0.0).astype(BF16)
        xs_ref[pl.ds(r0, gr), :] = _dot(sel, h).astype(BF16)
        return carry

    lax.fori_loop(0, n_gather, gather, 0)
    for pred, cp in _segment_copies(cnt_ref, loff_ref, goff_ref, i, xs_ref, xout_ref, sem, True):
        pl.when(pred)(cp.start)

    @pl.when(i == last)
    def _():
        @pl.when(i >= 1)
        def _():
            wait_tile(i - 1, 1 - slot)

        wait_tile(i, slot)


def _experts_kernel(be_ref, nv_ref, x_ref, wg_ref, wu_ref, wd_ref, o_ref):
    del be_ref
    used = pl.program_id(0) < nv_ref[0]

    @pl.when(used)
    def _():
        xb = x_ref[...]
        a = _silu(_dot(xb, wg_ref[0])) * _dot(xb, wu_ref[0])
        o_ref[...] = _dot(a.astype(BF16), wd_ref[0]).astype(BF16)

    @pl.when(jnp.logical_not(used))
    def _():
        o_ref[...] = jnp.zeros(o_ref.shape, BF16)


def _combine_kernel(cnt_ref, loff_ref, goff_ref, ds_ref, icol_ref, x_ref, gt_ref, o_ref, dl2_ref, sem2, *, rb):
    i = pl.program_id(0)
    tm = x_ref.shape[1]
    s_loc = dl2_ref.shape[1]
    slot = lax.rem(i, 2)

    def fetch_tile(tile, s):
        for pred, cp in _segment_copies(cnt_ref, loff_ref, goff_ref, tile, dl2_ref.at[s], ds_ref, sem2.at[s], False):
            pl.when(pred)(cp.start)

        def zero_tail(u, carry):
            dl2_ref[s, pl.ds(pl.multiple_of(u * SEG, SEG), SEG), :] = jnp.zeros((SEG, dl2_ref.shape[2]), BF16)
            return carry

        lax.fori_loop(_tile_units(cnt_ref, tile), s_loc // SEG, zero_tail, 0)

    @pl.when(i == 0)
    def _():
        fetch_tile(0, 0)

    @pl.when(i + 1 < pl.num_programs(0))
    def _():
        fetch_tile(i + 1, 1 - slot)

    for pred, cp in _segment_copies(cnt_ref, loff_ref, goff_ref, i, dl2_ref.at[slot], ds_ref, sem2.at[slot], False):
        pl.when(pred)(cp.wait)
    scol = lax.broadcasted_iota(jnp.int32, (1, s_loc), 1).astype(F32)
    for r in range(tm // rb):
        rows = slice(r * rb, (r + 1) * rb)
        cmb = (jnp.where(scol == icol_ref[rows, 0:1], icol_ref[rows, 2:3], 0.0)
               + jnp.where(scol == icol_ref[rows, 1:2], icol_ref[rows, 3:4], 0.0))
        y = _dot(cmb.astype(BF16), dl2_ref[slot])
        gt = gt_ref[0] if gt_ref.shape[1] == 1 else gt_ref[0, rows, :]
        o_ref[0, rows, :] = x_ref[0, rows, :] + gt * y


def _sorted_rows(n_tokens, tm):
    n_tiles = n_tokens // tm
    er = _expert_rows(n_tokens)
    rows = 2 * n_tokens + n_tiles * N_EXPERTS * (SEG - 1) + N_EXPERTS * (er - 1)
    return -(-rows // er) * er


def _moe(cnt, h2, x, gt, info_row, info_col, w_eg, w_eu, w_ed, x_sorted, tm, layer):
    g, t, d = x.shape
    n = g * t
    n_tiles = n // tm
    tpg = t // tm
    rm = gt.shape[1]
    er = _expert_rows(n)
    n_blocks = x_sorted.shape[0] // er
    s_loc = -(-(2 * tm + N_EXPERTS * (SEG - 1)) // GATHER_ROWS) * GATHER_ROWS
    upb = er // SEG
    reg_blk = (jnp.sum(cnt, axis=0) + upb - 1) // upb
    blk_end = jnp.cumsum(reg_blk)
    goff = ((blk_end - reg_blk)[None, :] * upb + jnp.cumsum(cnt, axis=0) - cnt).reshape(-1)
    loff = (jnp.cumsum(cnt, axis=1) - cnt).reshape(-1)
    cntf = cnt.reshape(-1)
    n_valid = blk_end[-1:]
    blk_expert = jnp.minimum(
        jnp.sum(jnp.arange(n_blocks, dtype=jnp.int32)[:, None] >= blk_end[None, :], axis=1), N_EXPERTS - 1
    ).astype(jnp.int32)

    x_sorted = pl.pallas_call(
        _sort_kernel,
        out_shape=jax.ShapeDtypeStruct(x_sorted.shape, BF16),
        grid_spec=pltpu.PrefetchScalarGridSpec(
            num_scalar_prefetch=3,
            grid=(n_tiles,),
            in_specs=[pl.BlockSpec((tm, d), lambda i, *_: (i, 0)),
                      pl.BlockSpec((8, tm), lambda i, *_: (0, i)),
                      pl.BlockSpec(memory_space=pl.ANY)],
            out_specs=pl.BlockSpec(memory_space=pl.ANY),
            scratch_shapes=[pltpu.VMEM((2, s_loc, d), BF16), pltpu.SemaphoreType.DMA((2,))],
        ),
        input_output_aliases={5: 0},
        compiler_params=_cparams(("arbitrary",)),
        name="moe_sort",
    )(cntf, loff, goff, h2, info_row, x_sorted)

    blk = lambda b, be, nv: (jnp.minimum(b, nv[0] - 1), 0)
    d_sorted = pl.pallas_call(
        _experts_kernel,
        out_shape=jax.ShapeDtypeStruct(x_sorted.shape, BF16),
        grid_spec=pltpu.PrefetchScalarGridSpec(
            num_scalar_prefetch=2,
            grid=(n_blocks,),
            in_specs=[pl.BlockSpec((er, d), blk),
                      pl.BlockSpec((None, 1, d, D_EXPERT), lambda b, be, nv: (layer, be[b], 0, 0)),
                      pl.BlockSpec((None, 1, d, D_EXPERT), lambda b, be, nv: (layer, be[b], 0, 0)),
                      pl.BlockSpec((None, 1, D_EXPERT, d), lambda b, be, nv: (layer, be[b], 0, 0))],
            out_specs=pl.BlockSpec((er, d), lambda b, be, nv: (b, 0)),
        ),
        compiler_params=_cparams(("arbitrary",), VMEM_LIMIT),
        name="moe_experts",
    )(blk_expert, n_valid, x_sorted, w_eg, w_eu, w_ed)

    gt_spec = pl.BlockSpec((1, rm, d), (lambda i, *_: (i // tpg, 0, 0)) if rm == 1
                           else (lambda i, *_: (i // tpg, i % tpg, 0)))
    tok3 = pl.BlockSpec((1, tm, d), lambda i, *_: (i // tpg, i % tpg, 0))
    out = pl.pallas_call(
        functools.partial(_combine_kernel, rb=min(tm, 256)),
        out_shape=jax.ShapeDtypeStruct((g, t, d), F32),
        grid_spec=pltpu.PrefetchScalarGridSpec(
            num_scalar_prefetch=3,
            grid=(n_tiles,),
            in_specs=[pl.BlockSpec(memory_space=pl.ANY),
                      pl.BlockSpec((tm, 8), lambda i, *_: (i, 0)),
                      tok3, gt_spec],
            out_specs=tok3,
            scratch_shapes=[pltpu.VMEM((2, s_loc, d), BF16), pltpu.SemaphoreType.DMA((2,))],
        ),
        compiler_params=_cparams(("arbitrary",), VMEM_LIMIT),
        name="moe_combine",
    )(cntf, loff, goff, d_sorted, info_col, x, gt)
    return out, x_sorted


def _rope_tables(pos):
    half = HEAD_DK // 2
    inv_freq = ROPE_BASE ** (-jnp.arange(half, dtype=F32) / half)
    ang = pos[:, None] * inv_freq[None, :]
    cos, sin = jnp.cos(ang), jnp.sin(ang)
    return jnp.concatenate([cos, cos], axis=1), jnp.concatenate([-sin, sin], axis=1)


def _layer(x, mods, lw, cfg, ret0, gla0, cache_k, cache_v, cosf, sinf, x_sorted):
    g, t, d = x.shape
    b, tseq = cfg["b"], cfg["tseq"]
    sh_m, sc_m, gt_m, sh_f, sc_f, gt_f = mods
    proj, log_a = _in_proj(x, sc_m, sh_m, lw["g_mix"], lw["w_in"], lw["w_ga"], lw["a2"], lw["a_bias"],
                           cosf, sinf, lw["gq"], lw["gk"], cfg["tm"], lw["layer"])
    proj_s = proj.reshape(b, tseq, PROJ_COLS)
    log_a_s = log_a.reshape(b, tseq, BRANCH_WIDTH)
    o_r, ret_new = _retention(proj_s, ret0, lw["g_ret_gn"], cfg["chunk"], cfg["tc"])
    o_g, gla_new = _gla(proj_s, log_a_s, gla0, lw["g_gla_gn"], cfg["chunk"], cfg["tc"])
    if cache_k is None:
        o_a = _band_attn_prompt(proj_s, lw["rel_bias"])
    else:
        o_a = _band_attn_sample(proj_s, cache_k, cache_v, lw["rel_bias"], lw["layer"])
    tok = lambda a: a.reshape(g, t, BRANCH_WIDTH)
    x1, h2 = _merge(x, tok(o_r), tok(o_g), tok(o_a), (sc_m, sh_m, gt_m, sc_f, sh_f), lw["g_mix"], lw["g_ffn"],
                    lw["w_gate"], lw["b_gate"], lw["w_branch"], lw["w_out"], cfg["tm"], lw["layer"])
    h2f = h2.reshape(g * t, d)
    info_row, tbl = _route(h2f, lw["wr_t"], lw["br_t"], cfg["tme"], SEG)
    x2, x_sorted = _moe(tbl[:, :, 0], h2f, x1, gt_f, info_row, info_row.T, lw["w_eg"], lw["w_eu"], lw["w_ed"],
                        x_sorted, cfg["tme"], lw["layer"])
    tail = proj_s[:, -min(tseq, BAND_PAST):]
    k_new = tail[:, :, 9 * BRANCH_WIDTH:10 * BRANCH_WIDTH]
    v_new = tail[:, :, 10 * BRANCH_WIDTH:11 * BRANCH_WIDTH]
    return x2, ret_new, gla_new, k_new, v_new, x_sorted


def kernel(x_prompt, x_sample, state_ret, state_gla, cache_att_k, cache_att_v, c_prompt, c_sample,
           w_ada, b_ada, g_mix, w_in, gla_a2, gla_a_bias, g_ret_gn, g_gla_gn, g_q_att, g_k_att, rel_bias,
           w_branch, w_gate, b_gate, w_out, g_ffn, w_router_group, b_router_group, w_router_exp,
           b_router_exp, w_exp_gate, w_exp_up, w_exp_down):
    depth = w_ada.shape[0]
    bp, seq, d = x_prompt.shape
    bs, dseq, _ = x_sample.shape
    n_s = bs * dseq

    pad = (-(bp + bs)) % 8
    c_all = jnp.concatenate([c_prompt, c_sample, jnp.zeros((pad, d), F32)], axis=0)
    mod = _ada_mod(c_all, w_ada, b_ada)

    ga0 = 8 * BRANCH_WIDTH
    w_in_b = jnp.concatenate([w_in[:, :, :ga0], w_in[:, :, ga0 + GLA_RANK:]], axis=2).astype(BF16)
    w_ga = jnp.pad(w_in[:, :, ga0:ga0 + GLA_RANK], ((0, 0), (0, 0), (0, V7X_LANES - GLA_RANK))).astype(BF16)
    a2 = jnp.pad(gla_a2, ((0, 0), (0, V7X_LANES - GLA_RANK), (0, 0))).astype(BF16)
    wr_t = jnp.zeros((depth, 32, d), F32)
    wr_t = wr_t.at[:, 0:N_GROUPS].set(jnp.swapaxes(w_router_group, 1, 2))
    wr_t = wr_t.at[:, 8:8 + N_EXPERTS].set(jnp.swapaxes(w_router_exp, 1, 2)).astype(BF16)
    br_t = jnp.zeros((depth, 32, 1), F32)
    br_t = br_t.at[:, 0:N_GROUPS, 0].set(b_router_group).at[:, 8:8 + N_EXPERTS, 0].set(b_router_exp)
    w_gate_b, w_branch_b, w_out_b = w_gate.astype(BF16), w_branch.astype(BF16), w_out.astype(BF16)
    w_eg, w_eu, w_ed = w_exp_gate.astype(BF16), w_exp_up.astype(BF16), w_exp_down.astype(BF16)

    cos_p, sin_p = _rope_tables(jnp.arange(seq, dtype=F32))
    cos_s, sin_s = _rope_tables(PAST_LEN + jnp.arange(dseq, dtype=F32))
    cos_s, sin_s = jnp.tile(cos_s, (bs, 1)), jnp.tile(sin_s, (bs, 1))

    cfg_p = dict(b=bp, tseq=seq, tm=512, chunk=4 * CHUNK, tc=512, tme=512)
    cfg_s = dict(b=bs, tseq=dseq, tm=n_s, chunk=min(dseq, CHUNK), tc=dseq, tme=n_s)
    zero_state = jnp.zeros((bp, RET_HEADS, HEAD_DK, HEAD_DK), F32)
    sorted_p = jnp.zeros((_sorted_rows(bp * seq, cfg_p["tme"]), d), BF16)
    sorted_s = jnp.zeros((_sorted_rows(n_s, cfg_s["tme"]), d), BF16)

    cache_kt = jnp.transpose(cache_att_k, (0, 1, 3, 4, 2))
    cache_vt = jnp.transpose(cache_att_v, (0, 1, 3, 4, 2))

    xp = x_prompt
    xs = x_sample.reshape(1, n_s, d)
    outs = [[] for _ in range(8)]
    for l in range(depth):
        lw = dict(
            layer=l, g_mix=g_mix[l][None], g_ffn=g_ffn[l][None], w_in=w_in_b, w_ga=w_ga[l], a2=a2[l],
            a_bias=gla_a_bias[l][None], gq=jnp.tile(g_q_att[l], ATT_HEADS)[None],
            gk=jnp.tile(g_k_att[l], ATT_HEADS)[None], g_ret_gn=g_ret_gn[l][None], g_gla_gn=g_gla_gn[l][None],
            rel_bias=rel_bias[l], w_gate=w_gate_b, b_gate=b_gate[l][None], w_branch=w_branch_b,
            w_out=w_out_b, wr_t=wr_t[l], br_t=br_t[l], w_eg=w_eg, w_eu=w_eu, w_ed=w_ed)
        mods_p = tuple(m[:, None, :] for m in jnp.split(mod[l, :bp], 6, axis=-1))
        mods_s = tuple(jnp.repeat(m, dseq, axis=0)[None] for m in jnp.split(mod[l, bp:bp + bs], 6, axis=-1))
        xp, rp, gp, kp, vp, sorted_p = _layer(xp, mods_p, lw, cfg_p, zero_state, zero_state, None, None,
                                              cos_p, sin_p, sorted_p)
        xs, rs, gs, ks, vs, sorted_s = _layer(xs, mods_s, lw, cfg_s, state_ret[l], state_gla[l], cache_kt, cache_vt,
                                              cos_s, sin_s, sorted_s)
        heads = lambda a: a.astype(F32).reshape(a.shape[0], a.shape[1], ATT_HEADS, ATT_HD)
        for lst, val in zip(outs, (rp, gp, heads(kp), heads(vp), rs, gs, heads(ks), heads(vs))):
            lst.append(val)
    return (xp, xs.reshape(bs, dseq, d)) + tuple(jnp.stack(o) for o in outs)
```

```python
import functools

import numpy as np
import jax
import jax.numpy as jnp
from jax import lax
from jax.experimental import pallas as pl
from jax.experimental.pallas import tpu as pltpu

F32 = jnp.float32
BF16 = jnp.bfloat16

D_MODEL = 1024
CHUNK = 64
BRANCH_WIDTH = 512
N_BRANCH = 3
RET_HEADS = 4
HEAD_DK = 128
GLA_RANK = 16
GLA_TAU = 16.0
ATT_HEADS = 8
ATT_HD = 64
BAND_PAST = 512
MAX_REL = 128
N_GROUPS = 4
EXPERTS_PER_GROUP = 4
N_EXPERTS = 16
D_EXPERT = 512
ROPE_BASE = 10000.0
EPS = 1e-6
GN_EPS = 1e-5
PAST_LEN = 1024
N_PROJ_BLOCKS = 11
PROJ_COLS = N_PROJ_BLOCKS * BRANCH_WIDTH

V7X_LANES = 128
V7X_VMEM_BYTES = 64 * 1024 * 1024
VMEM_LIMIT = 56 * 1024 * 1024
NEG_BIG = -1e30


def _cparams(sem, vmem=None):
    return pltpu.CompilerParams(dimension_semantics=sem, vmem_limit_bytes=vmem)


def _const_spec(shape):
    nd = len(shape)
    return pl.BlockSpec(shape, lambda *_: (0,) * nd, pipeline_mode=pl.Buffered(1))


def _layer_spec(shape, layer):
    nd = len(shape)
    return pl.BlockSpec((None,) + tuple(shape), lambda *_: (layer,) + (0,) * nd, pipeline_mode=pl.Buffered(1))


def _dot(a, b):
    return jnp.dot(a, b, preferred_element_type=F32)


def _dot_nt(a, b):
    return lax.dot_general(a, b, (((1,), (1,)), ((), ())), preferred_element_type=F32)


def _dot_tn(a, b):
    return lax.dot_general(a, b, (((0,), (0,)), ((), ())), preferred_element_type=F32)


def _silu(x):
    return x * jax.nn.sigmoid(x)


def _norm_mod(x, g, sc, sh):
    ms = jnp.mean(x * x, axis=-1, keepdims=True)
    return x * lax.rsqrt(ms + EPS) * g * (1.0 + sc) + sh


def _ada_kernel(c_ref, w_ref, b_ref, o_ref):
    s = _silu(c_ref[...])
    o_ref[0] = _dot(s.astype(BF16), w_ref[0].astype(BF16)) + b_ref[0]


def _ada_mod(c_all, w_ada, b_ada):
    depth, d, n = w_ada.shape
    rows = c_all.shape[0]
    bn = 1536
    return pl.pallas_call(
        _ada_kernel,
        out_shape=jax.ShapeDtypeStruct((depth, rows, n), F32),
        grid=(depth, n // bn),
        in_specs=[
            pl.BlockSpec((rows, d), lambda l, j: (0, 0)),
            pl.BlockSpec((1, d, bn), lambda l, j: (l, 0, j)),
            pl.BlockSpec((1, 1, bn), lambda l, j: (l, 0, j)),
        ],
        out_specs=pl.BlockSpec((1, rows, bn), lambda l, j: (l, 0, j)),
        compiler_params=_cparams(("arbitrary", "arbitrary"), VMEM_LIMIT),
        name="ada_mod",
    )(c_all, w_ada, b_ada.reshape(depth, 1, n))


def _rope_heads(a, cosf, sinf):
    outs = []
    for h in range(RET_HEADS):
        ah = a[:, h * HEAD_DK:(h + 1) * HEAD_DK]
        outs.append(ah * cosf + pltpu.roll(ah, HEAD_DK // 2, 1) * sinf)
    return jnp.concatenate(outs, axis=1)


def _rms_heads64(a, gain):
    low = lax.broadcasted_iota(jnp.int32, (1, V7X_LANES), 1) < ATT_HD
    outs = []
    for c in range(a.shape[1] // V7X_LANES):
        ac = a[:, c * V7X_LANES:(c + 1) * V7X_LANES]
        sq = ac * ac
        lo = jnp.sum(jnp.where(low, sq, 0.0), axis=-1, keepdims=True)
        hi = jnp.sum(jnp.where(low, 0.0, sq), axis=-1, keepdims=True)
        ms = jnp.where(low, lo, hi) * (1.0 / ATT_HD)
        outs.append(ac * lax.rsqrt(ms + EPS))
    return jnp.concatenate(outs, axis=1) * gain


def _in_proj_kernel(x_ref, sc_ref, sh_ref, g_ref, w_ref, wga_ref, a2_ref, ab_ref, cos_ref, sin_ref,
                    gq_ref, gk_ref, proj_ref, la_ref):
    x = x_ref[0]
    hb = _norm_mod(x, g_ref[...], sc_ref[0], sh_ref[0]).astype(BF16)
    cosf = cos_ref[...]
    sinf = sin_ref[...]
    ga = _dot(hb, wga_ref[...])
    for j in range(N_PROJ_BLOCKS):
        if j == N_PROJ_BLOCKS // 2:
            z = _dot(ga.astype(BF16), a2_ref[...]) + ab_ref[...]
            la_ref[0] = jax.nn.log_sigmoid(z) * (1.0 / GLA_TAU)
        cols = slice(j * BRANCH_WIDTH, (j + 1) * BRANCH_WIDTH)
        acc = _dot(hb, w_ref[:, cols])
        if j == 0:
            acc = _rope_heads(acc, cosf, sinf)
        elif j == 1:
            acc = _rope_heads(acc, cosf, sinf) * (HEAD_DK ** -0.5)
        elif j == 4:
            acc = acc * (HEAD_DK ** -0.5)
        elif j == 8:
            acc = _rms_heads64(acc, gq_ref[...]) * (ATT_HD ** -0.5)
        elif j == 9:
            acc = _rms_heads64(acc, gk_ref[...])
        proj_ref[0, :, cols] = acc.astype(BF16)


def _in_proj(x, sc, sh, g_mix, w_in, w_ga, a2, a_bias, cosf, sinf, gq, gk, tm, layer):
    g, t, d = x.shape
    rm = sc.shape[1]
    mod_spec = pl.BlockSpec((1, rm, d), (lambda b, i: (b, 0, 0)) if rm == 1 else (lambda b, i: (b, i, 0)))
    return pl.pallas_call(
        _in_proj_kernel,
        out_shape=(jax.ShapeDtypeStruct((g, t, PROJ_COLS), BF16),
                   jax.ShapeDtypeStruct((g, t, BRANCH_WIDTH), F32)),
        grid=(g, t // tm),
        in_specs=[
            pl.BlockSpec((1, tm, d), lambda b, i: (b, i, 0)),
            mod_spec, mod_spec,
            _const_spec((1, d)),
            _layer_spec((d, PROJ_COLS), layer),
            _const_spec((d, V7X_LANES)),
            _const_spec((V7X_LANES, BRANCH_WIDTH)),
            _const_spec((1, BRANCH_WIDTH)),
            pl.BlockSpec((tm, HEAD_DK), lambda b, i: (i, 0)),
            pl.BlockSpec((tm, HEAD_DK), lambda b, i: (i, 0)),
            _const_spec((1, BRANCH_WIDTH)),
            _const_spec((1, BRANCH_WIDTH)),
        ],
        out_specs=(pl.BlockSpec((1, tm, PROJ_COLS), lambda b, i: (b, i, 0)),
                   pl.BlockSpec((1, tm, BRANCH_WIDTH), lambda b, i: (b, i, 0))),
        compiler_params=_cparams(("arbitrary", "arbitrary"), VMEM_LIMIT),
        name="in_proj",
    )(x, sc, sh, g_mix, w_in, w_ga, a2, a_bias, cosf, sinf, gq, gk)


def _retention_kernel(q_ref, k_ref, v_ref, g_ref, s0_ref, dm_ref, qd_ref, kd_ref, bd_ref, gn_ref,
                      o_ref, so_ref, st_ref, *, chunk, n_chunks):
    n_seq = q_ref.shape[0]

    @pl.when(pl.program_id(1) == 0)
    def _():
        st_ref[...] = s0_ref[...]

    def body(c, carry):
        rows = pl.ds(pl.multiple_of(c * chunk, chunk), chunk)
        for b in range(n_seq):
            for h in range(RET_HEADS):
                cols = slice(h * HEAD_DK, (h + 1) * HEAD_DK)
                q = q_ref[b, rows, cols]
                k = k_ref[b, rows, cols]
                v = v_ref[b, rows, cols]
                scores = _dot_nt(q, k) * dm_ref[h]
                o = _dot(scores.astype(BF16), v)
                o = o + _dot((q.astype(F32) * qd_ref[h]).astype(BF16), st_ref[b, h].astype(BF16))
                st_ref[b, h] = (st_ref[b, h] * bd_ref[h]
                                + _dot_tn((k.astype(F32) * kd_ref[h]).astype(BF16), v))
                mu = jnp.mean(o, axis=-1, keepdims=True)
                oc = o - mu
                var = jnp.mean(oc * oc, axis=-1, keepdims=True)
                on = oc * lax.rsqrt(var + GN_EPS) * gn_ref[:, cols]
                o_ref[b, rows, cols] = (_silu(g_ref[b, rows, cols].astype(F32)) * on).astype(BF16)
        return carry

    lax.fori_loop(0, n_chunks, body, 0)

    @pl.when(pl.program_id(1) == pl.num_programs(1) - 1)
    def _():
        so_ref[...] = st_ref[...]


def _retention_tables(chunk):
    log_gamma = jnp.log(1.0 - jnp.exp2(-5.0 - jnp.arange(RET_HEADS, dtype=F32)))
    idx = jnp.arange(chunk, dtype=F32)
    diff = idx[:, None] - idx[None, :]
    dmask = jnp.where(diff >= 0, jnp.exp(log_gamma[:, None, None] * jnp.maximum(diff, 0.0)), 0.0)
    qd = jnp.exp(log_gamma[:, None] * (idx + 1.0))[:, :, None]
    kd = jnp.exp(log_gamma[:, None] * (chunk - 1.0 - idx))[:, :, None]
    bd = jnp.exp(log_gamma * chunk)[:, None, None]
    bc = lambda a, r: jnp.broadcast_to(a, (RET_HEADS, r, HEAD_DK)).astype(F32)
    return dmask.astype(F32), bc(qd, chunk), bc(kd, chunk), bc(bd, 1)


def _seqs_per_step(b):
    return 4 if b % 4 == 0 and b > 4 else (2 if b % 2 == 0 else 1)


def _retention(proj, state0, gn, chunk, tc):
    b, t, _ = proj.shape
    nb = _seqs_per_step(b)
    dm, qd, kd, bd = _retention_tables(chunk)
    col = lambda j: pl.BlockSpec((nb, tc, BRANCH_WIDTH), lambda i, s, j=j: (i, s, j))
    st_spec = pl.BlockSpec((nb, RET_HEADS, HEAD_DK, HEAD_DK), lambda i, s: (i, 0, 0, 0))
    return pl.pallas_call(
        functools.partial(_retention_kernel, chunk=chunk, n_chunks=tc // chunk),
        out_shape=(jax.ShapeDtypeStruct((b, t, BRANCH_WIDTH), BF16),
                   jax.ShapeDtypeStruct((b, RET_HEADS, HEAD_DK, HEAD_DK), F32)),
        grid=(b // nb, t // tc),
        in_specs=[col(0), col(1), col(2), col(3), st_spec,
                  _const_spec(dm.shape), _const_spec(qd.shape), _const_spec(kd.shape), _const_spec(bd.shape),
                  _const_spec((1, BRANCH_WIDTH))],
        out_specs=(pl.BlockSpec((nb, tc, BRANCH_WIDTH), lambda i, s: (i, s, 0)), st_spec),
        scratch_shapes=[pltpu.VMEM((nb, RET_HEADS, HEAD_DK, HEAD_DK), F32)],
        compiler_params=_cparams(("arbitrary", "arbitrary")),
        name="retention",
    )(proj, proj, proj, proj, state0, dm, qd, kd, bd, gn)


GLA_SAFE_LOG_DECAY = -60.0


def _gla_intra_direct(q, bh, rows_ref, chunk):
    trow = lax.broadcasted_iota(jnp.int32, (chunk, 1), 0)

    def step(s, acc):
        bs = rows_ref[0, pl.ds(s, 1), :]
        ks = rows_ref[1, pl.ds(s, 1), :]
        vs = rows_ref[2, pl.ds(s, 1), :]
        e = jnp.exp(jnp.minimum(bh - bs, 0.0))
        col = jnp.sum(q * ks * e, axis=-1, keepdims=True)
        return acc + jnp.where(trow >= s, col, 0.0) * vs

    return lax.fori_loop(0, chunk, step, jnp.zeros((chunk, HEAD_DK), F32))


def _gla_kernel(q_ref, k_ref, v_ref, r_ref, la_ref, s0_ref, tri_ref, gn_ref, o_ref, so_ref, st_ref, rows_ref,
                *, chunk, n_chunks):
    n_seq = q_ref.shape[0]

    @pl.when(pl.program_id(1) == 0)
    def _():
        for b in range(n_seq):
            for h in range(RET_HEADS):
                st_ref[b, h] = s0_ref[b, h].T

    tri = tri_ref[...]
    causal = (lax.broadcasted_iota(jnp.int32, (chunk, chunk), 0)
              >= lax.broadcasted_iota(jnp.int32, (chunk, chunk), 1))

    def chunk_body(factored):
        def body(c, carry):
            rows = pl.ds(pl.multiple_of(c * chunk, chunk), chunk)
            for b in range(n_seq):
                la = la_ref[b, rows, :]
                la_hi = la.astype(BF16)
                la_lo = (la - la_hi.astype(F32)).astype(BF16)
                b_all = _dot(tri, la_hi) + _dot(tri, la_lo)
                for h in range(RET_HEADS):
                    cols = slice(h * HEAD_DK, (h + 1) * HEAD_DK)
                    q = q_ref[b, rows, cols].astype(F32)
                    k = k_ref[b, rows, cols].astype(F32)
                    v = v_ref[b, rows, cols]
                    bh = b_all[:, cols]
                    bl = bh[chunk - 1:chunk, :]
                    qh = (q * jnp.exp(bh)).astype(BF16)
                    if factored:
                        kh = (k * jnp.exp(-bh)).astype(BF16)
                        scores = jnp.where(causal, _dot_nt(qh, kh), 0.0)
                        o = _dot(scores.astype(BF16), v)
                    else:
                        rows_ref[0] = bh
                        rows_ref[1] = k
                        rows_ref[2] = v.astype(F32)
                        o = _gla_intra_direct(q, bh, rows_ref, chunk)
                    o = o + _dot_nt(qh, st_ref[b, h].astype(BF16))
                    kd = (k * jnp.exp(bl - bh)).astype(BF16)
                    st_ref[b, h] = st_ref[b, h] * jnp.exp(bl) + _dot_tn(v, kd)
                    ms = jnp.mean(o * o, axis=-1, keepdims=True)
                    on = o * lax.rsqrt(ms + EPS) * gn_ref[:, cols]
                    o_ref[b, rows, cols] = (_silu(r_ref[b, rows, cols].astype(F32)) * on).astype(BF16)
            return carry
        return body

    lowest = None
    for b in range(n_seq):
        for c in range(n_chunks):
            tot = jnp.sum(la_ref[b, c * chunk:(c + 1) * chunk, :], axis=0, keepdims=True)
            lowest = tot if lowest is None else jnp.minimum(lowest, tot)
    safe = jnp.min(lowest) > GLA_SAFE_LOG_DECAY

    @pl.when(safe)
    def _():
        lax.fori_loop(0, n_chunks, chunk_body(True), 0)

    @pl.when(jnp.logical_not(safe))
    def _():
        lax.fori_loop(0, n_chunks, chunk_body(False), 0)

    @pl.when(pl.program_id(1) == pl.num_programs(1) - 1)
    def _():
        for b in range(n_seq):
            for h in range(RET_HEADS):
                so_ref[b, h] = st_ref[b, h].T


def _gla(proj, log_a, state0, gn, chunk, tc):
    b, t, _ = proj.shape
    nb = _seqs_per_step(b)
    tri = jnp.tril(jnp.ones((chunk, chunk), F32)).astype(BF16)
    col = lambda j: pl.BlockSpec((nb, tc, BRANCH_WIDTH), lambda i, s, j=j: (i, s, j))
    st_spec = pl.BlockSpec((nb, RET_HEADS, HEAD_DK, HEAD_DK), lambda i, s: (i, 0, 0, 0))
    return pl.pallas_call(
        functools.partial(_gla_kernel, chunk=chunk, n_chunks=tc // chunk),
        out_shape=(jax.ShapeDtypeStruct((b, t, BRANCH_WIDTH), BF16),
                   jax.ShapeDtypeStruct((b, RET_HEADS, HEAD_DK, HEAD_DK), F32)),
        grid=(b // nb, t // tc),
        in_specs=[col(4), col(5), col(6), col(7),
                  pl.BlockSpec((nb, tc, BRANCH_WIDTH), lambda i, s: (i, s, 0)),
                  st_spec, _const_spec((chunk, chunk)), _const_spec((1, BRANCH_WIDTH))],
        out_specs=(pl.BlockSpec((nb, tc, BRANCH_WIDTH), lambda i, s: (i, s, 0)), st_spec),
        scratch_shapes=[pltpu.VMEM((nb, RET_HEADS, HEAD_DK, HEAD_DK), F32),
                        pltpu.VMEM((3, chunk, HEAD_DK), F32)],
        compiler_params=_cparams(("arbitrary", "arbitrary")),
        name="gla",
    )(proj, proj, proj, proj, log_a, state0, tri, gn)


def _attn_stage_scratch(rb, win):
    pairs = ATT_HEADS // 2
    return [pltpu.VMEM((2, pairs, 2 * rb, win), F32), pltpu.VMEM((2, pairs, 2 * rb, win), BF16),
            pltpu.VMEM((2, pairs, 2 * rb, V7X_LANES), F32)]


def _band_attn_kernel(q_ref, kp_ref, kc_ref, vp_ref, vc_ref, bias_ref, o_ref, kw_ref, vw_ref,
                      s_ref, e_ref, den_ref,
                      *, n_sub, rb, win, prev_rows, mask_start):
    cur_rows = kc_ref.shape[1]
    kw_ref[0:prev_rows, :] = kp_ref[0].astype(BF16)
    kw_ref[prev_rows:prev_rows + cur_rows, :] = kc_ref[0]
    vw_ref[0:prev_rows, :] = vp_ref[0].astype(BF16)
    vw_ref[prev_rows:prev_rows + cur_rows, :] = vc_ref[0]
    low = lax.broadcasted_iota(jnp.int32, (1, V7X_LANES), 1) < ATT_HD
    q0 = pl.program_id(1) * cur_rows
    pairs = ATT_HEADS // 2

    def scores(i):
        for p in range(pairs):
            cols = slice(p * V7X_LANES, (p + 1) * V7X_LANES)
            q2 = q_ref[0, i * rb:(i + 1) * rb, cols]
            zero = jnp.zeros_like(q2)
            qst = jnp.concatenate([jnp.where(low, q2, zero), jnp.where(low, zero, q2)], axis=0)
            s_ref[i % 2, p] = _dot_nt(qst, kw_ref[i * rb:i * rb + win, cols])

    def softmax(i, masked):
        if masked:
            valid = lax.broadcasted_iota(jnp.int32, (1, win), 1) >= prev_rows - q0 - i * rb
        for p in range(pairs):
            s = s_ref[i % 2, p] + bias_ref[p]
            if masked:
                s = jnp.where(valid, s, NEG_BIG)
            m = jnp.max(s, axis=-1, keepdims=True)
            e = jnp.exp(s - m)
            den_ref[i % 2, p] = jnp.broadcast_to(jnp.sum(e, axis=-1, keepdims=True), den_ref.shape[2:])
            e_ref[i % 2, p] = e.astype(BF16)

    def values(i):
        for p in range(pairs):
            cols = slice(p * V7X_LANES, (p + 1) * V7X_LANES)
            o2 = _dot(e_ref[i % 2, p], vw_ref[i * rb:i * rb + win, cols]) / den_ref[i % 2, p]
            o_ref[0, i * rb:(i + 1) * rb, cols] = jnp.where(low, o2[:rb], o2[rb:]).astype(BF16)

    def pipeline(masked):
        for step in range(n_sub + 2):
            if step < n_sub:
                scores(step)
            if 0 <= step - 1 < n_sub:
                softmax(step - 1, masked)
            if 0 <= step - 2 < n_sub:
                values(step - 2)

    if mask_start:
        pl.when(q0 < prev_rows)(functools.partial(pipeline, True))
        pl.when(q0 >= prev_rows)(functools.partial(pipeline, False))
    else:
        pipeline(False)


def _rel_bias_table(rel_bias, rb, win, chunk):
    period = win + rb
    u = np.arange(period)
    u = np.where(u < win, u, u - period)
    vec = rel_bias.astype(F32)[:, np.clip(BAND_PAST - u, -MAX_REL, MAX_REL) + MAX_REL]
    bias = jnp.tile(vec, (1, rb))[:, :rb * (period - 1)].reshape(-1, rb, period - 1)[:, :, :win]
    r = np.arange(rb)[:, None]
    j = np.arange(win)[None, :]
    c0 = (r // chunk) * chunk
    allowed = (j >= c0) & (j < c0 + BAND_PAST + chunk)
    bias = jnp.where(jnp.asarray(allowed)[None], bias, NEG_BIG)
    return bias.reshape(ATT_HEADS // 2, 2 * rb, win)


def _band_attn_prompt(proj, rel_bias):
    b, t, _ = proj.shape
    qb, rb = 512, 128
    win = BAND_PAST + rb
    bias = _rel_bias_table(rel_bias, rb, win, CHUNK)
    cur = lambda j: pl.BlockSpec((1, qb, BRANCH_WIDTH), lambda i, s, j=j: (i, s, j))
    prev = lambda j: pl.BlockSpec((1, qb, BRANCH_WIDTH), lambda i, s, j=j: (i, jnp.maximum(s - 1, 0), j))
    return pl.pallas_call(
        functools.partial(_band_attn_kernel, n_sub=qb // rb, rb=rb, win=win, prev_rows=BAND_PAST,
                          mask_start=True),
        out_shape=jax.ShapeDtypeStruct((b, t, BRANCH_WIDTH), BF16),
        grid=(b, t // qb),
        in_specs=[cur(8), prev(9), cur(9), prev(10), cur(10), _const_spec(bias.shape)],
        out_specs=pl.BlockSpec((1, qb, BRANCH_WIDTH), lambda i, s: (i, s, 0)),
        scratch_shapes=[pltpu.VMEM((BAND_PAST + qb, BRANCH_WIDTH), BF16),
                        pltpu.VMEM((BAND_PAST + qb, BRANCH_WIDTH), BF16)] + _attn_stage_scratch(rb, win),
        compiler_params=_cparams(("arbitrary", "arbitrary")),
        name="band_attn",
    )(proj, proj, proj, proj, proj, bias)


def _band_attn_sample_kernel(q_ref, kn_ref, vn_ref, kc_ref, vc_ref, bias_ref, o_ref):
    t = q_ref.shape[1]
    past = kc_ref.shape[-1]
    low = lax.broadcasted_iota(jnp.int32, (1, V7X_LANES), 1) < ATT_HD
    for p in range(ATT_HEADS // 2):
        cols = slice(p * V7X_LANES, (p + 1) * V7X_LANES)
        q2 = q_ref[0, :, cols]
        zero = jnp.zeros_like(q2)
        qst = jnp.concatenate([jnp.where(low, q2, zero), jnp.where(low, zero, q2)], axis=0)
        kt = kc_ref[0, 2 * p:2 * p + 2].reshape(V7X_LANES, past).astype(BF16)
        vt = vc_ref[0, 2 * p:2 * p + 2].reshape(V7X_LANES, past).astype(BF16)
        s = jnp.concatenate([_dot(qst, kt), _dot_nt(qst, kn_ref[0, :, cols])], axis=1) + bias_ref[p]
        m = jnp.max(s, axis=-1, keepdims=True)
        e = jnp.exp(s - m)
        den = jnp.sum(e, axis=-1, keepdims=True)
        eb = e.astype(BF16)
        o2 = (_dot_nt(eb[:, :past], vt) + _dot(eb[:, past:], vn_ref[0, :, cols])) / den
        o_ref[0, :, cols] = jnp.where(low, o2[:t], o2[t:]).astype(BF16)


def _band_attn_sample(proj, cache_kt, cache_vt, rel_bias, layer):
    b, t, _ = proj.shape
    win = BAND_PAST + t
    bias = _rel_bias_table(rel_bias, t, win, t)
    cur = lambda j: pl.BlockSpec((1, t, BRANCH_WIDTH), lambda i, j=j: (i, 0, j))
    cache = pl.BlockSpec((None, 1, ATT_HEADS, ATT_HD, BAND_PAST), lambda i: (layer, i, 0, 0, 0))
    return pl.pallas_call(
        _band_attn_sample_kernel,
        out_shape=jax.ShapeDtypeStruct((b, t, BRANCH_WIDTH), BF16),
        grid=(b,),
        in_specs=[cur(8), cur(9), cur(10), cache, cache, _const_spec(bias.shape)],
        out_specs=pl.BlockSpec((1, t, BRANCH_WIDTH), lambda i: (i, 0, 0)),
        compiler_params=_cparams(("arbitrary",)),
        name="band_attn_sample",
    )(proj, proj, proj, cache_kt, cache_vt, bias)


def _merge_kernel(x_ref, or_ref, og_ref, oa_ref, scm_ref, shm_ref, gtm_ref, scf_ref, shf_ref,
                  gmix_ref, gffn_ref, wg_ref, bg_ref, wb_ref, wo_ref, xo_ref, h2_ref):
    x = x_ref[0]
    d = x.shape[1]
    hb = _norm_mod(x, gmix_ref[...], scm_ref[0], shm_ref[0]).astype(BF16)
    merged = None
    for n, o_ref in enumerate((or_ref, og_ref, oa_ref)):
        gate = jax.nn.sigmoid(_dot(hb, wg_ref[:, n * d:(n + 1) * d]) + bg_ref[:, n * d:(n + 1) * d])
        y = gate * _dot(o_ref[0], wb_ref[n])
        merged = y if merged is None else merged + y
    mix = _dot(merged.astype(BF16), wo_ref[...])
    xn = x + gtm_ref[0] * mix
    xo_ref[0] = xn
    h2_ref[0] = _norm_mod(xn, gffn_ref[...], scf_ref[0], shf_ref[0]).astype(BF16)


def _merge(x, o_r, o_g, o_a, mods, g_mix, g_ffn, w_gate, b_gate, w_branch, w_out, tm, layer):
    g, t, d = x.shape
    rm = mods[0].shape[1]
    mod_spec = pl.BlockSpec((1, rm, d), (lambda b, i: (b, 0, 0)) if rm == 1 else (lambda b, i: (b, i, 0)))
    tok = lambda w: pl.BlockSpec((1, tm, w), lambda b, i: (b, i, 0))
    return pl.pallas_call(
        _merge_kernel,
        out_shape=(jax.ShapeDtypeStruct((g, t, d), F32), jax.ShapeDtypeStruct((g, t, d), BF16)),
        grid=(g, t // tm),
        in_specs=[tok(d), tok(BRANCH_WIDTH), tok(BRANCH_WIDTH), tok(BRANCH_WIDTH)] + [mod_spec] * 5 + [
            _const_spec((1, d)), _const_spec((1, d)),
            _layer_spec((d, N_BRANCH * d), layer), _const_spec((1, N_BRANCH * d)),
            _layer_spec((N_BRANCH, BRANCH_WIDTH, d), layer), _layer_spec((d, d), layer)],
        out_specs=(tok(d), tok(d)),
        compiler_params=_cparams(("arbitrary", "arbitrary"), VMEM_LIMIT),
        name="merge",
    )(x, o_r, o_g, o_a, *mods, g_mix, g_ffn, w_gate, b_gate, w_branch, w_out)


def _first_argmax(vals, n):
    row = lax.broadcasted_iota(jnp.int32, vals.shape, 0).astype(F32)
    m = jnp.max(vals, axis=0, keepdims=True)
    idx = jnp.min(jnp.where(vals == m, row, float(n)), axis=0, keepdims=True)
    return m, idx


def _route_kernel(h_ref, wr_ref, br_ref, up_ref, l16_ref, info_ref, tbl_ref, *, rows_per_block):
    tm = h_ref.shape[0]
    lg = _dot_nt(wr_ref[...], h_ref[...]) + br_ref[...]
    g = lg[0:N_GROUPS]
    gmax, grp = _first_argmax(g, N_GROUPS)
    p_group = 1.0 / jnp.sum(jnp.exp(g - gmax), axis=0, keepdims=True)
    esel = jnp.zeros((EXPERTS_PER_GROUP, tm), F32)
    for gi in range(N_GROUPS):
        blk = lg[8 + gi * EXPERTS_PER_GROUP:8 + (gi + 1) * EXPERTS_PER_GROUP]
        esel = esel + jnp.where(grp == float(gi), blk, 0.0)
    v1, i1 = _first_argmax(esel, EXPERTS_PER_GROUP)
    row4 = lax.broadcasted_iota(jnp.int32, esel.shape, 0).astype(F32)
    v2, i2 = _first_argmax(jnp.where(row4 == i1, -jnp.inf, esel), EXPERTS_PER_GROUP)
    e21 = jnp.exp(v2 - v1)
    w1 = p_group / (1.0 + e21)
    w2 = p_group * e21 / (1.0 + e21)
    e1 = grp * float(EXPERTS_PER_GROUP) + i1
    e2 = grp * float(EXPERTS_PER_GROUP) + i2
    row16 = lax.broadcasted_iota(jnp.int32, (N_EXPERTS, tm), 0).astype(F32)
    hit1 = row16 == e1
    hit2 = row16 == e2
    onehot = jnp.where(hit1 | hit2, 1.0, 0.0)
    prefix = _dot(onehot.astype(BF16), up_ref[...])
    cnt = jnp.sum(onehot, axis=1, keepdims=True)
    nblk = jnp.floor((cnt + float(rows_per_block - 1)) * (1.0 / rows_per_block))
    nblk_b = jnp.broadcast_to(nblk, (N_EXPERTS, V7X_LANES))
    offb = _dot(l16_ref[...], nblk_b.astype(BF16))[:, 0:1]
    base = offb * float(rows_per_block) + prefix
    info_ref[0:1, :] = jnp.sum(jnp.where(hit1, base, 0.0), axis=0, keepdims=True)
    info_ref[1:2, :] = jnp.sum(jnp.where(hit2, base, 0.0), axis=0, keepdims=True)
    info_ref[2:3, :] = w1
    info_ref[3:4, :] = w2
    info_ref[4:8, :] = jnp.zeros((4, tm), F32)
    tbl_ref[0] = nblk_b.astype(jnp.int32)


def _route(h2, wr_t, br_t, tm, rows_per_block):
    n, d = h2.shape
    nt = n // tm
    upper = jnp.triu(jnp.ones((tm, tm), F32), 1).astype(BF16)
    l16 = jnp.tril(jnp.ones((N_EXPERTS, N_EXPERTS), F32), -1).astype(BF16)
    return pl.pallas_call(
        functools.partial(_route_kernel, rows_per_block=rows_per_block),
        out_shape=(jax.ShapeDtypeStruct((8, n), F32),
                   jax.ShapeDtypeStruct((nt, N_EXPERTS, V7X_LANES), jnp.int32)),
        grid=(nt,),
        in_specs=[pl.BlockSpec((tm, d), lambda i: (i, 0)),
                  _const_spec((32, d)), _const_spec((32, 1)),
                  _const_spec((tm, tm)), _const_spec((N_EXPERTS, N_EXPERTS))],
        out_specs=(pl.BlockSpec((8, tm), lambda i: (0, i)),
                   pl.BlockSpec((1, N_EXPERTS, V7X_LANES), lambda i: (i, 0, 0))),
        compiler_params=_cparams(("arbitrary",)),
        name="route",
    )(h2, wr_t, br_t, upper, l16)


SEG = 16
GATHER_ROWS = 256
MAX_EXPERT_ROWS = 512


def _expert_rows(n_tokens):
    mean_rows = 2 * n_tokens // N_EXPERTS
    return int(min(MAX_EXPERT_ROWS, max(4 * SEG, 1 << (mean_rows // 4).bit_length())))


def _segment_copies(cnt_ref, loff_ref, goff_ref, tile, local_ref, global_ref, sem, to_global):
    for e in range(N_EXPERTS):
        cnt = cnt_ref[tile * N_EXPERTS + e]
        rows = pl.multiple_of(cnt * SEG, SEG)
        loc = local_ref.at[pl.ds(pl.multiple_of(loff_ref[tile * N_EXPERTS + e] * SEG, SEG), rows)]
        glo = global_ref.at[pl.ds(pl.multiple_of(goff_ref[tile * N_EXPERTS + e] * SEG, SEG), rows)]
        cp = pltpu.make_async_copy(loc, glo, sem) if to_global else pltpu.make_async_copy(glo, loc, sem)
        yield cnt > 0, cp


def _tile_units(cnt_ref, tile):
    total = jnp.int32(0)
    for e in range(N_EXPERTS):
        total = total + cnt_ref[tile * N_EXPERTS + e]
    return total


def _wait_segments(cnt_ref, tile, local_ref, global_ref, sem, to_global):
    rows = pl.multiple_of(_tile_units(cnt_ref, tile) * SEG, SEG)
    loc = local_ref.at[pl.ds(0, rows)]
    glo = global_ref.at[pl.ds(0, rows)]
    cp = pltpu.make_async_copy(loc, glo, sem) if to_global else pltpu.make_async_copy(glo, loc, sem)
    pl.when(rows > 0)(cp.wait)


def _sort_kernel(cnt_ref, loff_ref, goff_ref, h_ref, irow_ref, xin_ref, xout_ref, xs2_ref, sem2):
    del xin_ref
    i = pl.program_id(0)
    last = pl.num_programs(0) - 1
    slot = lax.rem(i, 2)
    xs_ref = xs2_ref.at[slot]
    sem = sem2.at[slot]

    def wait_tile(tile, s):
        _wait_segments(cnt_ref, tile, xs2_ref.at[s], xout_ref, sem2.at[s], True)

    @pl.when(i >= 2)
    def _():
        wait_tile(i - 2, slot)

    gr = GATHER_ROWS
    n_gather = (_tile_units(cnt_ref, i) * SEG + gr - 1) // gr
    pos1 = irow_ref[0:1, :]
    pos2 = irow_ref[1:2, :]
    h = h_ref[...]

    def gather(gb, carry):
        r0 = pl.multiple_of(gb * gr, gr)
        srow = (lax.broadcasted_iota(jnp.int32, (gr, 1), 0) + r0).astype(F32)
        sel = jnp.where((srow == pos1) | (srow == pos2), 1.0, 0.0).astype(BF16)
        xs_ref[pl.ds(r0, gr), :] = _dot(sel, h).astype(BF16)
        return carry

    lax.fori_loop(0, n_gather, gather, 0)
    for pred, cp in _segment_copies(cnt_ref, loff_ref, goff_ref, i, xs_ref, xout_ref, sem, True):
        pl.when(pred)(cp.start)

    @pl.when(i == last)
    def _():
        @pl.when(i >= 1)
        def _():
            wait_tile(i - 1, 1 - slot)

        wait_tile(i, slot)


def _experts_kernel(be_ref, nv_ref, x_ref, wg_ref, wu_ref, wd_ref, o_ref):
    del be_ref
    used = pl.program_id(0) < nv_ref[0]

    @pl.when(used)
    def _():
        xb = x_ref[...]
        a = _silu(_dot(xb, wg_ref[0].astype(BF16))) * _dot(xb, wu_ref[0].astype(BF16))
        o_ref[...] = _dot(a.astype(BF16), wd_ref[0].astype(BF16)).astype(BF16)

    @pl.when(jnp.logical_not(used))
    def _():
        o_ref[...] = jnp.zeros(o_ref.shape, BF16)


def _combine_kernel(cnt_ref, loff_ref, goff_ref, ds_ref, icol_ref, x_ref, gt_ref, o_ref, dl2_ref, sem2, *, rb):
    i = pl.program_id(0)
    tm = x_ref.shape[1]
    s_loc = dl2_ref.shape[1]
    slot = lax.rem(i, 2)

    def fetch_tile(tile, s):
        for pred, cp in _segment_copies(cnt_ref, loff_ref, goff_ref, tile, dl2_ref.at[s], ds_ref, sem2.at[s], False):
            pl.when(pred)(cp.start)

        def zero_tail(u, carry):
            dl2_ref[s, pl.ds(pl.multiple_of(u * SEG, SEG), SEG), :] = jnp.zeros((SEG, dl2_ref.shape[2]), BF16)
            return carry

        lax.fori_loop(_tile_units(cnt_ref, tile), s_loc // SEG, zero_tail, 0)

    @pl.when(i == 0)
    def _():
        fetch_tile(0, 0)

    @pl.when(i + 1 < pl.num_programs(0))
    def _():
        fetch_tile(i + 1, 1 - slot)

    _wait_segments(cnt_ref, i, dl2_ref.at[slot], ds_ref, sem2.at[slot], False)
    scol = lax.broadcasted_iota(jnp.int32, (1, s_loc), 1).astype(F32)
    for r in range(tm // rb):
        rows = slice(r * rb, (r + 1) * rb)
        cmb = (jnp.where(scol == icol_ref[rows, 0:1], icol_ref[rows, 2:3], 0.0)
               + jnp.where(scol == icol_ref[rows, 1:2], icol_ref[rows, 3:4], 0.0))
        y = _dot(cmb.astype(BF16), dl2_ref[slot])
        gt = gt_ref[0] if gt_ref.shape[1] == 1 else gt_ref[0, rows, :]
        o_ref[0, rows, :] = x_ref[0, rows, :] + gt * y


def _sorted_rows(n_tokens, tm):
    n_tiles = n_tokens // tm
    er = _expert_rows(n_tokens)
    rows = 2 * n_tokens + n_tiles * N_EXPERTS * (SEG - 1) + N_EXPERTS * (er - 1)
    return -(-rows // er) * er


def _moe(cnt, h2, x, gt, info_row, info_col, w_eg, w_eu, w_ed, x_sorted, tm, layer):
    g, t, d = x.shape
    n = g * t
    n_tiles = n // tm
    tpg = t // tm
    rm = gt.shape[1]
    er = _expert_rows(n)
    n_blocks = x_sorted.shape[0] // er
    s_loc = -(-(2 * tm + N_EXPERTS * (SEG - 1)) // GATHER_ROWS) * GATHER_ROWS
    upb = er // SEG
    reg_blk = (jnp.sum(cnt, axis=0) + upb - 1) // upb
    blk_end = jnp.cumsum(reg_blk)
    goff = ((blk_end - reg_blk)[None, :] * upb + jnp.cumsum(cnt, axis=0) - cnt).reshape(-1)
    loff = (jnp.cumsum(cnt, axis=1) - cnt).reshape(-1)
    cntf = cnt.reshape(-1)
    n_valid = blk_end[-1:]
    blk_expert = jnp.minimum(
        jnp.sum(jnp.arange(n_blocks, dtype=jnp.int32)[:, None] >= blk_end[None, :], axis=1), N_EXPERTS - 1
    ).astype(jnp.int32)

    x_sorted = pl.pallas_call(
        _sort_kernel,
        out_shape=jax.ShapeDtypeStruct(x_sorted.shape, BF16),
        grid_spec=pltpu.PrefetchScalarGridSpec(
            num_scalar_prefetch=3,
            grid=(n_tiles,),
            in_specs=[pl.BlockSpec((tm, d), lambda i, *_: (i, 0)),
                      pl.BlockSpec((8, tm), lambda i, *_: (0, i)),
                      pl.BlockSpec(memory_space=pl.ANY)],
            out_specs=pl.BlockSpec(memory_space=pl.ANY),
            scratch_shapes=[pltpu.VMEM((2, s_loc, d), BF16), pltpu.SemaphoreType.DMA((2,))],
        ),
        input_output_aliases={5: 0},
        compiler_params=_cparams(("arbitrary",)),
        name="moe_sort",
    )(cntf, loff, goff, h2, info_row, x_sorted)

    blk = lambda b, be, nv: (jnp.minimum(b, nv[0] - 1), 0)
    d_sorted = pl.pallas_call(
        _experts_kernel,
        out_shape=jax.ShapeDtypeStruct(x_sorted.shape, BF16),
        grid_spec=pltpu.PrefetchScalarGridSpec(
            num_scalar_prefetch=2,
            grid=(n_blocks,),
            in_specs=[pl.BlockSpec((er, d), blk),
                      pl.BlockSpec((None, 1, d, D_EXPERT), lambda b, be, nv: (layer, be[b], 0, 0)),
                      pl.BlockSpec((None, 1, d, D_EXPERT), lambda b, be, nv: (layer, be[b], 0, 0)),
                      pl.BlockSpec((None, 1, D_EXPERT, d), lambda b, be, nv: (layer, be[b], 0, 0))],
            out_specs=pl.BlockSpec((er, d), lambda b, be, nv: (b, 0)),
        ),
        compiler_params=_cparams(("arbitrary",), VMEM_LIMIT),
        name="moe_experts",
    )(blk_expert, n_valid, x_sorted, w_eg, w_eu, w_ed)

    gt_spec = pl.BlockSpec((1, rm, d), (lambda i, *_: (i // tpg, 0, 0)) if rm == 1
                           else (lambda i, *_: (i // tpg, i % tpg, 0)))
    tok3 = pl.BlockSpec((1, tm, d), lambda i, *_: (i // tpg, i % tpg, 0))
    out = pl.pallas_call(
        functools.partial(_combine_kernel, rb=min(tm, 256)),
        out_shape=jax.ShapeDtypeStruct((g, t, d), F32),
        grid_spec=pltpu.PrefetchScalarGridSpec(
            num_scalar_prefetch=3,
            grid=(n_tiles,),
            in_specs=[pl.BlockSpec(memory_space=pl.ANY),
                      pl.BlockSpec((tm, 8), lambda i, *_: (i, 0)),
                      tok3, gt_spec],
            out_specs=tok3,
            scratch_shapes=[pltpu.VMEM((2, s_loc, d), BF16), pltpu.SemaphoreType.DMA((2,))],
        ),
        compiler_params=_cparams(("arbitrary",), VMEM_LIMIT),
        name="moe_combine",
    )(cntf, loff, goff, d_sorted, info_col, x, gt)
    return out, x_sorted


def _rope_tables(pos):
    half = HEAD_DK // 2
    inv_freq = ROPE_BASE ** (-jnp.arange(half, dtype=F32) / half)
    ang = pos[:, None] * inv_freq[None, :]
    cos, sin = jnp.cos(ang), jnp.sin(ang)
    return jnp.concatenate([cos, cos], axis=1), jnp.concatenate([-sin, sin], axis=1)


def _layer(x, mods, lw, cfg, ret0, gla0, cache_k, cache_v, cosf, sinf, x_sorted):
    g, t, d = x.shape
    b, tseq = cfg["b"], cfg["tseq"]
    sh_m, sc_m, gt_m, sh_f, sc_f, gt_f = mods
    proj, log_a = _in_proj(x, sc_m, sh_m, lw["g_mix"], lw["w_in"], lw["w_ga"], lw["a2"], lw["a_bias"],
                           cosf, sinf, lw["gq"], lw["gk"], cfg["tm"], lw["layer"])
    proj_s = proj.reshape(b, tseq, PROJ_COLS)
    log_a_s = log_a.reshape(b, tseq, BRANCH_WIDTH)
    o_r, ret_new = _retention(proj_s, ret0, lw["g_ret_gn"], cfg["chunk"], cfg["tc"])
    o_g, gla_new = _gla(proj_s, log_a_s, gla0, lw["g_gla_gn"], cfg["chunk"], cfg["tc"])
    if cache_k is None:
        o_a = _band_attn_prompt(proj_s, lw["rel_bias"])
    else:
        o_a = _band_attn_sample(proj_s, cache_k, cache_v, lw["rel_bias"], lw["layer"])
    tok = lambda a: a.reshape(g, t, BRANCH_WIDTH)
    x1, h2 = _merge(x, tok(o_r), tok(o_g), tok(o_a), (sc_m, sh_m, gt_m, sc_f, sh_f), lw["g_mix"], lw["g_ffn"],
                    lw["w_gate"], lw["b_gate"], lw["w_branch"], lw["w_out"], cfg["tm"], lw["layer"])
    h2f = h2.reshape(g * t, d)
    info_row, tbl = _route(h2f, lw["wr_t"], lw["br_t"], cfg["tme"], SEG)
    x2, x_sorted = _moe(tbl[:, :, 0], h2f, x1, gt_f, info_row, info_row.T, lw["w_eg"], lw["w_eu"], lw["w_ed"],
                        x_sorted, cfg["tme"], lw["layer"])
    tail = proj_s[:, -min(tseq, BAND_PAST):]
    k_new = tail[:, :, 9 * BRANCH_WIDTH:10 * BRANCH_WIDTH]
    v_new = tail[:, :, 10 * BRANCH_WIDTH:11 * BRANCH_WIDTH]
    return x2, ret_new, gla_new, k_new, v_new, x_sorted


def kernel(x_prompt, x_sample, state_ret, state_gla, cache_att_k, cache_att_v, c_prompt, c_sample,
           w_ada, b_ada, g_mix, w_in, gla_a2, gla_a_bias, g_ret_gn, g_gla_gn, g_q_att, g_k_att, rel_bias,
           w_branch, w_gate, b_gate, w_out, g_ffn, w_router_group, b_router_group, w_router_exp,
           b_router_exp, w_exp_gate, w_exp_up, w_exp_down):
    depth = w_ada.shape[0]
    bp, seq, d = x_prompt.shape
    bs, dseq, _ = x_sample.shape
    n_s = bs * dseq

    pad = (-(bp + bs)) % 8
    c_all = jnp.concatenate([c_prompt, c_sample, jnp.zeros((pad, d), F32)], axis=0)
    mod = _ada_mod(c_all, w_ada, b_ada)

    ga0 = 8 * BRANCH_WIDTH
    w_in_b = jnp.concatenate([w_in[:, :, :ga0], w_in[:, :, ga0 + GLA_RANK:]], axis=2).astype(BF16)
    w_ga = jnp.pad(w_in[:, :, ga0:ga0 + GLA_RANK], ((0, 0), (0, 0), (0, V7X_LANES - GLA_RANK))).astype(BF16)
    a2 = jnp.pad(gla_a2, ((0, 0), (0, V7X_LANES - GLA_RANK), (0, 0))).astype(BF16)
    wr_t = jnp.zeros((depth, 32, d), F32)
    wr_t = wr_t.at[:, 0:N_GROUPS].set(jnp.swapaxes(w_router_group, 1, 2))
    wr_t = wr_t.at[:, 8:8 + N_EXPERTS].set(jnp.swapaxes(w_router_exp, 1, 2)).astype(BF16)
    br_t = jnp.zeros((depth, 32, 1), F32)
    br_t = br_t.at[:, 0:N_GROUPS, 0].set(b_router_group).at[:, 8:8 + N_EXPERTS, 0].set(b_router_exp)
    w_gate_b, w_branch_b, w_out_b = w_gate.astype(BF16), w_branch.astype(BF16), w_out.astype(BF16)
    w_eg, w_eu, w_ed = w_exp_gate, w_exp_up, w_exp_down

    cos_p, sin_p = _rope_tables(jnp.arange(seq, dtype=F32))
    cos_s, sin_s = _rope_tables(PAST_LEN + jnp.arange(dseq, dtype=F32))
    cos_s, sin_s = jnp.tile(cos_s, (bs, 1)), jnp.tile(sin_s, (bs, 1))

    cfg_p = dict(b=bp, tseq=seq, tm=512, chunk=4 * CHUNK, tc=512, tme=512)
    cfg_s = dict(b=bs, tseq=dseq, tm=n_s, chunk=min(dseq, CHUNK), tc=dseq, tme=n_s)
    zero_state = jnp.zeros((bp, RET_HEADS, HEAD_DK, HEAD_DK), F32)
    sorted_p = jnp.zeros((_sorted_rows(bp * seq, cfg_p["tme"]), d), BF16)
    sorted_s = jnp.zeros((_sorted_rows(n_s, cfg_s["tme"]), d), BF16)

    cache_kt = jnp.transpose(cache_att_k, (0, 1, 3, 4, 2))
    cache_vt = jnp.transpose(cache_att_v, (0, 1, 3, 4, 2))

    xp = x_prompt
    xs = x_sample.reshape(1, n_s, d)
    outs = [[] for _ in range(8)]
    for l in range(depth):
        lw = dict(
            layer=l, g_mix=g_mix[l][None], g_ffn=g_ffn[l][None], w_in=w_in_b, w_ga=w_ga[l], a2=a2[l],
            a_bias=gla_a_bias[l][None], gq=jnp.tile(g_q_att[l], ATT_HEADS)[None],
            gk=jnp.tile(g_k_att[l], ATT_HEADS)[None], g_ret_gn=g_ret_gn[l][None], g_gla_gn=g_gla_gn[l][None],
            rel_bias=rel_bias[l], w_gate=w_gate_b, b_gate=b_gate[l][None], w_branch=w_branch_b,
            w_out=w_out_b, wr_t=wr_t[l], br_t=br_t[l], w_eg=w_eg, w_eu=w_eu, w_ed=w_ed)
        mods_p = tuple(m[:, None, :] for m in jnp.split(mod[l, :bp], 6, axis=-1))
        mods_s = tuple(jnp.repeat(m, dseq, axis=0)[None] for m in jnp.split(mod[l, bp:bp + bs], 6, axis=-1))
        xp, rp, gp, kp, vp, sorted_p = _layer(xp, mods_p, lw, cfg_p, zero_state, zero_state, None, None,
                                              cos_p, sin_p, sorted_p)
        xs, rs, gs, ks, vs, sorted_s = _layer(xs, mods_s, lw, cfg_s, state_ret[l], state_gla[l], cache_kt, cache_vt,
                                              cos_s, sin_s, sorted_s)
        heads = lambda a: a.astype(F32).reshape(a.shape[0], a.shape[1], ATT_HEADS, ATT_HD)
        for lst, val in zip(outs, (rp, gp, heads(kp), heads(vp), rs, gs, heads(ks), heads(vs))):
            lst.append(val)
    return (xp, xs.reshape(bs, dseq, d)) + tuple(jnp.stack(o) for o in outs)
```

```python
import functools

import numpy as np
import jax
import jax.numpy as jnp
from jax import lax
from jax.experimental import pallas as pl
from jax.experimental.pallas import tpu as pltpu

F32 = jnp.float32
BF16 = jnp.bfloat16

D_MODEL = 1024
CHUNK = 64
BRANCH_WIDTH = 512
N_BRANCH = 3
RET_HEADS = 4
HEAD_DK = 128
GLA_RANK = 16
GLA_TAU = 16.0
ATT_HEADS = 8
ATT_HD = 64
BAND_PAST = 512
MAX_REL = 128
N_GROUPS = 4
EXPERTS_PER_GROUP = 4
N_EXPERTS = 16
D_EXPERT = 512
ROPE_BASE = 10000.0
EPS = 1e-6
GN_EPS = 1e-5
PAST_LEN = 1024
N_PROJ_BLOCKS = 11
PROJ_COLS = N_PROJ_BLOCKS * BRANCH_WIDTH

V7X_LANES = 128
V7X_VMEM_BYTES = 64 * 1024 * 1024
VMEM_LIMIT = 56 * 1024 * 1024
NEG_BIG = -1e30
LOG2_E = 1.4426950408889634


def _cparams(sem, vmem=None):
    return pltpu.CompilerParams(dimension_semantics=sem, vmem_limit_bytes=vmem)


def _const_spec(shape):
    nd = len(shape)
    return pl.BlockSpec(shape, lambda *_: (0,) * nd, pipeline_mode=pl.Buffered(1))


def _layer_spec(shape, layer):
    nd = len(shape)
    return pl.BlockSpec((None,) + tuple(shape), lambda *_: (layer,) + (0,) * nd, pipeline_mode=pl.Buffered(1))


def _dot(a, b):
    return jnp.dot(a, b, preferred_element_type=F32)


def _dot_nt(a, b):
    return lax.dot_general(a, b, (((1,), (1,)), ((), ())), preferred_element_type=F32)


def _dot_tn(a, b):
    return lax.dot_general(a, b, (((0,), (0,)), ((), ())), preferred_element_type=F32)


def _silu(x):
    return x * jax.nn.sigmoid(x)


def _norm_mod(x, g, sc, sh):
    ms = jnp.mean(x * x, axis=-1, keepdims=True)
    return x * lax.rsqrt(ms + EPS) * g * (1.0 + sc) + sh


def _ada_kernel(c_ref, w_ref, b_ref, o_ref):
    s = _silu(c_ref[...])
    o_ref[0] = _dot(s.astype(BF16), w_ref[0].astype(BF16)) + b_ref[0]


def _ada_mod(c_all, w_ada, b_ada):
    depth, d, n = w_ada.shape
    rows = c_all.shape[0]
    bn = 1536
    return pl.pallas_call(
        _ada_kernel,
        out_shape=jax.ShapeDtypeStruct((depth, rows, n), F32),
        grid=(depth, n // bn),
        in_specs=[
            pl.BlockSpec((rows, d), lambda l, j: (0, 0)),
            pl.BlockSpec((1, d, bn), lambda l, j: (l, 0, j)),
            pl.BlockSpec((1, 1, bn), lambda l, j: (l, 0, j)),
        ],
        out_specs=pl.BlockSpec((1, rows, bn), lambda l, j: (l, 0, j)),
        compiler_params=_cparams(("arbitrary", "arbitrary"), VMEM_LIMIT),
        name="ada_mod",
    )(c_all, w_ada, b_ada.reshape(depth, 1, n))


def _rope_heads(a, cosf, sinf):
    outs = []
    for h in range(RET_HEADS):
        ah = a[:, h * HEAD_DK:(h + 1) * HEAD_DK]
        outs.append(ah * cosf + pltpu.roll(ah, HEAD_DK // 2, 1) * sinf)
    return jnp.concatenate(outs, axis=1)


def _rms_heads64(a, gain):
    low = lax.broadcasted_iota(jnp.int32, (1, V7X_LANES), 1) < ATT_HD
    outs = []
    for c in range(a.shape[1] // V7X_LANES):
        ac = a[:, c * V7X_LANES:(c + 1) * V7X_LANES]
        sq = ac * ac
        lo = jnp.sum(jnp.where(low, sq, 0.0), axis=-1, keepdims=True)
        hi = jnp.sum(jnp.where(low, 0.0, sq), axis=-1, keepdims=True)
        ms = jnp.where(low, lo, hi) * (1.0 / ATT_HD)
        outs.append(ac * lax.rsqrt(ms + EPS))
    return jnp.concatenate(outs, axis=1) * gain


def _in_proj_kernel(x_ref, sc_ref, sh_ref, g_ref, w_ref, wga_ref, a2_ref, ab_ref, cos_ref, sin_ref,
                    gq_ref, gk_ref, proj_ref, la_ref):
    x = x_ref[0]
    hb = _norm_mod(x, g_ref[...], sc_ref[0], sh_ref[0]).astype(BF16)
    cosf = cos_ref[...]
    sinf = sin_ref[...]
    ga = _dot(hb, wga_ref[...])
    for j in range(N_PROJ_BLOCKS):
        if j == N_PROJ_BLOCKS // 2:
            z = _dot(ga.astype(BF16), a2_ref[...]) + ab_ref[...]
            la_ref[0] = jax.nn.log_sigmoid(z) * (1.0 / GLA_TAU)
        cols = slice(j * BRANCH_WIDTH, (j + 1) * BRANCH_WIDTH)
        acc = _dot(hb, w_ref[:, cols])
        if j == 0:
            acc = _rope_heads(acc, cosf, sinf)
        elif j == 1:
            acc = _rope_heads(acc, cosf, sinf) * (HEAD_DK ** -0.5)
        elif j == 4:
            acc = acc * (HEAD_DK ** -0.5)
        elif j == 8:
            acc = _rms_heads64(acc, gq_ref[...]) * (ATT_HD ** -0.5 * LOG2_E)
        elif j == 9:
            acc = _rms_heads64(acc, gk_ref[...])
        proj_ref[0, :, cols] = acc.astype(BF16)


def _in_proj(x, sc, sh, g_mix, w_in, w_ga, a2, a_bias, cosf, sinf, gq, gk, tm, layer):
    g, t, d = x.shape
    rm = sc.shape[1]
    mod_spec = pl.BlockSpec((1, rm, d), (lambda b, i: (b, 0, 0)) if rm == 1 else (lambda b, i: (b, i, 0)))
    return pl.pallas_call(
        _in_proj_kernel,
        out_shape=(jax.ShapeDtypeStruct((g, t, PROJ_COLS), BF16),
                   jax.ShapeDtypeStruct((g, t, BRANCH_WIDTH), F32)),
        grid=(g, t // tm),
        in_specs=[
            pl.BlockSpec((1, tm, d), lambda b, i: (b, i, 0)),
            mod_spec, mod_spec,
            _const_spec((1, d)),
            _layer_spec((d, PROJ_COLS), layer),
            _const_spec((d, V7X_LANES)),
            _const_spec((V7X_LANES, BRANCH_WIDTH)),
            _const_spec((1, BRANCH_WIDTH)),
            pl.BlockSpec((tm, HEAD_DK), lambda b, i: (i, 0)),
            pl.BlockSpec((tm, HEAD_DK), lambda b, i: (i, 0)),
            _const_spec((1, BRANCH_WIDTH)),
            _const_spec((1, BRANCH_WIDTH)),
        ],
        out_specs=(pl.BlockSpec((1, tm, PROJ_COLS), lambda b, i: (b, i, 0)),
                   pl.BlockSpec((1, tm, BRANCH_WIDTH), lambda b, i: (b, i, 0))),
        compiler_params=_cparams(("arbitrary", "arbitrary"), VMEM_LIMIT),
        name="in_proj",
    )(x, sc, sh, g_mix, w_in, w_ga, a2, a_bias, cosf, sinf, gq, gk)


def _retention_kernel(q_ref, k_ref, v_ref, g_ref, s0_ref, dm_ref, qd_ref, kd_ref, bd_ref, gn_ref,
                      o_ref, so_ref, st_ref, *, chunk, n_chunks):
    n_seq = q_ref.shape[0]

    @pl.when(pl.program_id(1) == 0)
    def _():
        st_ref[...] = s0_ref[...]

    def body(c, carry):
        rows = pl.ds(pl.multiple_of(c * chunk, chunk), chunk)
        for b in range(n_seq):
            for h in range(RET_HEADS):
                cols = slice(h * HEAD_DK, (h + 1) * HEAD_DK)
                q = q_ref[b, rows, cols]
                k = k_ref[b, rows, cols]
                v = v_ref[b, rows, cols]
                scores = _dot_nt(q, k) * dm_ref[h]
                o = _dot(scores.astype(BF16), v)
                o = o + _dot((q.astype(F32) * qd_ref[h]).astype(BF16), st_ref[b, h].astype(BF16))
                st_ref[b, h] = (st_ref[b, h] * bd_ref[h]
                                + _dot_tn((k.astype(F32) * kd_ref[h]).astype(BF16), v))
                mu = jnp.mean(o, axis=-1, keepdims=True)
                oc = o - mu
                var = jnp.mean(oc * oc, axis=-1, keepdims=True)
                on = oc * lax.rsqrt(var + GN_EPS) * gn_ref[:, cols]
                o_ref[b, rows, cols] = (_silu(g_ref[b, rows, cols].astype(F32)) * on).astype(BF16)
        return carry

    lax.fori_loop(0, n_chunks, body, 0)

    @pl.when(pl.program_id(1) == pl.num_programs(1) - 1)
    def _():
        so_ref[...] = st_ref[...]


def _retention_tables(chunk):
    log_gamma = jnp.log(1.0 - jnp.exp2(-5.0 - jnp.arange(RET_HEADS, dtype=F32)))
    idx = jnp.arange(chunk, dtype=F32)
    diff = idx[:, None] - idx[None, :]
    dmask = jnp.where(diff >= 0, jnp.exp(log_gamma[:, None, None] * jnp.maximum(diff, 0.0)), 0.0)
    qd = jnp.exp(log_gamma[:, None] * (idx + 1.0))[:, :, None]
    kd = jnp.exp(log_gamma[:, None] * (chunk - 1.0 - idx))[:, :, None]
    bd = jnp.exp(log_gamma * chunk)[:, None, None]
    bc = lambda a, r: jnp.broadcast_to(a, (RET_HEADS, r, HEAD_DK)).astype(F32)
    return dmask.astype(F32), bc(qd, chunk), bc(kd, chunk), bc(bd, 1)


def _seqs_per_step(b):
    return 4 if b % 4 == 0 and b > 4 else (2 if b % 2 == 0 else 1)


def _retention(proj, state0, gn, chunk, tc):
    b, t, _ = proj.shape
    nb = _seqs_per_step(b)
    dm, qd, kd, bd = _retention_tables(chunk)
    col = lambda j: pl.BlockSpec((nb, tc, BRANCH_WIDTH), lambda i, s, j=j: (i, s, j))
    st_spec = pl.BlockSpec((nb, RET_HEADS, HEAD_DK, HEAD_DK), lambda i, s: (i, 0, 0, 0))
    return pl.pallas_call(
        functools.partial(_retention_kernel, chunk=chunk, n_chunks=tc // chunk),
        out_shape=(jax.ShapeDtypeStruct((b, t, BRANCH_WIDTH), BF16),
                   jax.ShapeDtypeStruct((b, RET_HEADS, HEAD_DK, HEAD_DK), F32)),
        grid=(b // nb, t // tc),
        in_specs=[col(0), col(1), col(2), col(3), st_spec,
                  _const_spec(dm.shape), _const_spec(qd.shape), _const_spec(kd.shape), _const_spec(bd.shape),
                  _const_spec((1, BRANCH_WIDTH))],
        out_specs=(pl.BlockSpec((nb, tc, BRANCH_WIDTH), lambda i, s: (i, s, 0)), st_spec),
        scratch_shapes=[pltpu.VMEM((nb, RET_HEADS, HEAD_DK, HEAD_DK), F32)],
        compiler_params=_cparams(("arbitrary", "arbitrary")),
        name="retention",
    )(proj, proj, proj, proj, state0, dm, qd, kd, bd, gn)


GLA_SAFE_LOG_DECAY = -60.0


def _gla_intra_direct(q, bh, rows_ref, chunk):
    trow = lax.broadcasted_iota(jnp.int32, (chunk, 1), 0)

    def step(s, acc):
        bs = rows_ref[0, pl.ds(s, 1), :]
        ks = rows_ref[1, pl.ds(s, 1), :]
        vs = rows_ref[2, pl.ds(s, 1), :]
        e = jnp.exp(jnp.minimum(bh - bs, 0.0))
        col = jnp.sum(q * ks * e, axis=-1, keepdims=True)
        return acc + jnp.where(trow >= s, col, 0.0) * vs

    return lax.fori_loop(0, chunk, step, jnp.zeros((chunk, HEAD_DK), F32))


def _gla_kernel(q_ref, k_ref, v_ref, r_ref, la_ref, s0_ref, tri_ref, gn_ref, o_ref, so_ref, st_ref, rows_ref,
                *, chunk, n_chunks):
    n_seq = q_ref.shape[0]

    @pl.when(pl.program_id(1) == 0)
    def _():
        for b in range(n_seq):
            for h in range(RET_HEADS):
                st_ref[b, h] = s0_ref[b, h].T

    tri = tri_ref[...]
    causal = (lax.broadcasted_iota(jnp.int32, (chunk, chunk), 0)
              >= lax.broadcasted_iota(jnp.int32, (chunk, chunk), 1))

    def chunk_body(factored):
        def body(c, carry):
            rows = pl.ds(pl.multiple_of(c * chunk, chunk), chunk)
            for b in range(n_seq):
                la = la_ref[b, rows, :]
                la_hi = la.astype(BF16)
                la_lo = (la - la_hi.astype(F32)).astype(BF16)
                b_all = _dot(tri, la_hi) + _dot(tri, la_lo)
                for h in range(RET_HEADS):
                    cols = slice(h * HEAD_DK, (h + 1) * HEAD_DK)
                    q = q_ref[b, rows, cols].astype(F32)
                    k = k_ref[b, rows, cols].astype(F32)
                    v = v_ref[b, rows, cols]
                    bh = b_all[:, cols]
                    bl = bh[chunk - 1:chunk, :]
                    qh = (q * jnp.exp(bh)).astype(BF16)
                    if factored:
                        kh = (k * jnp.exp(-bh)).astype(BF16)
                        scores = jnp.where(causal, _dot_nt(qh, kh), 0.0)
                        o = _dot(scores.astype(BF16), v)
                    else:
                        rows_ref[0] = bh
                        rows_ref[1] = k
                        rows_ref[2] = v.astype(F32)
                        o = _gla_intra_direct(q, bh, rows_ref, chunk)
                    o = o + _dot_nt(qh, st_ref[b, h].astype(BF16))
                    kd = (k * jnp.exp(bl - bh)).astype(BF16)
                    st_ref[b, h] = st_ref[b, h] * jnp.exp(bl) + _dot_tn(v, kd)
                    ms = jnp.mean(o * o, axis=-1, keepdims=True)
                    on = o * lax.rsqrt(ms + EPS) * gn_ref[:, cols]
                    o_ref[b, rows, cols] = (_silu(r_ref[b, rows, cols].astype(F32)) * on).astype(BF16)
            return carry
        return body

    lowest = None
    for b in range(n_seq):
        for c in range(n_chunks):
            tot = jnp.sum(la_ref[b, c * chunk:(c + 1) * chunk, :], axis=0, keepdims=True)
            lowest = tot if lowest is None else jnp.minimum(lowest, tot)
    safe = jnp.min(lowest) > GLA_SAFE_LOG_DECAY

    @pl.when(safe)
    def _():
        lax.fori_loop(0, n_chunks, chunk_body(True), 0)

    @pl.when(jnp.logical_not(safe))
    def _():
        lax.fori_loop(0, n_chunks, chunk_body(False), 0)

    @pl.when(pl.program_id(1) == pl.num_programs(1) - 1)
    def _():
        for b in range(n_seq):
            for h in range(RET_HEADS):
                so_ref[b, h] = st_ref[b, h].T


def _gla(proj, log_a, state0, gn, chunk, tc):
    b, t, _ = proj.shape
    nb = _seqs_per_step(b)
    tri = jnp.tril(jnp.ones((chunk, chunk), F32)).astype(BF16)
    col = lambda j: pl.BlockSpec((nb, tc, BRANCH_WIDTH), lambda i, s, j=j: (i, s, j))
    st_spec = pl.BlockSpec((nb, RET_HEADS, HEAD_DK, HEAD_DK), lambda i, s: (i, 0, 0, 0))
    return pl.pallas_call(
        functools.partial(_gla_kernel, chunk=chunk, n_chunks=tc // chunk),
        out_shape=(jax.ShapeDtypeStruct((b, t, BRANCH_WIDTH), BF16),
                   jax.ShapeDtypeStruct((b, RET_HEADS, HEAD_DK, HEAD_DK), F32)),
        grid=(b // nb, t // tc),
        in_specs=[col(4), col(5), col(6), col(7),
                  pl.BlockSpec((nb, tc, BRANCH_WIDTH), lambda i, s: (i, s, 0)),
                  st_spec, _const_spec((chunk, chunk)), _const_spec((1, BRANCH_WIDTH))],
        out_specs=(pl.BlockSpec((nb, tc, BRANCH_WIDTH), lambda i, s: (i, s, 0)), st_spec),
        scratch_shapes=[pltpu.VMEM((nb, RET_HEADS, HEAD_DK, HEAD_DK), F32),
                        pltpu.VMEM((3, chunk, HEAD_DK), F32)],
        compiler_params=_cparams(("arbitrary", "arbitrary")),
        name="gla",
    )(proj, proj, proj, proj, log_a, state0, tri, gn)


def _attn_stage_scratch(rb, win):
    pairs = ATT_HEADS // 2
    return [pltpu.VMEM((2, pairs, 2 * rb, win), F32), pltpu.VMEM((2, pairs, 2 * rb, win), BF16)]


def _band_attn_kernel(q_ref, kp_ref, kc_ref, vp_ref, vc_ref, bias_ref, o_ref, kw_ref, vw_ref,
                      s_ref, e_ref,
                      *, n_sub, rb, win, prev_rows, mask_start):
    cur_rows = kc_ref.shape[1]
    kw_ref[0:prev_rows, :] = kp_ref[0].astype(BF16)
    kw_ref[prev_rows:prev_rows + cur_rows, :] = kc_ref[0]
    pairs = ATT_HEADS // 2
    for p in range(pairs):
        cols = slice(p * V7X_LANES, (p + 1) * V7X_LANES)
        vw_ref[0:prev_rows, 2 * p * V7X_LANES:(2 * p + 1) * V7X_LANES] = vp_ref[0, :, cols].astype(BF16)
        vw_ref[prev_rows:prev_rows + cur_rows, 2 * p * V7X_LANES:(2 * p + 1) * V7X_LANES] = vc_ref[0, :, cols]
        vw_ref[:, (2 * p + 1) * V7X_LANES:(2 * p + 2) * V7X_LANES] = jnp.ones((vw_ref.shape[0], V7X_LANES), BF16)
    low = lax.broadcasted_iota(jnp.int32, (1, V7X_LANES), 1) < ATT_HD
    q0 = pl.program_id(1) * cur_rows

    def scores(i, masked):
        if masked:
            valid = lax.broadcasted_iota(jnp.int32, (1, win), 1) >= prev_rows - q0 - i * rb
        for p in range(pairs):
            cols = slice(p * V7X_LANES, (p + 1) * V7X_LANES)
            q2 = q_ref[0, i * rb:(i + 1) * rb, cols]
            zero = jnp.zeros_like(q2)
            qst = jnp.concatenate([jnp.where(low, q2, zero), jnp.where(low, zero, q2)], axis=0)
            s = _dot_nt(qst, kw_ref[i * rb:i * rb + win, cols]) + bias_ref[p]
            if masked:
                s = jnp.where(valid, s, NEG_BIG)
            s_ref[i % 2, p] = s

    def softmax(i):
        for p in range(pairs):
            s = s_ref[i % 2, p]
            e_ref[i % 2, p] = jnp.exp2(s - jnp.max(s, axis=-1, keepdims=True)).astype(BF16)

    def values(i):
        for p in range(pairs):
            cols = slice(p * V7X_LANES, (p + 1) * V7X_LANES)
            od = _dot(e_ref[i % 2, p], vw_ref[i * rb:i * rb + win, 2 * p * V7X_LANES:2 * (p + 1) * V7X_LANES])
            o2 = od[:, :V7X_LANES] / od[:, V7X_LANES:]
            o_ref[0, i * rb:(i + 1) * rb, cols] = jnp.where(low, o2[:rb], o2[rb:]).astype(BF16)

    def pipeline(masked):
        for step in range(n_sub + 2):
            if step < n_sub:
                scores(step, masked)
            if 0 <= step - 1 < n_sub:
                softmax(step - 1)
            if 0 <= step - 2 < n_sub:
                values(step - 2)

    if mask_start:
        pl.when(q0 < prev_rows)(functools.partial(pipeline, True))
        pl.when(q0 >= prev_rows)(functools.partial(pipeline, False))
    else:
        pipeline(False)


def _rel_bias_table(rel_bias, rb, win, chunk):
    period = win + rb
    u = np.arange(period)
    u = np.where(u < win, u, u - period)
    vec = rel_bias.astype(F32)[:, np.clip(BAND_PAST - u, -MAX_REL, MAX_REL) + MAX_REL]
    bias = jnp.tile(vec, (1, rb))[:, :rb * (period - 1)].reshape(-1, rb, period - 1)[:, :, :win]
    r = np.arange(rb)[:, None]
    j = np.arange(win)[None, :]
    c0 = (r // chunk) * chunk
    allowed = (j >= c0) & (j < c0 + BAND_PAST + chunk)
    bias = jnp.where(jnp.asarray(allowed)[None], bias * LOG2_E, NEG_BIG)
    return bias.reshape(ATT_HEADS // 2, 2 * rb, win)


def _band_attn_prompt(proj, rel_bias):
    b, t, _ = proj.shape
    qb, rb = 512, 128
    win = BAND_PAST + rb
    bias = _rel_bias_table(rel_bias, rb, win, CHUNK)
    cur = lambda j: pl.BlockSpec((1, qb, BRANCH_WIDTH), lambda i, s, j=j: (i, s, j))
    prev = lambda j: pl.BlockSpec((1, qb, BRANCH_WIDTH), lambda i, s, j=j: (i, jnp.maximum(s - 1, 0), j))
    return pl.pallas_call(
        functools.partial(_band_attn_kernel, n_sub=qb // rb, rb=rb, win=win, prev_rows=BAND_PAST,
                          mask_start=True),
        out_shape=jax.ShapeDtypeStruct((b, t, BRANCH_WIDTH), BF16),
        grid=(b, t // qb),
        in_specs=[cur(8), prev(9), cur(9), prev(10), cur(10), _const_spec(bias.shape)],
        out_specs=pl.BlockSpec((1, qb, BRANCH_WIDTH), lambda i, s: (i, s, 0)),
        scratch_shapes=[pltpu.VMEM((BAND_PAST + qb, BRANCH_WIDTH), BF16),
                        pltpu.VMEM((BAND_PAST + qb, 2 * BRANCH_WIDTH), BF16)] + _attn_stage_scratch(rb, win),
        compiler_params=_cparams(("arbitrary", "arbitrary")),
        name="band_attn",
    )(proj, proj, proj, proj, proj, bias)


def _band_attn_sample_kernel(q_ref, kn_ref, vn_ref, kc_ref, vc_ref, bias_ref, o_ref):
    t = q_ref.shape[1]
    past = kc_ref.shape[-1]
    low = lax.broadcasted_iota(jnp.int32, (1, V7X_LANES), 1) < ATT_HD
    for p in range(ATT_HEADS // 2):
        cols = slice(p * V7X_LANES, (p + 1) * V7X_LANES)
        q2 = q_ref[0, :, cols]
        zero = jnp.zeros_like(q2)
        qst = jnp.concatenate([jnp.where(low, q2, zero), jnp.where(low, zero, q2)], axis=0)
        kt = kc_ref[0, 2 * p:2 * p + 2].reshape(V7X_LANES, past).astype(BF16)
        vt = vc_ref[0, 2 * p:2 * p + 2].reshape(V7X_LANES, past).astype(BF16)
        s = jnp.concatenate([_dot(qst, kt), _dot_nt(qst, kn_ref[0, :, cols])], axis=1) + bias_ref[p]
        m = jnp.max(s, axis=-1, keepdims=True)
        e = jnp.exp2(s - m)
        den = jnp.sum(e, axis=-1, keepdims=True)
        eb = e.astype(BF16)
        o2 = (_dot_nt(eb[:, :past], vt) + _dot(eb[:, past:], vn_ref[0, :, cols])) / den
        o_ref[0, :, cols] = jnp.where(low, o2[:t], o2[t:]).astype(BF16)


def _band_attn_sample(proj, cache_kt, cache_vt, rel_bias, layer):
    b, t, _ = proj.shape
    win = BAND_PAST + t
    bias = _rel_bias_table(rel_bias, t, win, t)
    cur = lambda j: pl.BlockSpec((1, t, BRANCH_WIDTH), lambda i, j=j: (i, 0, j))
    cache = pl.BlockSpec((None, 1, ATT_HEADS, ATT_HD, BAND_PAST), lambda i: (layer, i, 0, 0, 0))
    return pl.pallas_call(
        _band_attn_sample_kernel,
        out_shape=jax.ShapeDtypeStruct((b, t, BRANCH_WIDTH), BF16),
        grid=(b,),
        in_specs=[cur(8), cur(9), cur(10), cache, cache, _const_spec(bias.shape)],
        out_specs=pl.BlockSpec((1, t, BRANCH_WIDTH), lambda i: (i, 0, 0)),
        compiler_params=_cparams(("arbitrary",)),
        name="band_attn_sample",
    )(proj, proj, proj, cache_kt, cache_vt, bias)


def _merge_kernel(x_ref, or_ref, og_ref, oa_ref, scm_ref, shm_ref, gtm_ref, scf_ref, shf_ref,
                  gmix_ref, gffn_ref, wg_ref, bg_ref, wb_ref, wo_ref, xo_ref, h2_ref):
    x = x_ref[0]
    d = x.shape[1]
    hb = _norm_mod(x, gmix_ref[...], scm_ref[0], shm_ref[0]).astype(BF16)
    merged = None
    for n, o_ref in enumerate((or_ref, og_ref, oa_ref)):
        gate = jax.nn.sigmoid(_dot(hb, wg_ref[:, n * d:(n + 1) * d]) + bg_ref[:, n * d:(n + 1) * d])
        y = gate * _dot(o_ref[0], wb_ref[n])
        merged = y if merged is None else merged + y
    mix = _dot(merged.astype(BF16), wo_ref[...])
    xn = x + gtm_ref[0] * mix
    xo_ref[0] = xn
    h2_ref[0] = _norm_mod(xn, gffn_ref[...], scf_ref[0], shf_ref[0]).astype(BF16)


def _merge(x, o_r, o_g, o_a, mods, g_mix, g_ffn, w_gate, b_gate, w_branch, w_out, tm, layer):
    g, t, d = x.shape
    rm = mods[0].shape[1]
    mod_spec = pl.BlockSpec((1, rm, d), (lambda b, i: (b, 0, 0)) if rm == 1 else (lambda b, i: (b, i, 0)))
    tok = lambda w: pl.BlockSpec((1, tm, w), lambda b, i: (b, i, 0))
    return pl.pallas_call(
        _merge_kernel,
        out_shape=(jax.ShapeDtypeStruct((g, t, d), F32), jax.ShapeDtypeStruct((g, t, d), BF16)),
        grid=(g, t // tm),
        in_specs=[tok(d), tok(BRANCH_WIDTH), tok(BRANCH_WIDTH), tok(BRANCH_WIDTH)] + [mod_spec] * 5 + [
            _const_spec((1, d)), _const_spec((1, d)),
            _layer_spec((d, N_BRANCH * d), layer), _const_spec((1, N_BRANCH * d)),
            _layer_spec((N_BRANCH, BRANCH_WIDTH, d), layer), _layer_spec((d, d), layer)],
        out_specs=(tok(d), tok(d)),
        compiler_params=_cparams(("arbitrary", "arbitrary"), VMEM_LIMIT),
        name="merge",
    )(x, o_r, o_g, o_a, *mods, g_mix, g_ffn, w_gate, b_gate, w_branch, w_out)


def _first_argmax(vals, n):
    row = lax.broadcasted_iota(jnp.int32, vals.shape, 0).astype(F32)
    m = jnp.max(vals, axis=0, keepdims=True)
    idx = jnp.min(jnp.where(vals == m, row, float(n)), axis=0, keepdims=True)
    return m, idx


def _route_kernel(h_ref, wr_ref, br_ref, up_ref, l16_ref, info_ref, tbl_ref, *, rows_per_block):
    tm = h_ref.shape[0]
    lg = _dot_nt(wr_ref[...], h_ref[...]) + br_ref[...]
    g = lg[0:N_GROUPS]
    gmax, grp = _first_argmax(g, N_GROUPS)
    p_group = 1.0 / jnp.sum(jnp.exp(g - gmax), axis=0, keepdims=True)
    esel = jnp.zeros((EXPERTS_PER_GROUP, tm), F32)
    for gi in range(N_GROUPS):
        blk = lg[8 + gi * EXPERTS_PER_GROUP:8 + (gi + 1) * EXPERTS_PER_GROUP]
        esel = esel + jnp.where(grp == float(gi), blk, 0.0)
    v1, i1 = _first_argmax(esel, EXPERTS_PER_GROUP)
    row4 = lax.broadcasted_iota(jnp.int32, esel.shape, 0).astype(F32)
    v2, i2 = _first_argmax(jnp.where(row4 == i1, -jnp.inf, esel), EXPERTS_PER_GROUP)
    e21 = jnp.exp(v2 - v1)
    w1 = p_group / (1.0 + e21)
    w2 = p_group * e21 / (1.0 + e21)
    e1 = grp * float(EXPERTS_PER_GROUP) + i1
    e2 = grp * float(EXPERTS_PER_GROUP) + i2
    row16 = lax.broadcasted_iota(jnp.int32, (N_EXPERTS, tm), 0).astype(F32)
    hit1 = row16 == e1
    hit2 = row16 == e2
    onehot = jnp.where(hit1 | hit2, 1.0, 0.0)
    prefix = _dot(onehot.astype(BF16), up_ref[...])
    cnt = jnp.sum(onehot, axis=1, keepdims=True)
    nblk = jnp.floor((cnt + float(rows_per_block - 1)) * (1.0 / rows_per_block))
    nblk_b = jnp.broadcast_to(nblk, (N_EXPERTS, V7X_LANES))
    offb = _dot(l16_ref[...], nblk_b.astype(BF16))[:, 0:1]
    base = offb * float(rows_per_block) + prefix
    info_ref[0:1, :] = jnp.sum(jnp.where(hit1, base, 0.0), axis=0, keepdims=True)
    info_ref[1:2, :] = jnp.sum(jnp.where(hit2, base, 0.0), axis=0, keepdims=True)
    info_ref[2:3, :] = w1
    info_ref[3:4, :] = w2
    info_ref[4:8, :] = jnp.zeros((4, tm), F32)
    tbl_ref[0] = nblk_b.astype(jnp.int32)


def _route(h2, wr_t, br_t, tm, rows_per_block):
    n, d = h2.shape
    nt = n // tm
    upper = jnp.triu(jnp.ones((tm, tm), F32), 1).astype(BF16)
    l16 = jnp.tril(jnp.ones((N_EXPERTS, N_EXPERTS), F32), -1).astype(BF16)
    return pl.pallas_call(
        functools.partial(_route_kernel, rows_per_block=rows_per_block),
        out_shape=(jax.ShapeDtypeStruct((8, n), F32),
                   jax.ShapeDtypeStruct((nt, N_EXPERTS, V7X_LANES), jnp.int32)),
        grid=(nt,),
        in_specs=[pl.BlockSpec((tm, d), lambda i: (i, 0)),
                  _const_spec((32, d)), _const_spec((32, 1)),
                  _const_spec((tm, tm)), _const_spec((N_EXPERTS, N_EXPERTS))],
        out_specs=(pl.BlockSpec((8, tm), lambda i: (0, i)),
                   pl.BlockSpec((1, N_EXPERTS, V7X_LANES), lambda i: (i, 0, 0))),
        compiler_params=_cparams(("arbitrary",)),
        name="route",
    )(h2, wr_t, br_t, upper, l16)


SEG = 16
GATHER_ROWS = 256
MAX_EXPERT_ROWS = 512


def _expert_rows(n_tokens):
    mean_rows = 2 * n_tokens // N_EXPERTS
    return int(min(MAX_EXPERT_ROWS, max(4 * SEG, 1 << (mean_rows // 4).bit_length())))


def _segment_copies(cnt_ref, loff_ref, goff_ref, tile, local_ref, global_ref, sem, to_global):
    for e in range(N_EXPERTS):
        cnt = cnt_ref[tile * N_EXPERTS + e]
        rows = pl.multiple_of(cnt * SEG, SEG)
        loc = local_ref.at[pl.ds(pl.multiple_of(loff_ref[tile * N_EXPERTS + e] * SEG, SEG), rows)]
        glo = global_ref.at[pl.ds(pl.multiple_of(goff_ref[tile * N_EXPERTS + e] * SEG, SEG), rows)]
        cp = pltpu.make_async_copy(loc, glo, sem) if to_global else pltpu.make_async_copy(glo, loc, sem)
        yield cnt > 0, cp


def _tile_units(cnt_ref, tile):
    total = jnp.int32(0)
    for e in range(N_EXPERTS):
        total = total + cnt_ref[tile * N_EXPERTS + e]
    return total


def _wait_segments(cnt_ref, tile, local_ref, global_ref, sem, to_global):
    rows = pl.multiple_of(_tile_units(cnt_ref, tile) * SEG, SEG)
    loc = local_ref.at[pl.ds(0, rows)]
    glo = global_ref.at[pl.ds(0, rows)]
    cp = pltpu.make_async_copy(loc, glo, sem) if to_global else pltpu.make_async_copy(glo, loc, sem)
    pl.when(rows > 0)(cp.wait)


def _sort_kernel(cnt_ref, loff_ref, goff_ref, h_ref, irow_ref, xin_ref, xout_ref, xs2_ref, sem2):
    del xin_ref
    i = pl.program_id(0)
    last = pl.num_programs(0) - 1
    slot = lax.rem(i, 2)
    xs_ref = xs2_ref.at[slot]
    sem = sem2.at[slot]

    def wait_tile(tile, s):
        _wait_segments(cnt_ref, tile, xs2_ref.at[s], xout_ref, sem2.at[s], True)

    @pl.when(i >= 2)
    def _():
        wait_tile(i - 2, slot)

    gr = GATHER_ROWS
    n_gather = (_tile_units(cnt_ref, i) * SEG + gr - 1) // gr
    pos1 = irow_ref[0:1, :]
    pos2 = irow_ref[1:2, :]
    h = h_ref[...]

    def gather(gb, carry):
        r0 = pl.multiple_of(gb * gr, gr)
        srow = (lax.broadcasted_iota(jnp.int32, (gr, 1), 0) + r0).astype(F32)
        sel = jnp.where((srow == pos1) | (srow == pos2), 1.0, 0.0).astype(BF16)
        xs_ref[pl.ds(r0, gr), :] = _dot(sel, h).astype(BF16)
        return carry

    lax.fori_loop(0, n_gather, gather, 0)
    for pred, cp in _segment_copies(cnt_ref, loff_ref, goff_ref, i, xs_ref, xout_ref, sem, True):
        pl.when(pred)(cp.start)

    @pl.when(i == last)
    def _():
        @pl.when(i >= 1)
        def _():
            wait_tile(i - 1, 1 - slot)

        wait_tile(i, slot)


def _experts_kernel(be_ref, nv_ref, x_ref, wg_ref, wu_ref, wd_ref, o_ref):
    del be_ref
    used = pl.program_id(0) < nv_ref[0]

    @pl.when(used)
    def _():
        xb = x_ref[...]
        a = _silu(_dot(xb, wg_ref[0].astype(BF16))) * _dot(xb, wu_ref[0].astype(BF16))
        o_ref[...] = _dot(a.astype(BF16), wd_ref[0].astype(BF16)).astype(BF16)

    @pl.when(jnp.logical_not(used))
    def _():
        o_ref[...] = jnp.zeros(o_ref.shape, BF16)


def _combine_kernel(cnt_ref, loff_ref, goff_ref, ds_ref, icol_ref, x_ref, gt_ref, o_ref, dl2_ref, sem2, *, rb):
    i = pl.program_id(0)
    tm = x_ref.shape[1]
    s_loc = dl2_ref.shape[1]
    slot = lax.rem(i, 2)

    def fetch_tile(tile, s):
        for pred, cp in _segment_copies(cnt_ref, loff_ref, goff_ref, tile, dl2_ref.at[s], ds_ref, sem2.at[s], False):
            pl.when(pred)(cp.start)

        def zero_tail(u, carry):
            dl2_ref[s, pl.ds(pl.multiple_of(u * SEG, SEG), SEG), :] = jnp.zeros((SEG, dl2_ref.shape[2]), BF16)
            return carry

        lax.fori_loop(_tile_units(cnt_ref, tile), s_loc // SEG, zero_tail, 0)

    @pl.when(i == 0)
    def _():
        fetch_tile(0, 0)

    @pl.when(i + 1 < pl.num_programs(0))
    def _():
        fetch_tile(i + 1, 1 - slot)

    _wait_segments(cnt_ref, i, dl2_ref.at[slot], ds_ref, sem2.at[slot], False)
    scol = lax.broadcasted_iota(jnp.int32, (1, s_loc), 1).astype(F32)
    for r in range(tm // rb):
        rows = slice(r * rb, (r + 1) * rb)
        cmb = (jnp.where(scol == icol_ref[rows, 0:1], icol_ref[rows, 2:3], 0.0)
               + jnp.where(scol == icol_ref[rows, 1:2], icol_ref[rows, 3:4], 0.0))
        y = _dot(cmb.astype(BF16), dl2_ref[slot])
        gt = gt_ref[0] if gt_ref.shape[1] == 1 else gt_ref[0, rows, :]
        o_ref[0, rows, :] = x_ref[0, rows, :] + gt * y


def _sorted_rows(n_tokens, tm):
    n_tiles = n_tokens // tm
    er = _expert_rows(n_tokens)
    rows = 2 * n_tokens + n_tiles * N_EXPERTS * (SEG - 1) + N_EXPERTS * (er - 1)
    return -(-rows // er) * er


def _moe(cnt, h2, x, gt, info_row, info_col, w_eg, w_eu, w_ed, x_sorted, tm, layer):
    g, t, d = x.shape
    n = g * t
    n_tiles = n // tm
    tpg = t // tm
    rm = gt.shape[1]
    er = _expert_rows(n)
    n_blocks = x_sorted.shape[0] // er
    s_loc = -(-(2 * tm + N_EXPERTS * (SEG - 1)) // GATHER_ROWS) * GATHER_ROWS
    upb = er // SEG
    reg_blk = (jnp.sum(cnt, axis=0) + upb - 1) // upb
    blk_end = jnp.cumsum(reg_blk)
    goff = ((blk_end - reg_blk)[None, :] * upb + jnp.cumsum(cnt, axis=0) - cnt).reshape(-1)
    loff = (jnp.cumsum(cnt, axis=1) - cnt).reshape(-1)
    cntf = cnt.reshape(-1)
    n_valid = blk_end[-1:]
    blk_expert = jnp.minimum(
        jnp.sum(jnp.arange(n_blocks, dtype=jnp.int32)[:, None] >= blk_end[None, :], axis=1), N_EXPERTS - 1
    ).astype(jnp.int32)

    x_sorted = pl.pallas_call(
        _sort_kernel,
        out_shape=jax.ShapeDtypeStruct(x_sorted.shape, BF16),
        grid_spec=pltpu.PrefetchScalarGridSpec(
            num_scalar_prefetch=3,
            grid=(n_tiles,),
            in_specs=[pl.BlockSpec((tm, d), lambda i, *_: (i, 0)),
                      pl.BlockSpec((8, tm), lambda i, *_: (0, i)),
                      pl.BlockSpec(memory_space=pl.ANY)],
            out_specs=pl.BlockSpec(memory_space=pl.ANY),
            scratch_shapes=[pltpu.VMEM((2, s_loc, d), BF16), pltpu.SemaphoreType.DMA((2,))],
        ),
        input_output_aliases={5: 0},
        compiler_params=_cparams(("arbitrary",)),
        name="moe_sort",
    )(cntf, loff, goff, h2, info_row, x_sorted)

    blk = lambda b, be, nv: (jnp.minimum(b, nv[0] - 1), 0)
    d_sorted = pl.pallas_call(
        _experts_kernel,
        out_shape=jax.ShapeDtypeStruct(x_sorted.shape, BF16),
        grid_spec=pltpu.PrefetchScalarGridSpec(
            num_scalar_prefetch=2,
            grid=(n_blocks,),
            in_specs=[pl.BlockSpec((er, d), blk),
                      pl.BlockSpec((None, 1, d, D_EXPERT), lambda b, be, nv: (layer, be[b], 0, 0)),
                      pl.BlockSpec((None, 1, d, D_EXPERT), lambda b, be, nv: (layer, be[b], 0, 0)),
                      pl.BlockSpec((None, 1, D_EXPERT, d), lambda b, be, nv: (layer, be[b], 0, 0))],
            out_specs=pl.BlockSpec((er, d), lambda b, be, nv: (b, 0)),
        ),
        compiler_params=_cparams(("arbitrary",), VMEM_LIMIT),
        name="moe_experts",
    )(blk_expert, n_valid, x_sorted, w_eg, w_eu, w_ed)

    gt_spec = pl.BlockSpec((1, rm, d), (lambda i, *_: (i // tpg, 0, 0)) if rm == 1
                           else (lambda i, *_: (i // tpg, i % tpg, 0)))
    tok3 = pl.BlockSpec((1, tm, d), lambda i, *_: (i // tpg, i % tpg, 0))
    out = pl.pallas_call(
        functools.partial(_combine_kernel, rb=min(tm, 256)),
        out_shape=jax.ShapeDtypeStruct((g, t, d), F32),
        grid_spec=pltpu.PrefetchScalarGridSpec(
            num_scalar_prefetch=3,
            grid=(n_tiles,),
            in_specs=[pl.BlockSpec(memory_space=pl.ANY),
                      pl.BlockSpec((tm, 8), lambda i, *_: (i, 0)),
                      tok3, gt_spec],
            out_specs=tok3,
            scratch_shapes=[pltpu.VMEM((2, s_loc, d), BF16), pltpu.SemaphoreType.DMA((2,))],
        ),
        compiler_params=_cparams(("arbitrary",), VMEM_LIMIT),
        name="moe_combine",
    )(cntf, loff, goff, d_sorted, info_col, x, gt)
    return out, x_sorted


def _rope_tables(pos):
    half = HEAD_DK // 2
    inv_freq = ROPE_BASE ** (-jnp.arange(half, dtype=F32) / half)
    ang = pos[:, None] * inv_freq[None, :]
    cos, sin = jnp.cos(ang), jnp.sin(ang)
    return jnp.concatenate([cos, cos], axis=1), jnp.concatenate([-sin, sin], axis=1)


def _layer(x, mods, lw, cfg, ret0, gla0, cache_k, cache_v, cosf, sinf, x_sorted):
    g, t, d = x.shape
    b, tseq = cfg["b"], cfg["tseq"]
    sh_m, sc_m, gt_m, sh_f, sc_f, gt_f = mods
    proj, log_a = _in_proj(x, sc_m, sh_m, lw["g_mix"], lw["w_in"], lw["w_ga"], lw["a2"], lw["a_bias"],
                           cosf, sinf, lw["gq"], lw["gk"], cfg["tm"], lw["layer"])
    proj_s = proj.reshape(b, tseq, PROJ_COLS)
    log_a_s = log_a.reshape(b, tseq, BRANCH_WIDTH)
    o_r, ret_new = _retention(proj_s, ret0, lw["g_ret_gn"], cfg["chunk"], cfg["tc"])
    o_g, gla_new = _gla(proj_s, log_a_s, gla0, lw["g_gla_gn"], cfg["chunk"], cfg["tc"])
    if cache_k is None:
        o_a = _band_attn_prompt(proj_s, lw["rel_bias"])
    else:
        o_a = _band_attn_sample(proj_s, cache_k, cache_v, lw["rel_bias"], lw["layer"])
    tok = lambda a: a.reshape(g, t, BRANCH_WIDTH)
    x1, h2 = _merge(x, tok(o_r), tok(o_g), tok(o_a), (sc_m, sh_m, gt_m, sc_f, sh_f), lw["g_mix"], lw["g_ffn"],
                    lw["w_gate"], lw["b_gate"], lw["w_branch"], lw["w_out"], cfg["tm"], lw["layer"])
    h2f = h2.reshape(g * t, d)
    info_row, tbl = _route(h2f, lw["wr_t"], lw["br_t"], cfg["tme"], SEG)
    x2, x_sorted = _moe(tbl[:, :, 0], h2f, x1, gt_f, info_row, info_row.T, lw["w_eg"], lw["w_eu"], lw["w_ed"],
                        x_sorted, cfg["tme"], lw["layer"])
    tail = proj_s[:, -min(tseq, BAND_PAST):]
    k_new = tail[:, :, 9 * BRANCH_WIDTH:10 * BRANCH_WIDTH]
    v_new = tail[:, :, 10 * BRANCH_WIDTH:11 * BRANCH_WIDTH]
    return x2, ret_new, gla_new, k_new, v_new, x_sorted


def kernel(x_prompt, x_sample, state_ret, state_gla, cache_att_k, cache_att_v, c_prompt, c_sample,
           w_ada, b_ada, g_mix, w_in, gla_a2, gla_a_bias, g_ret_gn, g_gla_gn, g_q_att, g_k_att, rel_bias,
           w_branch, w_gate, b_gate, w_out, g_ffn, w_router_group, b_router_group, w_router_exp,
           b_router_exp, w_exp_gate, w_exp_up, w_exp_down):
    depth = w_ada.shape[0]
    bp, seq, d = x_prompt.shape
    bs, dseq, _ = x_sample.shape
    n_s = bs * dseq

    pad = (-(bp + bs)) % 8
    c_all = jnp.concatenate([c_prompt, c_sample, jnp.zeros((pad, d), F32)], axis=0)
    mod = _ada_mod(c_all, w_ada, b_ada)

    ga0 = 8 * BRANCH_WIDTH
    w_in_b = jnp.concatenate([w_in[:, :, :ga0], w_in[:, :, ga0 + GLA_RANK:]], axis=2).astype(BF16)
    w_ga = jnp.pad(w_in[:, :, ga0:ga0 + GLA_RANK], ((0, 0), (0, 0), (0, V7X_LANES - GLA_RANK))).astype(BF16)
    a2 = jnp.pad(gla_a2, ((0, 0), (0, V7X_LANES - GLA_RANK), (0, 0))).astype(BF16)
    wr_t = jnp.zeros((depth, 32, d), F32)
    wr_t = wr_t.at[:, 0:N_GROUPS].set(jnp.swapaxes(w_router_group, 1, 2))
    wr_t = wr_t.at[:, 8:8 + N_EXPERTS].set(jnp.swapaxes(w_router_exp, 1, 2)).astype(BF16)
    br_t = jnp.zeros((depth, 32, 1), F32)
    br_t = br_t.at[:, 0:N_GROUPS, 0].set(b_router_group).at[:, 8:8 + N_EXPERTS, 0].set(b_router_exp)
    w_gate_b, w_branch_b, w_out_b = w_gate.astype(BF16), w_branch.astype(BF16), w_out.astype(BF16)
    w_eg, w_eu, w_ed = w_exp_gate, w_exp_up, w_exp_down

    cos_p, sin_p = _rope_tables(jnp.arange(seq, dtype=F32))
    cos_s, sin_s = _rope_tables(PAST_LEN + jnp.arange(dseq, dtype=F32))
    cos_s, sin_s = jnp.tile(cos_s, (bs, 1)), jnp.tile(sin_s, (bs, 1))

    cfg_p = dict(b=bp, tseq=seq, tm=512, chunk=4 * CHUNK, tc=512, tme=512)
    cfg_s = dict(b=bs, tseq=dseq, tm=n_s, chunk=min(dseq, CHUNK), tc=dseq, tme=n_s)
    zero_state = jnp.zeros((bp, RET_HEADS, HEAD_DK, HEAD_DK), F32)
    sorted_p = jnp.zeros((_sorted_rows(bp * seq, cfg_p["tme"]), d), BF16)
    sorted_s = jnp.zeros((_sorted_rows(n_s, cfg_s["tme"]), d), BF16)

    cache_kt = jnp.transpose(cache_att_k, (0, 1, 3, 4, 2))
    cache_vt = jnp.transpose(cache_att_v, (0, 1, 3, 4, 2))

    xp = x_prompt
    xs = x_sample.reshape(1, n_s, d)
    outs = [[] for _ in range(8)]
    for l in range(depth):
        lw = dict(
            layer=l, g_mix=g_mix[l][None], g_ffn=g_ffn[l][None], w_in=w_in_b, w_ga=w_ga[l], a2=a2[l],
            a_bias=gla_a_bias[l][None], gq=jnp.tile(g_q_att[l], ATT_HEADS)[None],
            gk=jnp.tile(g_k_att[l], ATT_HEADS)[None], g_ret_gn=g_ret_gn[l][None], g_gla_gn=g_gla_gn[l][None],
            rel_bias=rel_bias[l], w_gate=w_gate_b, b_gate=b_gate[l][None], w_branch=w_branch_b,
            w_out=w_out_b, wr_t=wr_t[l], br_t=br_t[l], w_eg=w_eg, w_eu=w_eu, w_ed=w_ed)
        mods_p = tuple(m[:, None, :] for m in jnp.split(mod[l, :bp], 6, axis=-1))
        mods_s = tuple(jnp.repeat(m, dseq, axis=0)[None] for m in jnp.split(mod[l, bp:bp + bs], 6, axis=-1))
        xp, rp, gp, kp, vp, sorted_p = _layer(xp, mods_p, lw, cfg_p, zero_state, zero_state, None, None,
                                              cos_p, sin_p, sorted_p)
        xs, rs, gs, ks, vs, sorted_s = _layer(xs, mods_s, lw, cfg_s, state_ret[l], state_gla[l], cache_kt, cache_vt,
                                              cos_s, sin_s, sorted_s)
        heads = lambda a: a.astype(F32).reshape(a.shape[0], a.shape[1], ATT_HEADS, ATT_HD)
        for lst, val in zip(outs, (rp, gp, heads(kp), heads(vp), rs, gs, heads(ks), heads(vs))):
            lst.append(val)
    return (xp, xs.reshape(bs, dseq, d)) + tuple(jnp.stack(o) for o in outs)
```

```python
import functools

import numpy as np
import jax
import jax.numpy as jnp
from jax import lax
from jax.experimental import pallas as pl
from jax.experimental.pallas import tpu as pltpu

F32 = jnp.float32
BF16 = jnp.bfloat16

D_MODEL = 1024
CHUNK = 64
BRANCH_WIDTH = 512
N_BRANCH = 3
RET_HEADS = 4
HEAD_DK = 128
GLA_RANK = 16
GLA_TAU = 16.0
ATT_HEADS = 8
ATT_HD = 64
BAND_PAST = 512
MAX_REL = 128
N_GROUPS = 4
EXPERTS_PER_GROUP = 4
N_EXPERTS = 16
D_EXPERT = 512
ROPE_BASE = 10000.0
EPS = 1e-6
GN_EPS = 1e-5
PAST_LEN = 1024
N_PROJ_BLOCKS = 11
PROJ_COLS = N_PROJ_BLOCKS * BRANCH_WIDTH

V7X_LANES = 128
V7X_VMEM_BYTES = 64 * 1024 * 1024
VMEM_LIMIT = 56 * 1024 * 1024
NEG_BIG = -1e30
LOG2_E = 1.4426950408889634


def _cparams(sem, vmem=None):
    return pltpu.CompilerParams(dimension_semantics=sem, vmem_limit_bytes=vmem)


def _const_spec(shape):
    nd = len(shape)
    return pl.BlockSpec(shape, lambda *_: (0,) * nd, pipeline_mode=pl.Buffered(1))


def _layer_spec(shape, layer):
    nd = len(shape)
    return pl.BlockSpec((None,) + tuple(shape), lambda *_: (layer,) + (0,) * nd, pipeline_mode=pl.Buffered(1))


def _dot(a, b):
    return jnp.dot(a, b, preferred_element_type=F32)


def _dot_nt(a, b):
    return lax.dot_general(a, b, (((1,), (1,)), ((), ())), preferred_element_type=F32)


def _dot_tn(a, b):
    return lax.dot_general(a, b, (((0,), (0,)), ((), ())), preferred_element_type=F32)


def _silu(x):
    return x * jax.nn.sigmoid(x)


def _norm_mod(x, g, sc, sh):
    ms = jnp.mean(x * x, axis=-1, keepdims=True)
    return x * lax.rsqrt(ms + EPS) * g * (1.0 + sc) + sh


def _ada_kernel(c_ref, w_ref, b_ref, o_ref):
    s = _silu(c_ref[...])
    o_ref[0] = _dot(s.astype(BF16), w_ref[0].astype(BF16)) + b_ref[0]


def _ada_mod(c_all, w_ada, b_ada):
    depth, d, n = w_ada.shape
    rows = c_all.shape[0]
    bn = 1536
    return pl.pallas_call(
        _ada_kernel,
        out_shape=jax.ShapeDtypeStruct((depth, rows, n), F32),
        grid=(depth, n // bn),
        in_specs=[
            pl.BlockSpec((rows, d), lambda l, j: (0, 0)),
            pl.BlockSpec((1, d, bn), lambda l, j: (l, 0, j)),
            pl.BlockSpec((1, 1, bn), lambda l, j: (l, 0, j)),
        ],
        out_specs=pl.BlockSpec((1, rows, bn), lambda l, j: (l, 0, j)),
        compiler_params=_cparams(("arbitrary", "arbitrary"), VMEM_LIMIT),
        name="ada_mod",
    )(c_all, w_ada, b_ada.reshape(depth, 1, n))


def _rope_heads(a, cosf, sinf):
    outs = []
    for h in range(RET_HEADS):
        ah = a[:, h * HEAD_DK:(h + 1) * HEAD_DK]
        outs.append(ah * cosf + pltpu.roll(ah, HEAD_DK // 2, 1) * sinf)
    return jnp.concatenate(outs, axis=1)


def _rms_heads64(a, gain):
    low = lax.broadcasted_iota(jnp.int32, (1, V7X_LANES), 1) < ATT_HD
    outs = []
    for c in range(a.shape[1] // V7X_LANES):
        ac = a[:, c * V7X_LANES:(c + 1) * V7X_LANES]
        sq = ac * ac
        lo = jnp.sum(jnp.where(low, sq, 0.0), axis=-1, keepdims=True)
        hi = jnp.sum(jnp.where(low, 0.0, sq), axis=-1, keepdims=True)
        ms = jnp.where(low, lo, hi) * (1.0 / ATT_HD)
        outs.append(ac * lax.rsqrt(ms + EPS))
    return jnp.concatenate(outs, axis=1) * gain


def _in_proj_kernel(x_ref, sc_ref, sh_ref, g_ref, w_ref, wga_ref, a2_ref, ab_ref, cos_ref, sin_ref,
                    gq_ref, gk_ref, proj_ref, la_ref):
    x = x_ref[0]
    hb = _norm_mod(x, g_ref[...], sc_ref[0], sh_ref[0]).astype(BF16)
    cosf = cos_ref[...]
    sinf = sin_ref[...]
    ga = _dot(hb, wga_ref[...])
    for j in range(N_PROJ_BLOCKS):
        if j == N_PROJ_BLOCKS // 2:
            z = _dot(ga.astype(BF16), a2_ref[...]) + ab_ref[...]
            la_ref[0] = jax.nn.log_sigmoid(z) * (1.0 / GLA_TAU)
        cols = slice(j * BRANCH_WIDTH, (j + 1) * BRANCH_WIDTH)
        acc = _dot(hb, w_ref[:, cols])
        if j == 0:
            acc = _rope_heads(acc, cosf, sinf)
        elif j == 1:
            acc = _rope_heads(acc, cosf, sinf) * (HEAD_DK ** -0.5)
        elif j == 4:
            acc = acc * (HEAD_DK ** -0.5)
        elif j == 8:
            acc = _rms_heads64(acc, gq_ref[...]) * (ATT_HD ** -0.5 * LOG2_E)
        elif j == 9:
            acc = _rms_heads64(acc, gk_ref[...])
        proj_ref[0, :, cols] = acc.astype(BF16)


def _in_proj(x, sc, sh, g_mix, w_in, w_ga, a2, a_bias, cosf, sinf, gq, gk, tm, layer):
    g, t, d = x.shape
    rm = sc.shape[1]
    mod_spec = pl.BlockSpec((1, rm, d), (lambda b, i: (b, 0, 0)) if rm == 1 else (lambda b, i: (b, i, 0)))
    return pl.pallas_call(
        _in_proj_kernel,
        out_shape=(jax.ShapeDtypeStruct((g, t, PROJ_COLS), BF16),
                   jax.ShapeDtypeStruct((g, t, BRANCH_WIDTH), F32)),
        grid=(g, t // tm),
        in_specs=[
            pl.BlockSpec((1, tm, d), lambda b, i: (b, i, 0)),
            mod_spec, mod_spec,
            _const_spec((1, d)),
            _layer_spec((d, PROJ_COLS), layer),
            _const_spec((d, V7X_LANES)),
            _const_spec((V7X_LANES, BRANCH_WIDTH)),
            _const_spec((1, BRANCH_WIDTH)),
            pl.BlockSpec((tm, HEAD_DK), lambda b, i: (i, 0)),
            pl.BlockSpec((tm, HEAD_DK), lambda b, i: (i, 0)),
            _const_spec((1, BRANCH_WIDTH)),
            _const_spec((1, BRANCH_WIDTH)),
        ],
        out_specs=(pl.BlockSpec((1, tm, PROJ_COLS), lambda b, i: (b, i, 0)),
                   pl.BlockSpec((1, tm, BRANCH_WIDTH), lambda b, i: (b, i, 0))),
        compiler_params=_cparams(("arbitrary", "arbitrary"), VMEM_LIMIT),
        name="in_proj",
    )(x, sc, sh, g_mix, w_in, w_ga, a2, a_bias, cosf, sinf, gq, gk)


def _retention_kernel(q_ref, k_ref, v_ref, g_ref, s0_ref, dm_ref, qd_ref, kd_ref, bd_ref, gn_ref,
                      o_ref, so_ref, st_ref, *, chunk, n_chunks):
    n_seq = q_ref.shape[0]

    @pl.when(pl.program_id(1) == 0)
    def _():
        st_ref[...] = s0_ref[...]

    def body(c, carry):
        rows = pl.ds(pl.multiple_of(c * chunk, chunk), chunk)
        for b in range(n_seq):
            for h in range(RET_HEADS):
                cols = slice(h * HEAD_DK, (h + 1) * HEAD_DK)
                q = q_ref[b, rows, cols]
                k = k_ref[b, rows, cols]
                v = v_ref[b, rows, cols]
                scores = _dot_nt(q, k) * dm_ref[h]
                o = _dot(scores.astype(BF16), v)
                o = o + _dot((q.astype(F32) * qd_ref[h]).astype(BF16), st_ref[b, h].astype(BF16))
                st_ref[b, h] = (st_ref[b, h] * bd_ref[h]
                                + _dot_tn((k.astype(F32) * kd_ref[h]).astype(BF16), v))
                mu = jnp.mean(o, axis=-1, keepdims=True)
                oc = o - mu
                var = jnp.mean(oc * oc, axis=-1, keepdims=True)
                on = oc * lax.rsqrt(var + GN_EPS) * gn_ref[:, cols]
                o_ref[b, rows, cols] = (_silu(g_ref[b, rows, cols].astype(F32)) * on).astype(BF16)
        return carry

    lax.fori_loop(0, n_chunks, body, 0)

    @pl.when(pl.program_id(1) == pl.num_programs(1) - 1)
    def _():
        so_ref[...] = st_ref[...]


def _retention_tables(chunk):
    log_gamma = jnp.log(1.0 - jnp.exp2(-5.0 - jnp.arange(RET_HEADS, dtype=F32)))
    idx = jnp.arange(chunk, dtype=F32)
    diff = idx[:, None] - idx[None, :]
    dmask = jnp.where(diff >= 0, jnp.exp(log_gamma[:, None, None] * jnp.maximum(diff, 0.0)), 0.0)
    qd = jnp.exp(log_gamma[:, None] * (idx + 1.0))[:, :, None]
    kd = jnp.exp(log_gamma[:, None] * (chunk - 1.0 - idx))[:, :, None]
    bd = jnp.exp(log_gamma * chunk)[:, None, None]
    bc = lambda a, r: jnp.broadcast_to(a, (RET_HEADS, r, HEAD_DK)).astype(F32)
    return dmask.astype(F32), bc(qd, chunk), bc(kd, chunk), bc(bd, 1)


def _seqs_per_step(b):
    return 4 if b % 4 == 0 and b > 4 else (2 if b % 2 == 0 else 1)


def _retention(proj, state0, gn, chunk, tc):
    b, t, _ = proj.shape
    nb = _seqs_per_step(b)
    dm, qd, kd, bd = _retention_tables(chunk)
    col = lambda j: pl.BlockSpec((nb, tc, BRANCH_WIDTH), lambda i, s, j=j: (i, s, j))
    st_spec = pl.BlockSpec((nb, RET_HEADS, HEAD_DK, HEAD_DK), lambda i, s: (i, 0, 0, 0))
    return pl.pallas_call(
        functools.partial(_retention_kernel, chunk=chunk, n_chunks=tc // chunk),
        out_shape=(jax.ShapeDtypeStruct((b, t, BRANCH_WIDTH), BF16),
                   jax.ShapeDtypeStruct((b, RET_HEADS, HEAD_DK, HEAD_DK), F32)),
        grid=(b // nb, t // tc),
        in_specs=[col(0), col(1), col(2), col(3), st_spec,
                  _const_spec(dm.shape), _const_spec(qd.shape), _const_spec(kd.shape), _const_spec(bd.shape),
                  _const_spec((1, BRANCH_WIDTH))],
        out_specs=(pl.BlockSpec((nb, tc, BRANCH_WIDTH), lambda i, s: (i, s, 0)), st_spec),
        scratch_shapes=[pltpu.VMEM((nb, RET_HEADS, HEAD_DK, HEAD_DK), F32)],
        compiler_params=_cparams(("arbitrary", "arbitrary")),
        name="retention",
    )(proj, proj, proj, proj, state0, dm, qd, kd, bd, gn)


GLA_SAFE_LOG_DECAY = -60.0


def _gla_intra_direct(q, bh, rows_ref, chunk):
    trow = lax.broadcasted_iota(jnp.int32, (chunk, 1), 0)

    def step(s, acc):
        bs = rows_ref[0, pl.ds(s, 1), :]
        ks = rows_ref[1, pl.ds(s, 1), :]
        vs = rows_ref[2, pl.ds(s, 1), :]
        e = jnp.exp(jnp.minimum(bh - bs, 0.0))
        col = jnp.sum(q * ks * e, axis=-1, keepdims=True)
        return acc + jnp.where(trow >= s, col, 0.0) * vs

    return lax.fori_loop(0, chunk, step, jnp.zeros((chunk, HEAD_DK), F32))


def _gla_kernel(q_ref, k_ref, v_ref, r_ref, la_ref, s0_ref, tri_ref, gn_ref, o_ref, so_ref, st_ref, rows_ref,
                *, chunk, n_chunks):
    n_seq = q_ref.shape[0]

    @pl.when(pl.program_id(1) == 0)
    def _():
        for b in range(n_seq):
            for h in range(RET_HEADS):
                st_ref[b, h] = s0_ref[b, h].T

    tri = tri_ref[...]
    causal = (lax.broadcasted_iota(jnp.int32, (chunk, chunk), 0)
              >= lax.broadcasted_iota(jnp.int32, (chunk, chunk), 1))

    def chunk_body(factored):
        def body(c, carry):
            rows = pl.ds(pl.multiple_of(c * chunk, chunk), chunk)
            for b in range(n_seq):
                la = la_ref[b, rows, :]
                la_hi = la.astype(BF16)
                la_lo = (la - la_hi.astype(F32)).astype(BF16)
                b_all = _dot(tri, la_hi) + _dot(tri, la_lo)
                for h in range(RET_HEADS):
                    cols = slice(h * HEAD_DK, (h + 1) * HEAD_DK)
                    q = q_ref[b, rows, cols].astype(F32)
                    k = k_ref[b, rows, cols].astype(F32)
                    v = v_ref[b, rows, cols]
                    bh = b_all[:, cols]
                    bl = bh[chunk - 1:chunk, :]
                    qh = (q * jnp.exp(bh)).astype(BF16)
                    if factored:
                        kh = (k * jnp.exp(-bh)).astype(BF16)
                        scores = jnp.where(causal, _dot_nt(qh, kh), 0.0)
                        o = _dot(scores.astype(BF16), v)
                    else:
                        rows_ref[0] = bh
                        rows_ref[1] = k
                        rows_ref[2] = v.astype(F32)
                        o = _gla_intra_direct(q, bh, rows_ref, chunk)
                    o = o + _dot_nt(qh, st_ref[b, h].astype(BF16))
                    kd = (k * jnp.exp(bl - bh)).astype(BF16)
                    st_ref[b, h] = st_ref[b, h] * jnp.exp(bl) + _dot_tn(v, kd)
                    ms = jnp.mean(o * o, axis=-1, keepdims=True)
                    on = o * lax.rsqrt(ms + EPS) * gn_ref[:, cols]
                    o_ref[b, rows, cols] = (_silu(r_ref[b, rows, cols].astype(F32)) * on).astype(BF16)
            return carry
        return body

    lowest = None
    for b in range(n_seq):
        for c in range(n_chunks):
            tot = jnp.sum(la_ref[b, c * chunk:(c + 1) * chunk, :], axis=0, keepdims=True)
            lowest = tot if lowest is None else jnp.minimum(lowest, tot)
    safe = jnp.min(lowest) > GLA_SAFE_LOG_DECAY

    @pl.when(safe)
    def _():
        lax.fori_loop(0, n_chunks, chunk_body(True), 0)

    @pl.when(jnp.logical_not(safe))
    def _():
        lax.fori_loop(0, n_chunks, chunk_body(False), 0)

    @pl.when(pl.program_id(1) == pl.num_programs(1) - 1)
    def _():
        for b in range(n_seq):
            for h in range(RET_HEADS):
                so_ref[b, h] = st_ref[b, h].T


def _gla(proj, log_a, state0, gn, chunk, tc):
    b, t, _ = proj.shape
    nb = _seqs_per_step(b)
    tri = jnp.tril(jnp.ones((chunk, chunk), F32)).astype(BF16)
    col = lambda j: pl.BlockSpec((nb, tc, BRANCH_WIDTH), lambda i, s, j=j: (i, s, j))
    st_spec = pl.BlockSpec((nb, RET_HEADS, HEAD_DK, HEAD_DK), lambda i, s: (i, 0, 0, 0))
    return pl.pallas_call(
        functools.partial(_gla_kernel, chunk=chunk, n_chunks=tc // chunk),
        out_shape=(jax.ShapeDtypeStruct((b, t, BRANCH_WIDTH), BF16),
                   jax.ShapeDtypeStruct((b, RET_HEADS, HEAD_DK, HEAD_DK), F32)),
        grid=(b // nb, t // tc),
        in_specs=[col(4), col(5), col(6), col(7),
                  pl.BlockSpec((nb, tc, BRANCH_WIDTH), lambda i, s: (i, s, 0)),
                  st_spec, _const_spec((chunk, chunk)), _const_spec((1, BRANCH_WIDTH))],
        out_specs=(pl.BlockSpec((nb, tc, BRANCH_WIDTH), lambda i, s: (i, s, 0)), st_spec),
        scratch_shapes=[pltpu.VMEM((nb, RET_HEADS, HEAD_DK, HEAD_DK), F32),
                        pltpu.VMEM((3, chunk, HEAD_DK), F32)],
        compiler_params=_cparams(("arbitrary", "arbitrary")),
        name="gla",
    )(proj, proj, proj, proj, log_a, state0, tri, gn)


def _attn_stage_scratch(rb, win):
    pairs = ATT_HEADS // 2
    return [pltpu.VMEM((2, pairs, 2 * rb, win), F32), pltpu.VMEM((2, pairs, 2 * rb, win), BF16)]


def _band_attn_kernel(q_ref, kp_ref, kc_ref, vp_ref, vc_ref, bias_ref, o_ref, kw_ref, vw_ref,
                      s_ref, e_ref,
                      *, n_sub, rb, win, prev_rows, mask_start):
    cur_rows = kc_ref.shape[1]
    kw_ref[0:prev_rows, :] = kp_ref[0].astype(BF16)
    kw_ref[prev_rows:prev_rows + cur_rows, :] = kc_ref[0]
    pairs = ATT_HEADS // 2
    for p in range(pairs):
        cols = slice(p * V7X_LANES, (p + 1) * V7X_LANES)
        vw_ref[0:prev_rows, 2 * p * V7X_LANES:(2 * p + 1) * V7X_LANES] = vp_ref[0, :, cols].astype(BF16)
        vw_ref[prev_rows:prev_rows + cur_rows, 2 * p * V7X_LANES:(2 * p + 1) * V7X_LANES] = vc_ref[0, :, cols]
        vw_ref[:, (2 * p + 1) * V7X_LANES:(2 * p + 2) * V7X_LANES] = jnp.ones((vw_ref.shape[0], V7X_LANES), BF16)
    low = lax.broadcasted_iota(jnp.int32, (1, V7X_LANES), 1) < ATT_HD
    q0 = pl.program_id(1) * cur_rows

    def scores(i, masked):
        if masked:
            valid = lax.broadcasted_iota(jnp.int32, (1, win), 1) >= prev_rows - q0 - i * rb
        for p in range(pairs):
            cols = slice(p * V7X_LANES, (p + 1) * V7X_LANES)
            q2 = q_ref[0, i * rb:(i + 1) * rb, cols]
            zero = jnp.zeros_like(q2)
            qst = jnp.concatenate([jnp.where(low, q2, zero), jnp.where(low, zero, q2)], axis=0)
            s = _dot_nt(qst, kw_ref[i * rb:i * rb + win, cols]) + bias_ref[p]
            if masked:
                s = jnp.where(valid, s, NEG_BIG)
            s_ref[i % 2, p] = s

    def softmax(i):
        for p in range(pairs):
            s = s_ref[i % 2, p]
            e_ref[i % 2, p] = jnp.exp2(s - jnp.max(s, axis=-1, keepdims=True)).astype(BF16)

    def values(i):
        for p in range(pairs):
            cols = slice(p * V7X_LANES, (p + 1) * V7X_LANES)
            od = _dot(e_ref[i % 2, p], vw_ref[i * rb:i * rb + win, 2 * p * V7X_LANES:2 * (p + 1) * V7X_LANES])
            o2 = od[:, :V7X_LANES] / od[:, V7X_LANES:]
            o_ref[0, i * rb:(i + 1) * rb, cols] = jnp.where(low, o2[:rb], o2[rb:]).astype(BF16)

    def pipeline(masked):
        for step in range(n_sub + 2):
            if step < n_sub:
                scores(step, masked)
            if 0 <= step - 1 < n_sub:
                softmax(step - 1)
            if 0 <= step - 2 < n_sub:
                values(step - 2)

    if mask_start:
        pl.when(q0 < prev_rows)(functools.partial(pipeline, True))
        pl.when(q0 >= prev_rows)(functools.partial(pipeline, False))
    else:
        pipeline(False)


def _rel_bias_table(rel_bias, rb, win, chunk):
    period = win + rb
    u = np.arange(period)
    u = np.where(u < win, u, u - period)
    vec = rel_bias.astype(F32)[:, np.clip(BAND_PAST - u, -MAX_REL, MAX_REL) + MAX_REL]
    bias = jnp.tile(vec, (1, rb))[:, :rb * (period - 1)].reshape(-1, rb, period - 1)[:, :, :win]
    r = np.arange(rb)[:, None]
    j = np.arange(win)[None, :]
    c0 = (r // chunk) * chunk
    allowed = (j >= c0) & (j < c0 + BAND_PAST + chunk)
    bias = jnp.where(jnp.asarray(allowed)[None], bias * LOG2_E, NEG_BIG)
    return bias.reshape(ATT_HEADS // 2, 2 * rb, win)


def _band_attn_prompt(proj, rel_bias):
    b, t, _ = proj.shape
    qb, rb = 512, 128
    win = BAND_PAST + rb
    bias = _rel_bias_table(rel_bias, rb, win, CHUNK)
    cur = lambda j: pl.BlockSpec((1, qb, BRANCH_WIDTH), lambda i, s, j=j: (i, s, j))
    prev = lambda j: pl.BlockSpec((1, qb, BRANCH_WIDTH), lambda i, s, j=j: (i, jnp.maximum(s - 1, 0), j))
    return pl.pallas_call(
        functools.partial(_band_attn_kernel, n_sub=qb // rb, rb=rb, win=win, prev_rows=BAND_PAST,
                          mask_start=True),
        out_shape=jax.ShapeDtypeStruct((b, t, BRANCH_WIDTH), BF16),
        grid=(b, t // qb),
        in_specs=[cur(8), prev(9), cur(9), prev(10), cur(10), _const_spec(bias.shape)],
        out_specs=pl.BlockSpec((1, qb, BRANCH_WIDTH), lambda i, s: (i, s, 0)),
        scratch_shapes=[pltpu.VMEM((BAND_PAST + qb, BRANCH_WIDTH), BF16),
                        pltpu.VMEM((BAND_PAST + qb, 2 * BRANCH_WIDTH), BF16)] + _attn_stage_scratch(rb, win),
        compiler_params=_cparams(("arbitrary", "arbitrary")),
        name="band_attn",
    )(proj, proj, proj, proj, proj, bias)


def _band_attn_sample_kernel(q_ref, kn_ref, vn_ref, kc_ref, vc_ref, bias_ref, o_ref):
    t = q_ref.shape[1]
    past = kc_ref.shape[-1]
    low = lax.broadcasted_iota(jnp.int32, (1, V7X_LANES), 1) < ATT_HD
    for p in range(ATT_HEADS // 2):
        cols = slice(p * V7X_LANES, (p + 1) * V7X_LANES)
        q2 = q_ref[0, :, cols]
        zero = jnp.zeros_like(q2)
        qst = jnp.concatenate([jnp.where(low, q2, zero), jnp.where(low, zero, q2)], axis=0)
        kt = kc_ref[0, 2 * p:2 * p + 2].reshape(V7X_LANES, past).astype(BF16)
        vt = vc_ref[0, 2 * p:2 * p + 2].reshape(V7X_LANES, past).astype(BF16)
        s = jnp.concatenate([_dot(qst, kt), _dot_nt(qst, kn_ref[0, :, cols])], axis=1) + bias_ref[p]
        m = jnp.max(s, axis=-1, keepdims=True)
        e = jnp.exp2(s - m)
        den = jnp.sum(e, axis=-1, keepdims=True)
        eb = e.astype(BF16)
        o2 = (_dot_nt(eb[:, :past], vt) + _dot(eb[:, past:], vn_ref[0, :, cols])) / den
        o_ref[0, :, cols] = jnp.where(low, o2[:t], o2[t:]).astype(BF16)


def _band_attn_sample(proj, cache_kt, cache_vt, rel_bias, layer):
    b, t, _ = proj.shape
    win = BAND_PAST + t
    bias = _rel_bias_table(rel_bias, t, win, t)
    cur = lambda j: pl.BlockSpec((1, t, BRANCH_WIDTH), lambda i, j=j: (i, 0, j))
    cache = pl.BlockSpec((None, 1, ATT_HEADS, ATT_HD, BAND_PAST), lambda i: (layer, i, 0, 0, 0))
    return pl.pallas_call(
        _band_attn_sample_kernel,
        out_shape=jax.ShapeDtypeStruct((b, t, BRANCH_WIDTH), BF16),
        grid=(b,),
        in_specs=[cur(8), cur(9), cur(10), cache, cache, _const_spec(bias.shape)],
        out_specs=pl.BlockSpec((1, t, BRANCH_WIDTH), lambda i: (i, 0, 0)),
        compiler_params=_cparams(("arbitrary",)),
        name="band_attn_sample",
    )(proj, proj, proj, cache_kt, cache_vt, bias)


def _merge_kernel(x_ref, or_ref, og_ref, oa_ref, scm_ref, shm_ref, gtm_ref, scf_ref, shf_ref,
                  gmix_ref, gffn_ref, wg_ref, bg_ref, wb_ref, wo_ref, xo_ref, h2_ref):
    x = x_ref[0]
    d = x.shape[1]
    hb = _norm_mod(x, gmix_ref[...], scm_ref[0], shm_ref[0]).astype(BF16)
    merged = None
    for n, o_ref in enumerate((or_ref, og_ref, oa_ref)):
        gate = jax.nn.sigmoid(_dot(hb, wg_ref[:, n * d:(n + 1) * d]) + bg_ref[:, n * d:(n + 1) * d])
        y = gate * _dot(o_ref[0], wb_ref[n])
        merged = y if merged is None else merged + y
    mix = _dot(merged.astype(BF16), wo_ref[...])
    xn = x + gtm_ref[0] * mix
    xo_ref[0] = xn
    h2_ref[0] = _norm_mod(xn, gffn_ref[...], scf_ref[0], shf_ref[0]).astype(BF16)


def _merge(x, o_r, o_g, o_a, mods, g_mix, g_ffn, w_gate, b_gate, w_branch, w_out, tm, layer):
    g, t, d = x.shape
    rm = mods[0].shape[1]
    mod_spec = pl.BlockSpec((1, rm, d), (lambda b, i: (b, 0, 0)) if rm == 1 else (lambda b, i: (b, i, 0)))
    tok = lambda w: pl.BlockSpec((1, tm, w), lambda b, i: (b, i, 0))
    return pl.pallas_call(
        _merge_kernel,
        out_shape=(jax.ShapeDtypeStruct((g, t, d), F32), jax.ShapeDtypeStruct((g, t, d), BF16)),
        grid=(g, t // tm),
        in_specs=[tok(d), tok(BRANCH_WIDTH), tok(BRANCH_WIDTH), tok(BRANCH_WIDTH)] + [mod_spec] * 5 + [
            _const_spec((1, d)), _const_spec((1, d)),
            _layer_spec((d, N_BRANCH * d), layer), _const_spec((1, N_BRANCH * d)),
            _layer_spec((N_BRANCH, BRANCH_WIDTH, d), layer), _layer_spec((d, d), layer)],
        out_specs=(tok(d), tok(d)),
        compiler_params=_cparams(("arbitrary", "arbitrary"), VMEM_LIMIT),
        name="merge",
    )(x, o_r, o_g, o_a, *mods, g_mix, g_ffn, w_gate, b_gate, w_branch, w_out)


def _first_argmax(vals, n):
    row = lax.broadcasted_iota(jnp.int32, vals.shape, 0).astype(F32)
    m = jnp.max(vals, axis=0, keepdims=True)
    idx = jnp.min(jnp.where(vals == m, row, float(n)), axis=0, keepdims=True)
    return m, idx


def _route_kernel(h_ref, wr_ref, br_ref, up_ref, l16_ref, info_ref, tbl_ref, *, rows_per_block):
    tm = h_ref.shape[0]
    lg = _dot_nt(wr_ref[...], h_ref[...]) + br_ref[...]
    g = lg[0:N_GROUPS]
    gmax, grp = _first_argmax(g, N_GROUPS)
    p_group = 1.0 / jnp.sum(jnp.exp(g - gmax), axis=0, keepdims=True)
    esel = jnp.zeros((EXPERTS_PER_GROUP, tm), F32)
    for gi in range(N_GROUPS):
        blk = lg[8 + gi * EXPERTS_PER_GROUP:8 + (gi + 1) * EXPERTS_PER_GROUP]
        esel = esel + jnp.where(grp == float(gi), blk, 0.0)
    v1, i1 = _first_argmax(esel, EXPERTS_PER_GROUP)
    row4 = lax.broadcasted_iota(jnp.int32, esel.shape, 0).astype(F32)
    v2, i2 = _first_argmax(jnp.where(row4 == i1, -jnp.inf, esel), EXPERTS_PER_GROUP)
    e21 = jnp.exp(v2 - v1)
    w1 = p_group / (1.0 + e21)
    w2 = p_group * e21 / (1.0 + e21)
    e1 = grp * float(EXPERTS_PER_GROUP) + i1
    e2 = grp * float(EXPERTS_PER_GROUP) + i2
    row16 = lax.broadcasted_iota(jnp.int32, (N_EXPERTS, tm), 0).astype(F32)
    hit1 = row16 == e1
    hit2 = row16 == e2
    onehot = jnp.where(hit1 | hit2, 1.0, 0.0)
    prefix = _dot(onehot.astype(BF16), up_ref[...])
    cnt = jnp.sum(onehot, axis=1, keepdims=True)
    nblk = jnp.floor((cnt + float(rows_per_block - 1)) * (1.0 / rows_per_block))
    nblk_b = jnp.broadcast_to(nblk, (N_EXPERTS, V7X_LANES))
    offb = _dot(l16_ref[...], nblk_b.astype(BF16))[:, 0:1]
    base = offb * float(rows_per_block) + prefix
    info_ref[0:1, :] = jnp.sum(jnp.where(hit1, base, 0.0), axis=0, keepdims=True)
    info_ref[1:2, :] = jnp.sum(jnp.where(hit2, base, 0.0), axis=0, keepdims=True)
    info_ref[2:3, :] = w1
    info_ref[3:4, :] = w2
    info_ref[4:8, :] = jnp.zeros((4, tm), F32)
    tbl_ref[0] = nblk_b.astype(jnp.int32)


def _route(h2, wr_t, br_t, tm, rows_per_block):
    n, d = h2.shape
    nt = n // tm
    upper = jnp.triu(jnp.ones((tm, tm), F32), 1).astype(BF16)
    l16 = jnp.tril(jnp.ones((N_EXPERTS, N_EXPERTS), F32), -1).astype(BF16)
    return pl.pallas_call(
        functools.partial(_route_kernel, rows_per_block=rows_per_block),
        out_shape=(jax.ShapeDtypeStruct((8, n), F32),
                   jax.ShapeDtypeStruct((nt, N_EXPERTS, V7X_LANES), jnp.int32)),
        grid=(nt,),
        in_specs=[pl.BlockSpec((tm, d), lambda i: (i, 0)),
                  _const_spec((32, d)), _const_spec((32, 1)),
                  _const_spec((tm, tm)), _const_spec((N_EXPERTS, N_EXPERTS))],
        out_specs=(pl.BlockSpec((8, tm), lambda i: (0, i)),
                   pl.BlockSpec((1, N_EXPERTS, V7X_LANES), lambda i: (i, 0, 0))),
        compiler_params=_cparams(("arbitrary",)),
        name="route",
    )(h2, wr_t, br_t, upper, l16)


SEG = 16
GATHER_ROWS = 256
MAX_EXPERT_ROWS = 512


def _expert_rows(n_tokens):
    mean_rows = 2 * n_tokens // N_EXPERTS
    return int(min(MAX_EXPERT_ROWS, max(4 * SEG, 1 << (mean_rows // 4).bit_length())))


def _segment_copies(cnt_ref, loff_ref, goff_ref, tile, local_ref, global_ref, sem, to_global):
    for e in range(N_EXPERTS):
        cnt = cnt_ref[tile * N_EXPERTS + e]
        rows = pl.multiple_of(cnt * SEG, SEG)
        loc = local_ref.at[pl.ds(pl.multiple_of(loff_ref[tile * N_EXPERTS + e] * SEG, SEG), rows)]
        glo = global_ref.at[pl.ds(pl.multiple_of(goff_ref[tile * N_EXPERTS + e] * SEG, SEG), rows)]
        cp = pltpu.make_async_copy(loc, glo, sem) if to_global else pltpu.make_async_copy(glo, loc, sem)
        yield cnt > 0, cp


def _tile_units(cnt_ref, tile):
    total = jnp.int32(0)
    for e in range(N_EXPERTS):
        total = total + cnt_ref[tile * N_EXPERTS + e]
    return total


def _wait_segments(cnt_ref, tile, local_ref, global_ref, sem, to_global):
    rows = pl.multiple_of(_tile_units(cnt_ref, tile) * SEG, SEG)
    loc = local_ref.at[pl.ds(0, rows)]
    glo = global_ref.at[pl.ds(0, rows)]
    cp = pltpu.make_async_copy(loc, glo, sem) if to_global else pltpu.make_async_copy(glo, loc, sem)
    pl.when(rows > 0)(cp.wait)


def _sort_kernel(cnt_ref, loff_ref, goff_ref, h_ref, irow_ref, xin_ref, xout_ref, xs2_ref, sem2):
    del xin_ref
    i = pl.program_id(0)
    last = pl.num_programs(0) - 1
    slot = lax.rem(i, 2)
    xs_ref = xs2_ref.at[slot]
    sem = sem2.at[slot]

    def wait_tile(tile, s):
        _wait_segments(cnt_ref, tile, xs2_ref.at[s], xout_ref, sem2.at[s], True)

    @pl.when(i >= 2)
    def _():
        wait_tile(i - 2, slot)

    gr = GATHER_ROWS
    n_gather = (_tile_units(cnt_ref, i) * SEG + gr - 1) // gr
    pos1 = irow_ref[0:1, :]
    pos2 = irow_ref[1:2, :]
    h = h_ref[...]

    def gather(gb, carry):
        r0 = pl.multiple_of(gb * gr, gr)
        srow = (lax.broadcasted_iota(jnp.int32, (gr, 1), 0) + r0).astype(F32)
        sel = jnp.where((srow == pos1) | (srow == pos2), 1.0, 0.0).astype(BF16)
        xs_ref[pl.ds(r0, gr), :] = _dot(sel, h).astype(BF16)
        return carry

    lax.fori_loop(0, n_gather, gather, 0)
    for pred, cp in _segment_copies(cnt_ref, loff_ref, goff_ref, i, xs_ref, xout_ref, sem, True):
        pl.when(pred)(cp.start)

    @pl.when(i == last)
    def _():
        @pl.when(i >= 1)
        def _():
            wait_tile(i - 1, 1 - slot)

        wait_tile(i, slot)


def _experts_kernel(be_ref, nv_ref, x_ref, wg_ref, wu_ref, wd_ref, o_ref, a_ref):
    del be_ref
    b = pl.program_id(0)
    n_valid = nv_ref[0]
    slot = lax.rem(b, 2)

    def up(s):
        xb = x_ref[...]
        a = _silu(_dot(xb, wg_ref[0].astype(BF16))) * _dot(xb, wu_ref[0].astype(BF16))
        a_ref[s] = a.astype(BF16)

    def down(s):
        o_ref[...] = _dot(a_ref[s], wd_ref[0].astype(BF16)).astype(BF16)

    @pl.when(b == 0)
    def _():
        up(slot)

    @pl.when((b >= 1) & (b < n_valid))
    def _():
        down(1 - slot)
        up(slot)

    @pl.when((b >= 1) & (b == n_valid))
    def _():
        down(1 - slot)

    @pl.when(b > n_valid)
    def _():
        o_ref[...] = jnp.zeros(o_ref.shape, BF16)


def _combine_kernel(cnt_ref, loff_ref, goff_ref, ds_ref, icol_ref, x_ref, gt_ref, o_ref, dl2_ref, sem2, *, rb):
    i = pl.program_id(0)
    tm = x_ref.shape[1]
    s_loc = dl2_ref.shape[1]
    slot = lax.rem(i, 2)

    def fetch_tile(tile, s):
        for pred, cp in _segment_copies(cnt_ref, loff_ref, goff_ref, tile, dl2_ref.at[s], ds_ref, sem2.at[s], False):
            pl.when(pred)(cp.start)

        def zero_tail(u, carry):
            dl2_ref[s, pl.ds(pl.multiple_of(u * SEG, SEG), SEG), :] = jnp.zeros((SEG, dl2_ref.shape[2]), BF16)
            return carry

        lax.fori_loop(_tile_units(cnt_ref, tile), s_loc // SEG, zero_tail, 0)

    @pl.when(i == 0)
    def _():
        fetch_tile(0, 0)

    @pl.when(i + 1 < pl.num_programs(0))
    def _():
        fetch_tile(i + 1, 1 - slot)

    _wait_segments(cnt_ref, i, dl2_ref.at[slot], ds_ref, sem2.at[slot], False)
    scol = lax.broadcasted_iota(jnp.int32, (1, s_loc), 1).astype(F32)
    for r in range(tm // rb):
        rows = slice(r * rb, (r + 1) * rb)
        cmb = (jnp.where(scol == icol_ref[rows, 0:1], icol_ref[rows, 2:3], 0.0)
               + jnp.where(scol == icol_ref[rows, 1:2], icol_ref[rows, 3:4], 0.0))
        y = _dot(cmb.astype(BF16), dl2_ref[slot])
        gt = gt_ref[0] if gt_ref.shape[1] == 1 else gt_ref[0, rows, :]
        o_ref[0, rows, :] = x_ref[0, rows, :] + gt * y


def _sorted_rows(n_tokens, tm):
    n_tiles = n_tokens // tm
    er = _expert_rows(n_tokens)
    rows = 2 * n_tokens + n_tiles * N_EXPERTS * (SEG - 1) + N_EXPERTS * (er - 1)
    return -(-rows // er) * er


def _moe(cnt, h2, x, gt, info_row, info_col, w_eg, w_eu, w_ed, x_sorted, tm, layer):
    g, t, d = x.shape
    n = g * t
    n_tiles = n // tm
    tpg = t // tm
    rm = gt.shape[1]
    er = _expert_rows(n)
    n_blocks = x_sorted.shape[0] // er
    s_loc = -(-(2 * tm + N_EXPERTS * (SEG - 1)) // GATHER_ROWS) * GATHER_ROWS
    upb = er // SEG
    reg_blk = (jnp.sum(cnt, axis=0) + upb - 1) // upb
    blk_end = jnp.cumsum(reg_blk)
    goff = ((blk_end - reg_blk)[None, :] * upb + jnp.cumsum(cnt, axis=0) - cnt).reshape(-1)
    loff = (jnp.cumsum(cnt, axis=1) - cnt).reshape(-1)
    cntf = cnt.reshape(-1)
    n_valid = blk_end[-1:]
    blk_expert = jnp.minimum(
        jnp.sum(jnp.arange(n_blocks, dtype=jnp.int32)[:, None] >= blk_end[None, :], axis=1), N_EXPERTS - 1
    ).astype(jnp.int32)

    x_sorted = pl.pallas_call(
        _sort_kernel,
        out_shape=jax.ShapeDtypeStruct(x_sorted.shape, BF16),
        grid_spec=pltpu.PrefetchScalarGridSpec(
            num_scalar_prefetch=3,
            grid=(n_tiles,),
            in_specs=[pl.BlockSpec((tm, d), lambda i, *_: (i, 0)),
                      pl.BlockSpec((8, tm), lambda i, *_: (0, i)),
                      pl.BlockSpec(memory_space=pl.ANY)],
            out_specs=pl.BlockSpec(memory_space=pl.ANY),
            scratch_shapes=[pltpu.VMEM((2, s_loc, d), BF16), pltpu.SemaphoreType.DMA((2,))],
        ),
        input_output_aliases={5: 0},
        compiler_params=_cparams(("arbitrary",)),
        name="moe_sort",
    )(cntf, loff, goff, h2, info_row, x_sorted)

    blk = lambda b, be, nv: (jnp.minimum(b, nv[0] - 1), 0)
    last = n_blocks - 1
    d_sorted = pl.pallas_call(
        _experts_kernel,
        out_shape=jax.ShapeDtypeStruct(x_sorted.shape, BF16),
        grid_spec=pltpu.PrefetchScalarGridSpec(
            num_scalar_prefetch=2,
            grid=(n_blocks + 1,),
            in_specs=[pl.BlockSpec((er, d), blk),
                      pl.BlockSpec((None, 1, d, D_EXPERT), lambda b, be, nv: (layer, be[jnp.minimum(b, last)], 0, 0)),
                      pl.BlockSpec((None, 1, d, D_EXPERT), lambda b, be, nv: (layer, be[jnp.minimum(b, last)], 0, 0)),
                      pl.BlockSpec((None, 1, D_EXPERT, d), lambda b, be, nv: (layer, be[jnp.maximum(b - 1, 0)], 0, 0))],
            out_specs=pl.BlockSpec((er, d), lambda b, be, nv: (jnp.maximum(b - 1, 0), 0)),
            scratch_shapes=[pltpu.VMEM((2, er, D_EXPERT), BF16)],
        ),
        compiler_params=_cparams(("arbitrary",), VMEM_LIMIT),
        name="moe_experts",
    )(blk_expert, n_valid, x_sorted, w_eg, w_eu, w_ed)

    gt_spec = pl.BlockSpec((1, rm, d), (lambda i, *_: (i // tpg, 0, 0)) if rm == 1
                           else (lambda i, *_: (i // tpg, i % tpg, 0)))
    tok3 = pl.BlockSpec((1, tm, d), lambda i, *_: (i // tpg, i % tpg, 0))
    out = pl.pallas_call(
        functools.partial(_combine_kernel, rb=min(tm, 256)),
        out_shape=jax.ShapeDtypeStruct((g, t, d), F32),
        grid_spec=pltpu.PrefetchScalarGridSpec(
            num_scalar_prefetch=3,
            grid=(n_tiles,),
            in_specs=[pl.BlockSpec(memory_space=pl.ANY),
                      pl.BlockSpec((tm, 8), lambda i, *_: (i, 0)),
                      tok3, gt_spec],
            out_specs=tok3,
            scratch_shapes=[pltpu.VMEM((2, s_loc, d), BF16), pltpu.SemaphoreType.DMA((2,))],
        ),
        compiler_params=_cparams(("arbitrary",), VMEM_LIMIT),
        name="moe_combine",
    )(cntf, loff, goff, d_sorted, info_col, x, gt)
    return out, x_sorted


def _rope_tables(pos):
    half = HEAD_DK // 2
    inv_freq = ROPE_BASE ** (-jnp.arange(half, dtype=F32) / half)
    ang = pos[:, None] * inv_freq[None, :]
    cos, sin = jnp.cos(ang), jnp.sin(ang)
    return jnp.concatenate([cos, cos], axis=1), jnp.concatenate([-sin, sin], axis=1)


def _layer(x, mods, lw, cfg, ret0, gla0, cache_k, cache_v, cosf, sinf, x_sorted):
    g, t, d = x.shape
    b, tseq = cfg["b"], cfg["tseq"]
    sh_m, sc_m, gt_m, sh_f, sc_f, gt_f = mods
    proj, log_a = _in_proj(x, sc_m, sh_m, lw["g_mix"], lw["w_in"], lw["w_ga"], lw["a2"], lw["a_bias"],
                           cosf, sinf, lw["gq"], lw["gk"], cfg["tm"], lw["layer"])
    proj_s = proj.reshape(b, tseq, PROJ_COLS)
    log_a_s = log_a.reshape(b, tseq, BRANCH_WIDTH)
    o_r, ret_new = _retention(proj_s, ret0, lw["g_ret_gn"], cfg["chunk"], cfg["tc"])
    o_g, gla_new = _gla(proj_s, log_a_s, gla0, lw["g_gla_gn"], cfg["chunk"], cfg["tc"])
    if cache_k is None:
        o_a = _band_attn_prompt(proj_s, lw["rel_bias"])
    else:
        o_a = _band_attn_sample(proj_s, cache_k, cache_v, lw["rel_bias"], lw["layer"])
    tok = lambda a: a.reshape(g, t, BRANCH_WIDTH)
    x1, h2 = _merge(x, tok(o_r), tok(o_g), tok(o_a), (sc_m, sh_m, gt_m, sc_f, sh_f), lw["g_mix"], lw["g_ffn"],
                    lw["w_gate"], lw["b_gate"], lw["w_branch"], lw["w_out"], cfg["tm"], lw["layer"])
    h2f = h2.reshape(g * t, d)
    info_row, tbl = _route(h2f, lw["wr_t"], lw["br_t"], cfg["tme"], SEG)
    x2, x_sorted = _moe(tbl[:, :, 0], h2f, x1, gt_f, info_row, info_row.T, lw["w_eg"], lw["w_eu"], lw["w_ed"],
                        x_sorted, cfg["tme"], lw["layer"])
    tail = proj_s[:, -min(tseq, BAND_PAST):]
    k_new = tail[:, :, 9 * BRANCH_WIDTH:10 * BRANCH_WIDTH]
    v_new = tail[:, :, 10 * BRANCH_WIDTH:11 * BRANCH_WIDTH]
    return x2, ret_new, gla_new, k_new, v_new, x_sorted


def kernel(x_prompt, x_sample, state_ret, state_gla, cache_att_k, cache_att_v, c_prompt, c_sample,
           w_ada, b_ada, g_mix, w_in, gla_a2, gla_a_bias, g_ret_gn, g_gla_gn, g_q_att, g_k_att, rel_bias,
           w_branch, w_gate, b_gate, w_out, g_ffn, w_router_group, b_router_group, w_router_exp,
           b_router_exp, w_exp_gate, w_exp_up, w_exp_down):
    depth = w_ada.shape[0]
    bp, seq, d = x_prompt.shape
    bs, dseq, _ = x_sample.shape
    n_s = bs * dseq

    pad = (-(bp + bs)) % 8
    c_all = jnp.concatenate([c_prompt, c_sample, jnp.zeros((pad, d), F32)], axis=0)
    mod = _ada_mod(c_all, w_ada, b_ada)

    ga0 = 8 * BRANCH_WIDTH
    w_in_b = jnp.concatenate([w_in[:, :, :ga0], w_in[:, :, ga0 + GLA_RANK:]], axis=2).astype(BF16)
    w_ga = jnp.pad(w_in[:, :, ga0:ga0 + GLA_RANK], ((0, 0), (0, 0), (0, V7X_LANES - GLA_RANK))).astype(BF16)
    a2 = jnp.pad(gla_a2, ((0, 0), (0, V7X_LANES - GLA_RANK), (0, 0))).astype(BF16)
    wr_t = jnp.zeros((depth, 32, d), F32)
    wr_t = wr_t.at[:, 0:N_GROUPS].set(jnp.swapaxes(w_router_group, 1, 2))
    wr_t = wr_t.at[:, 8:8 + N_EXPERTS].set(jnp.swapaxes(w_router_exp, 1, 2)).astype(BF16)
    br_t = jnp.zeros((depth, 32, 1), F32)
    br_t = br_t.at[:, 0:N_GROUPS, 0].set(b_router_group).at[:, 8:8 + N_EXPERTS, 0].set(b_router_exp)
    w_gate_b, w_branch_b, w_out_b = w_gate.astype(BF16), w_branch.astype(BF16), w_out.astype(BF16)
    w_eg, w_eu, w_ed = w_exp_gate, w_exp_up, w_exp_down

    cos_p, sin_p = _rope_tables(jnp.arange(seq, dtype=F32))
    cos_s, sin_s = _rope_tables(PAST_LEN + jnp.arange(dseq, dtype=F32))
    cos_s, sin_s = jnp.tile(cos_s, (bs, 1)), jnp.tile(sin_s, (bs, 1))

    cfg_p = dict(b=bp, tseq=seq, tm=512, chunk=4 * CHUNK, tc=512, tme=512)
    cfg_s = dict(b=bs, tseq=dseq, tm=n_s, chunk=min(dseq, CHUNK), tc=dseq, tme=n_s)
    zero_state = jnp.zeros((bp, RET_HEADS, HEAD_DK, HEAD_DK), F32)
    sorted_p = jnp.zeros((_sorted_rows(bp * seq, cfg_p["tme"]), d), BF16)
    sorted_s = jnp.zeros((_sorted_rows(n_s, cfg_s["tme"]), d), BF16)

    cache_kt = jnp.transpose(cache_att_k, (0, 1, 3, 4, 2))
    cache_vt = jnp.transpose(cache_att_v, (0, 1, 3, 4, 2))

    xp = x_prompt
    xs = x_sample.reshape(1, n_s, d)
    outs = [[] for _ in range(8)]
    for l in range(depth):
        lw = dict(
            layer=l, g_mix=g_mix[l][None], g_ffn=g_ffn[l][None], w_in=w_in_b, w_ga=w_ga[l], a2=a2[l],
            a_bias=gla_a_bias[l][None], gq=jnp.tile(g_q_att[l], ATT_HEADS)[None],
            gk=jnp.tile(g_k_att[l], ATT_HEADS)[None], g_ret_gn=g_ret_gn[l][None], g_gla_gn=g_gla_gn[l][None],
            rel_bias=rel_bias[l], w_gate=w_gate_b, b_gate=b_gate[l][None], w_branch=w_branch_b,
            w_out=w_out_b, wr_t=wr_t[l], br_t=br_t[l], w_eg=w_eg, w_eu=w_eu, w_ed=w_ed)
        mods_p = tuple(m[:, None, :] for m in jnp.split(mod[l, :bp], 6, axis=-1))
        mods_s = tuple(jnp.repeat(m, dseq, axis=0)[None] for m in jnp.split(mod[l, bp:bp + bs], 6, axis=-1))
        xp, rp, gp, kp, vp, sorted_p = _layer(xp, mods_p, lw, cfg_p, zero_state, zero_state, None, None,
                                              cos_p, sin_p, sorted_p)
        xs, rs, gs, ks, vs, sorted_s = _layer(xs, mods_s, lw, cfg_s, state_ret[l], state_gla[l], cache_kt, cache_vt,
                                              cos_s, sin_s, sorted_s)
        heads = lambda a: a.astype(F32).reshape(a.shape[0], a.shape[1], ATT_HEADS, ATT_HD)
        for lst, val in zip(outs, (rp, gp, heads(kp), heads(vp), rs, gs, heads(ks), heads(vs))):
            lst.append(val)
    return (xp, xs.reshape(bs, dseq, d)) + tuple(jnp.stack(o) for o in outs)
```

```python
import functools

import numpy as np
import jax
import jax.numpy as jnp
from jax import lax
from jax.experimental import pallas as pl
from jax.experimental.pallas import tpu as pltpu

F32 = jnp.float32
BF16 = jnp.bfloat16

D_MODEL = 1024
CHUNK = 64
BRANCH_WIDTH = 512
N_BRANCH = 3
RET_HEADS = 4
HEAD_DK = 128
GLA_RANK = 16
GLA_TAU = 16.0
ATT_HEADS = 8
ATT_HD = 64
BAND_PAST = 512
MAX_REL = 128
N_GROUPS = 4
EXPERTS_PER_GROUP = 4
N_EXPERTS = 16
D_EXPERT = 512
ROPE_BASE = 10000.0
EPS = 1e-6
GN_EPS = 1e-5
PAST_LEN = 1024
N_PROJ_BLOCKS = 11
PROJ_COLS = N_PROJ_BLOCKS * BRANCH_WIDTH

V7X_LANES = 128
V7X_VMEM_BYTES = 64 * 1024 * 1024
VMEM_LIMIT = 56 * 1024 * 1024
NEG_BIG = -1e30
LOG2_E = 1.4426950408889634


def _cparams(sem, vmem=None):
    return pltpu.CompilerParams(dimension_semantics=sem, vmem_limit_bytes=vmem)


def _const_spec(shape):
    nd = len(shape)
    return pl.BlockSpec(shape, lambda *_: (0,) * nd, pipeline_mode=pl.Buffered(1))


def _layer_spec(shape, layer):
    nd = len(shape)
    return pl.BlockSpec((None,) + tuple(shape), lambda *_: (layer,) + (0,) * nd, pipeline_mode=pl.Buffered(1))


def _dot(a, b):
    return jnp.dot(a, b, preferred_element_type=F32)


def _dot_nt(a, b):
    return lax.dot_general(a, b, (((1,), (1,)), ((), ())), preferred_element_type=F32)


def _dot_tn(a, b):
    return lax.dot_general(a, b, (((0,), (0,)), ((), ())), preferred_element_type=F32)


def _silu(x):
    return x * jax.nn.sigmoid(x)


def _norm_mod(x, g, sc, sh):
    ms = jnp.mean(x * x, axis=-1, keepdims=True)
    return x * lax.rsqrt(ms + EPS) * g * (1.0 + sc) + sh


def _ada_kernel(c_ref, w_ref, b_ref, o_ref):
    s = _silu(c_ref[...])
    o_ref[0] = _dot(s.astype(BF16), w_ref[0].astype(BF16)) + b_ref[0]


def _ada_mod(c_all, w_ada, b_ada):
    depth, d, n = w_ada.shape
    rows = c_all.shape[0]
    bn = 1536
    return pl.pallas_call(
        _ada_kernel,
        out_shape=jax.ShapeDtypeStruct((depth, rows, n), F32),
        grid=(depth, n // bn),
        in_specs=[
            pl.BlockSpec((rows, d), lambda l, j: (0, 0)),
            pl.BlockSpec((1, d, bn), lambda l, j: (l, 0, j)),
            pl.BlockSpec((1, 1, bn), lambda l, j: (l, 0, j)),
        ],
        out_specs=pl.BlockSpec((1, rows, bn), lambda l, j: (l, 0, j)),
        compiler_params=_cparams(("arbitrary", "arbitrary"), VMEM_LIMIT),
        name="ada_mod",
    )(c_all, w_ada, b_ada.reshape(depth, 1, n))


def _rope_heads(a, cosf, sinf):
    outs = []
    for h in range(RET_HEADS):
        ah = a[:, h * HEAD_DK:(h + 1) * HEAD_DK]
        outs.append(ah * cosf + pltpu.roll(ah, HEAD_DK // 2, 1) * sinf)
    return jnp.concatenate(outs, axis=1)


def _rms_heads64(a, gain):
    low = lax.broadcasted_iota(jnp.int32, (1, V7X_LANES), 1) < ATT_HD
    outs = []
    for c in range(a.shape[1] // V7X_LANES):
        ac = a[:, c * V7X_LANES:(c + 1) * V7X_LANES]
        sq = ac * ac
        lo = jnp.sum(jnp.where(low, sq, 0.0), axis=-1, keepdims=True)
        hi = jnp.sum(jnp.where(low, 0.0, sq), axis=-1, keepdims=True)
        ms = jnp.where(low, lo, hi) * (1.0 / ATT_HD)
        outs.append(ac * lax.rsqrt(ms + EPS))
    return jnp.concatenate(outs, axis=1) * gain


def _in_proj_kernel(x_ref, sc_ref, sh_ref, g_ref, w_ref, wga_ref, a2_ref, ab_ref, cos_ref, sin_ref,
                    gq_ref, gk_ref, proj_ref, la_ref):
    x = x_ref[0]
    hb = _norm_mod(x, g_ref[...], sc_ref[0], sh_ref[0]).astype(BF16)
    cosf = cos_ref[...]
    sinf = sin_ref[...]
    ga = _dot(hb, wga_ref[...])
    for j in range(N_PROJ_BLOCKS):
        if j == N_PROJ_BLOCKS // 2:
            z = _dot(ga.astype(BF16), a2_ref[...]) + ab_ref[...]
            la_ref[0] = jax.nn.log_sigmoid(z) * (1.0 / GLA_TAU)
        cols = slice(j * BRANCH_WIDTH, (j + 1) * BRANCH_WIDTH)
        acc = _dot(hb, w_ref[:, cols])
        if j == 0:
            acc = _rope_heads(acc, cosf, sinf)
        elif j == 1:
            acc = _rope_heads(acc, cosf, sinf) * (HEAD_DK ** -0.5)
        elif j == 4:
            acc = acc * (HEAD_DK ** -0.5)
        elif j == 8:
            acc = _rms_heads64(acc, gq_ref[...]) * (ATT_HD ** -0.5 * LOG2_E)
        elif j == 9:
            acc = _rms_heads64(acc, gk_ref[...])
        proj_ref[0, :, cols] = acc.astype(BF16)


def _in_proj(x, sc, sh, g_mix, w_in, w_ga, a2, a_bias, cosf, sinf, gq, gk, tm, layer):
    g, t, d = x.shape
    rm = sc.shape[1]
    mod_spec = pl.BlockSpec((1, rm, d), (lambda b, i: (b, 0, 0)) if rm == 1 else (lambda b, i: (b, i, 0)))
    return pl.pallas_call(
        _in_proj_kernel,
        out_shape=(jax.ShapeDtypeStruct((g, t, PROJ_COLS), BF16),
                   jax.ShapeDtypeStruct((g, t, BRANCH_WIDTH), F32)),
        grid=(g, t // tm),
        in_specs=[
            pl.BlockSpec((1, tm, d), lambda b, i: (b, i, 0)),
            mod_spec, mod_spec,
            _const_spec((1, d)),
            _layer_spec((d, PROJ_COLS), layer),
            _const_spec((d, V7X_LANES)),
            _const_spec((V7X_LANES, BRANCH_WIDTH)),
            _const_spec((1, BRANCH_WIDTH)),
            pl.BlockSpec((tm, HEAD_DK), lambda b, i: (i, 0)),
            pl.BlockSpec((tm, HEAD_DK), lambda b, i: (i, 0)),
            _const_spec((1, BRANCH_WIDTH)),
            _const_spec((1, BRANCH_WIDTH)),
        ],
        out_specs=(pl.BlockSpec((1, tm, PROJ_COLS), lambda b, i: (b, i, 0)),
                   pl.BlockSpec((1, tm, BRANCH_WIDTH), lambda b, i: (b, i, 0))),
        compiler_params=_cparams(("arbitrary", "arbitrary"), VMEM_LIMIT),
        name="in_proj",
    )(x, sc, sh, g_mix, w_in, w_ga, a2, a_bias, cosf, sinf, gq, gk)


def _retention_kernel(q_ref, k_ref, v_ref, g_ref, s0_ref, dm_ref, qd_ref, kd_ref, bd_ref, gn_ref,
                      o_ref, so_ref, st_ref, *, chunk, n_chunks):
    n_seq = q_ref.shape[0]

    @pl.when(pl.program_id(1) == 0)
    def _():
        st_ref[...] = s0_ref[...]

    def body(c, carry):
        rows = pl.ds(pl.multiple_of(c * chunk, chunk), chunk)
        for b in range(n_seq):
            for h in range(RET_HEADS):
                cols = slice(h * HEAD_DK, (h + 1) * HEAD_DK)
                q = q_ref[b, rows, cols]
                k = k_ref[b, rows, cols]
                v = v_ref[b, rows, cols]
                scores = _dot_nt(q, k) * dm_ref[h]
                o = _dot(scores.astype(BF16), v)
                o = o + _dot((q.astype(F32) * qd_ref[h]).astype(BF16), st_ref[b, h].astype(BF16))
                st_ref[b, h] = (st_ref[b, h] * bd_ref[h]
                                + _dot_tn((k.astype(F32) * kd_ref[h]).astype(BF16), v))
                mu = jnp.mean(o, axis=-1, keepdims=True)
                oc = o - mu
                var = jnp.mean(oc * oc, axis=-1, keepdims=True)
                on = oc * lax.rsqrt(var + GN_EPS) * gn_ref[:, cols]
                o_ref[b, rows, cols] = (_silu(g_ref[b, rows, cols].astype(F32)) * on).astype(BF16)
        return carry

    lax.fori_loop(0, n_chunks, body, 0)

    @pl.when(pl.program_id(1) == pl.num_programs(1) - 1)
    def _():
        so_ref[...] = st_ref[...]


def _retention_tables(chunk):
    log_gamma = jnp.log(1.0 - jnp.exp2(-5.0 - jnp.arange(RET_HEADS, dtype=F32)))
    idx = jnp.arange(chunk, dtype=F32)
    diff = idx[:, None] - idx[None, :]
    dmask = jnp.where(diff >= 0, jnp.exp(log_gamma[:, None, None] * jnp.maximum(diff, 0.0)), 0.0)
    qd = jnp.exp(log_gamma[:, None] * (idx + 1.0))[:, :, None]
    kd = jnp.exp(log_gamma[:, None] * (chunk - 1.0 - idx))[:, :, None]
    bd = jnp.exp(log_gamma * chunk)[:, None, None]
    bc = lambda a, r: jnp.broadcast_to(a, (RET_HEADS, r, HEAD_DK)).astype(F32)
    return dmask.astype(F32), bc(qd, chunk), bc(kd, chunk), bc(bd, 1)


def _seqs_per_step(b):
    return 4 if b % 4 == 0 and b > 4 else (2 if b % 2 == 0 else 1)


def _retention(proj, state0, gn, chunk, tc):
    b, t, _ = proj.shape
    nb = _seqs_per_step(b)
    dm, qd, kd, bd = _retention_tables(chunk)
    col = lambda j: pl.BlockSpec((nb, tc, BRANCH_WIDTH), lambda i, s, j=j: (i, s, j))
    st_spec = pl.BlockSpec((nb, RET_HEADS, HEAD_DK, HEAD_DK), lambda i, s: (i, 0, 0, 0))
    return pl.pallas_call(
        functools.partial(_retention_kernel, chunk=chunk, n_chunks=tc // chunk),
        out_shape=(jax.ShapeDtypeStruct((b, t, BRANCH_WIDTH), BF16),
                   jax.ShapeDtypeStruct((b, RET_HEADS, HEAD_DK, HEAD_DK), F32)),
        grid=(b // nb, t // tc),
        in_specs=[col(0), col(1), col(2), col(3), st_spec,
                  _const_spec(dm.shape), _const_spec(qd.shape), _const_spec(kd.shape), _const_spec(bd.shape),
                  _const_spec((1, BRANCH_WIDTH))],
        out_specs=(pl.BlockSpec((nb, tc, BRANCH_WIDTH), lambda i, s: (i, s, 0)), st_spec),
        scratch_shapes=[pltpu.VMEM((nb, RET_HEADS, HEAD_DK, HEAD_DK), F32)],
        compiler_params=_cparams(("arbitrary", "arbitrary")),
        name="retention",
    )(proj, proj, proj, proj, state0, dm, qd, kd, bd, gn)


GLA_SAFE_LOG_DECAY = -60.0


def _gla_intra_direct(q, bh, rows_ref, chunk):
    trow = lax.broadcasted_iota(jnp.int32, (chunk, 1), 0)

    def step(s, acc):
        bs = rows_ref[0, pl.ds(s, 1), :]
        ks = rows_ref[1, pl.ds(s, 1), :]
        vs = rows_ref[2, pl.ds(s, 1), :]
        e = jnp.exp(jnp.minimum(bh - bs, 0.0))
        col = jnp.sum(q * ks * e, axis=-1, keepdims=True)
        return acc + jnp.where(trow >= s, col, 0.0) * vs

    return lax.fori_loop(0, chunk, step, jnp.zeros((chunk, HEAD_DK), F32))


def _gla_kernel(q_ref, k_ref, v_ref, r_ref, la_ref, s0_ref, tri_ref, gn_ref, o_ref, so_ref, st_ref, rows_ref,
                *, chunk, n_chunks):
    n_seq = q_ref.shape[0]

    @pl.when(pl.program_id(1) == 0)
    def _():
        for b in range(n_seq):
            for h in range(RET_HEADS):
                st_ref[b, h] = s0_ref[b, h].T

    tri = tri_ref[...]
    causal = (lax.broadcasted_iota(jnp.int32, (chunk, chunk), 0)
              >= lax.broadcasted_iota(jnp.int32, (chunk, chunk), 1))

    def chunk_body(factored):
        def body(c, carry):
            rows = pl.ds(pl.multiple_of(c * chunk, chunk), chunk)
            for b in range(n_seq):
                la = la_ref[b, rows, :]
                la_hi = la.astype(BF16)
                la_lo = (la - la_hi.astype(F32)).astype(BF16)
                b_all = _dot(tri, la_hi) + _dot(tri, la_lo)
                for h in range(RET_HEADS):
                    cols = slice(h * HEAD_DK, (h + 1) * HEAD_DK)
                    q = q_ref[b, rows, cols].astype(F32)
                    k = k_ref[b, rows, cols].astype(F32)
                    v = v_ref[b, rows, cols]
                    bh = b_all[:, cols]
                    bl = bh[chunk - 1:chunk, :]
                    qh = (q * jnp.exp(bh)).astype(BF16)
                    if factored:
                        kh = (k * jnp.exp(-bh)).astype(BF16)
                        scores = jnp.where(causal, _dot_nt(qh, kh), 0.0)
                        o = _dot(scores.astype(BF16), v)
                    else:
                        rows_ref[0] = bh
                        rows_ref[1] = k
                        rows_ref[2] = v.astype(F32)
                        o = _gla_intra_direct(q, bh, rows_ref, chunk)
                    o = o + _dot_nt(qh, st_ref[b, h].astype(BF16))
                    kd = (k * jnp.exp(bl - bh)).astype(BF16)
                    st_ref[b, h] = st_ref[b, h] * jnp.exp(bl) + _dot_tn(v, kd)
                    ms = jnp.mean(o * o, axis=-1, keepdims=True)
                    on = o * lax.rsqrt(ms + EPS) * gn_ref[:, cols]
                    o_ref[b, rows, cols] = (_silu(r_ref[b, rows, cols].astype(F32)) * on).astype(BF16)
            return carry
        return body

    lowest = None
    for b in range(n_seq):
        for c in range(n_chunks):
            tot = jnp.sum(la_ref[b, c * chunk:(c + 1) * chunk, :], axis=0, keepdims=True)
            lowest = tot if lowest is None else jnp.minimum(lowest, tot)
    safe = jnp.min(lowest) > GLA_SAFE_LOG_DECAY

    @pl.when(safe)
    def _():
        lax.fori_loop(0, n_chunks, chunk_body(True), 0)

    @pl.when(jnp.logical_not(safe))
    def _():
        lax.fori_loop(0, n_chunks, chunk_body(False), 0)

    @pl.when(pl.program_id(1) == pl.num_programs(1) - 1)
    def _():
        for b in range(n_seq):
            for h in range(RET_HEADS):
                so_ref[b, h] = st_ref[b, h].T


def _gla(proj, log_a, state0, gn, chunk, tc):
    b, t, _ = proj.shape
    nb = _seqs_per_step(b)
    tri = jnp.tril(jnp.ones((chunk, chunk), F32)).astype(BF16)
    col = lambda j: pl.BlockSpec((nb, tc, BRANCH_WIDTH), lambda i, s, j=j: (i, s, j))
    st_spec = pl.BlockSpec((nb, RET_HEADS, HEAD_DK, HEAD_DK), lambda i, s: (i, 0, 0, 0))
    return pl.pallas_call(
        functools.partial(_gla_kernel, chunk=chunk, n_chunks=tc // chunk),
        out_shape=(jax.ShapeDtypeStruct((b, t, BRANCH_WIDTH), BF16),
                   jax.ShapeDtypeStruct((b, RET_HEADS, HEAD_DK, HEAD_DK), F32)),
        grid=(b // nb, t // tc),
        in_specs=[col(4), col(5), col(6), col(7),
                  pl.BlockSpec((nb, tc, BRANCH_WIDTH), lambda i, s: (i, s, 0)),
                  st_spec, _const_spec((chunk, chunk)), _const_spec((1, BRANCH_WIDTH))],
        out_specs=(pl.BlockSpec((nb, tc, BRANCH_WIDTH), lambda i, s: (i, s, 0)), st_spec),
        scratch_shapes=[pltpu.VMEM((nb, RET_HEADS, HEAD_DK, HEAD_DK), F32),
                        pltpu.VMEM((3, chunk, HEAD_DK), F32)],
        compiler_params=_cparams(("arbitrary", "arbitrary")),
        name="gla",
    )(proj, proj, proj, proj, log_a, state0, tri, gn)


def _attn_stage_scratch(rb, win):
    pairs = ATT_HEADS // 2
    return [pltpu.VMEM((2, pairs, 2 * rb, win), F32), pltpu.VMEM((2, pairs, 2 * rb, win), BF16)]


def _band_attn_kernel(q_ref, kp_ref, kc_ref, vp_ref, vc_ref, bias_ref, o_ref, kw_ref, vw_ref,
                      s_ref, e_ref,
                      *, n_sub, rb, win, prev_rows):
    cur_rows = kc_ref.shape[1]
    kw_ref[0:prev_rows, :] = kp_ref[0]
    kw_ref[prev_rows:prev_rows + cur_rows, :] = kc_ref[0]
    pairs = ATT_HEADS // 2
    for p in range(pairs):
        cols = slice(p * V7X_LANES, (p + 1) * V7X_LANES)
        vw_ref[0:prev_rows, 2 * p * V7X_LANES:(2 * p + 1) * V7X_LANES] = vp_ref[0, :, cols]
        vw_ref[prev_rows:prev_rows + cur_rows, 2 * p * V7X_LANES:(2 * p + 1) * V7X_LANES] = vc_ref[0, :, cols]
        vw_ref[:, (2 * p + 1) * V7X_LANES:(2 * p + 2) * V7X_LANES] = jnp.ones((vw_ref.shape[0], V7X_LANES), BF16)
    low = lax.broadcasted_iota(jnp.int32, (1, V7X_LANES), 1) < ATT_HD
    q0 = pl.program_id(1) * cur_rows

    def scores(i, masked):
        if masked:
            valid = lax.broadcasted_iota(jnp.int32, (1, win), 1) >= prev_rows - q0 - i * rb
        for p in range(pairs):
            cols = slice(p * V7X_LANES, (p + 1) * V7X_LANES)
            q2 = q_ref[0, i * rb:(i + 1) * rb, cols]
            zero = jnp.zeros_like(q2)
            qst = jnp.concatenate([jnp.where(low, q2, zero), jnp.where(low, zero, q2)], axis=0)
            s = _dot_nt(qst, kw_ref[i * rb:i * rb + win, cols]) + bias_ref[p]
            if masked:
                s = jnp.where(valid, s, NEG_BIG)
            s_ref[i % 2, p] = s

    def softmax(i):
        for p in range(pairs):
            s = s_ref[i % 2, p]
            e_ref[i % 2, p] = jnp.exp2(s - jnp.max(s, axis=-1, keepdims=True)).astype(BF16)

    def values(i):
        for p in range(pairs):
            cols = slice(p * V7X_LANES, (p + 1) * V7X_LANES)
            od = _dot(e_ref[i % 2, p], vw_ref[i * rb:i * rb + win, 2 * p * V7X_LANES:2 * (p + 1) * V7X_LANES])
            o2 = od[:, :V7X_LANES] / od[:, V7X_LANES:]
            o_ref[0, i * rb:(i + 1) * rb, cols] = jnp.where(low, o2[:rb], o2[rb:]).astype(BF16)

    def pipeline(masked):
        for step in range(n_sub + 2):
            if step < n_sub:
                scores(step, masked)
            if 0 <= step - 1 < n_sub:
                softmax(step - 1)
            if 0 <= step - 2 < n_sub:
                values(step - 2)

    pl.when(q0 < prev_rows)(functools.partial(pipeline, True))
    pl.when(q0 >= prev_rows)(functools.partial(pipeline, False))


def _rel_bias_table(rel_bias, rb, win, chunk):
    period = win + rb
    u = np.arange(period)
    u = np.where(u < win, u, u - period)
    vec = rel_bias.astype(F32)[:, np.clip(BAND_PAST - u, -MAX_REL, MAX_REL) + MAX_REL]
    bias = jnp.tile(vec, (1, rb))[:, :rb * (period - 1)].reshape(-1, rb, period - 1)[:, :, :win]
    r = np.arange(rb)[:, None]
    j = np.arange(win)[None, :]
    c0 = (r // chunk) * chunk
    allowed = (j >= c0) & (j < c0 + BAND_PAST + chunk)
    bias = jnp.where(jnp.asarray(allowed)[None], bias * LOG2_E, NEG_BIG)
    return bias.reshape(ATT_HEADS // 2, 2 * rb, win)


def _band_attn_prompt(proj, rel_bias):
    b, t, _ = proj.shape
    qb, rb = 512, 128
    win = BAND_PAST + rb
    bias = _rel_bias_table(rel_bias, rb, win, CHUNK)
    cur = lambda j: pl.BlockSpec((1, qb, BRANCH_WIDTH), lambda i, s, j=j: (i, s, j))
    prev = lambda j: pl.BlockSpec((1, qb, BRANCH_WIDTH), lambda i, s, j=j: (i, jnp.maximum(s - 1, 0), j))
    return pl.pallas_call(
        functools.partial(_band_attn_kernel, n_sub=qb // rb, rb=rb, win=win, prev_rows=BAND_PAST),
        out_shape=jax.ShapeDtypeStruct((b, t, BRANCH_WIDTH), BF16),
        grid=(b, t // qb),
        in_specs=[cur(8), prev(9), cur(9), prev(10), cur(10), _const_spec(bias.shape)],
        out_specs=pl.BlockSpec((1, qb, BRANCH_WIDTH), lambda i, s: (i, s, 0)),
        scratch_shapes=[pltpu.VMEM((BAND_PAST + qb, BRANCH_WIDTH), BF16),
                        pltpu.VMEM((BAND_PAST + qb, 2 * BRANCH_WIDTH), BF16)] + _attn_stage_scratch(rb, win),
        compiler_params=_cparams(("arbitrary", "arbitrary")),
        name="band_attn",
    )(proj, proj, proj, proj, proj, bias)


def _band_attn_sample_kernel(q_ref, kn_ref, vn_ref, kc_ref, vc_ref, bias_ref, o_ref):
    t = q_ref.shape[1]
    past = kc_ref.shape[-1]
    low = lax.broadcasted_iota(jnp.int32, (1, V7X_LANES), 1) < ATT_HD
    for p in range(ATT_HEADS // 2):
        cols = slice(p * V7X_LANES, (p + 1) * V7X_LANES)
        q2 = q_ref[0, :, cols]
        zero = jnp.zeros_like(q2)
        qst = jnp.concatenate([jnp.where(low, q2, zero), jnp.where(low, zero, q2)], axis=0)
        kt = kc_ref[0, 2 * p:2 * p + 2].reshape(V7X_LANES, past).astype(BF16)
        vt = vc_ref[0, 2 * p:2 * p + 2].reshape(V7X_LANES, past).astype(BF16)
        s = jnp.concatenate([_dot(qst, kt), _dot_nt(qst, kn_ref[0, :, cols])], axis=1) + bias_ref[p]
        m = jnp.max(s, axis=-1, keepdims=True)
        e = jnp.exp2(s - m)
        den = jnp.sum(e, axis=-1, keepdims=True)
        eb = e.astype(BF16)
        o2 = (_dot_nt(eb[:, :past], vt) + _dot(eb[:, past:], vn_ref[0, :, cols])) / den
        o_ref[0, :, cols] = jnp.where(low, o2[:t], o2[t:]).astype(BF16)


def _band_attn_sample(proj, cache_kt, cache_vt, rel_bias, layer):
    b, t, _ = proj.shape
    win = BAND_PAST + t
    bias = _rel_bias_table(rel_bias, t, win, t)
    cur = lambda j: pl.BlockSpec((1, t, BRANCH_WIDTH), lambda i, j=j: (i, 0, j))
    cache = pl.BlockSpec((None, 1, ATT_HEADS, ATT_HD, BAND_PAST), lambda i: (layer, i, 0, 0, 0))
    return pl.pallas_call(
        _band_attn_sample_kernel,
        out_shape=jax.ShapeDtypeStruct((b, t, BRANCH_WIDTH), BF16),
        grid=(b,),
        in_specs=[cur(8), cur(9), cur(10), cache, cache, _const_spec(bias.shape)],
        out_specs=pl.BlockSpec((1, t, BRANCH_WIDTH), lambda i: (i, 0, 0)),
        compiler_params=_cparams(("arbitrary",)),
        name="band_attn_sample",
    )(proj, proj, proj, cache_kt, cache_vt, bias)


def _merge_kernel(x_ref, or_ref, og_ref, oa_ref, scm_ref, shm_ref, gtm_ref, scf_ref, shf_ref,
                  gmix_ref, gffn_ref, wg_ref, bg_ref, wb_ref, wo_ref, xo_ref, h2_ref):
    x = x_ref[0]
    d = x.shape[1]
    hb = _norm_mod(x, gmix_ref[...], scm_ref[0], shm_ref[0]).astype(BF16)
    merged = None
    for n, o_ref in enumerate((or_ref, og_ref, oa_ref)):
        gate = jax.nn.sigmoid(_dot(hb, wg_ref[:, n * d:(n + 1) * d]) + bg_ref[:, n * d:(n + 1) * d])
        y = gate * _dot(o_ref[0], wb_ref[n])
        merged = y if merged is None else merged + y
    mix = _dot(merged.astype(BF16), wo_ref[...])
    xn = x + gtm_ref[0] * mix
    xo_ref[0] = xn
    h2_ref[0] = _norm_mod(xn, gffn_ref[...], scf_ref[0], shf_ref[0]).astype(BF16)


def _merge(x, o_r, o_g, o_a, mods, g_mix, g_ffn, w_gate, b_gate, w_branch, w_out, tm, layer):
    g, t, d = x.shape
    rm = mods[0].shape[1]
    mod_spec = pl.BlockSpec((1, rm, d), (lambda b, i: (b, 0, 0)) if rm == 1 else (lambda b, i: (b, i, 0)))
    tok = lambda w: pl.BlockSpec((1, tm, w), lambda b, i: (b, i, 0))
    return pl.pallas_call(
        _merge_kernel,
        out_shape=(jax.ShapeDtypeStruct((g, t, d), F32), jax.ShapeDtypeStruct((g, t, d), BF16)),
        grid=(g, t // tm),
        in_specs=[tok(d), tok(BRANCH_WIDTH), tok(BRANCH_WIDTH), tok(BRANCH_WIDTH)] + [mod_spec] * 5 + [
            _const_spec((1, d)), _const_spec((1, d)),
            _layer_spec((d, N_BRANCH * d), layer), _const_spec((1, N_BRANCH * d)),
            _layer_spec((N_BRANCH, BRANCH_WIDTH, d), layer), _layer_spec((d, d), layer)],
        out_specs=(tok(d), tok(d)),
        compiler_params=_cparams(("arbitrary", "arbitrary"), VMEM_LIMIT),
        name="merge",
    )(x, o_r, o_g, o_a, *mods, g_mix, g_ffn, w_gate, b_gate, w_branch, w_out)


def _first_argmax(vals, n):
    row = lax.broadcasted_iota(jnp.int32, vals.shape, 0).astype(F32)
    m = jnp.max(vals, axis=0, keepdims=True)
    idx = jnp.min(jnp.where(vals == m, row, float(n)), axis=0, keepdims=True)
    return m, idx


def _route_kernel(h_ref, wr_ref, br_ref, up_ref, l16_ref, info_ref, tbl_ref, *, rows_per_block):
    tm = h_ref.shape[0]
    lg = _dot_nt(wr_ref[...], h_ref[...]) + br_ref[...]
    g = lg[0:N_GROUPS]
    gmax, grp = _first_argmax(g, N_GROUPS)
    p_group = 1.0 / jnp.sum(jnp.exp(g - gmax), axis=0, keepdims=True)
    esel = jnp.zeros((EXPERTS_PER_GROUP, tm), F32)
    for gi in range(N_GROUPS):
        blk = lg[8 + gi * EXPERTS_PER_GROUP:8 + (gi + 1) * EXPERTS_PER_GROUP]
        esel = esel + jnp.where(grp == float(gi), blk, 0.0)
    v1, i1 = _first_argmax(esel, EXPERTS_PER_GROUP)
    row4 = lax.broadcasted_iota(jnp.int32, esel.shape, 0).astype(F32)
    v2, i2 = _first_argmax(jnp.where(row4 == i1, -jnp.inf, esel), EXPERTS_PER_GROUP)
    e21 = jnp.exp(v2 - v1)
    w1 = p_group / (1.0 + e21)
    w2 = p_group * e21 / (1.0 + e21)
    e1 = grp * float(EXPERTS_PER_GROUP) + i1
    e2 = grp * float(EXPERTS_PER_GROUP) + i2
    row16 = lax.broadcasted_iota(jnp.int32, (N_EXPERTS, tm), 0).astype(F32)
    hit1 = row16 == e1
    hit2 = row16 == e2
    onehot = jnp.where(hit1 | hit2, 1.0, 0.0)
    prefix = _dot(onehot.astype(BF16), up_ref[...])
    cnt = jnp.sum(onehot, axis=1, keepdims=True)
    nblk = jnp.floor((cnt + float(rows_per_block - 1)) * (1.0 / rows_per_block))
    nblk_b = jnp.broadcast_to(nblk, (N_EXPERTS, V7X_LANES))
    offb = _dot(l16_ref[...], nblk_b.astype(BF16))[:, 0:1]
    base = offb * float(rows_per_block) + prefix
    info_ref[0:1, :] = jnp.sum(jnp.where(hit1, base, 0.0), axis=0, keepdims=True)
    info_ref[1:2, :] = jnp.sum(jnp.where(hit2, base, 0.0), axis=0, keepdims=True)
    info_ref[2:3, :] = w1
    info_ref[3:4, :] = w2
    info_ref[4:8, :] = jnp.zeros((4, tm), F32)
    tbl_ref[0] = nblk_b.astype(jnp.int32)


def _route(h2, wr_t, br_t, tm, rows_per_block):
    n, d = h2.shape
    nt = n // tm
    upper = jnp.triu(jnp.ones((tm, tm), F32), 1).astype(BF16)
    l16 = jnp.tril(jnp.ones((N_EXPERTS, N_EXPERTS), F32), -1).astype(BF16)
    return pl.pallas_call(
        functools.partial(_route_kernel, rows_per_block=rows_per_block),
        out_shape=(jax.ShapeDtypeStruct((8, n), F32),
                   jax.ShapeDtypeStruct((nt, N_EXPERTS, V7X_LANES), jnp.int32)),
        grid=(nt,),
        in_specs=[pl.BlockSpec((tm, d), lambda i: (i, 0)),
                  _const_spec((32, d)), _const_spec((32, 1)),
                  _const_spec((tm, tm)), _const_spec((N_EXPERTS, N_EXPERTS))],
        out_specs=(pl.BlockSpec((8, tm), lambda i: (0, i)),
                   pl.BlockSpec((1, N_EXPERTS, V7X_LANES), lambda i: (i, 0, 0))),
        compiler_params=_cparams(("arbitrary",)),
        name="route",
    )(h2, wr_t, br_t, upper, l16)


SEG = 16
GATHER_ROWS = 256
MAX_EXPERT_ROWS = 512


def _expert_rows(n_tokens):
    mean_rows = 2 * n_tokens // N_EXPERTS
    return int(min(MAX_EXPERT_ROWS, max(4 * SEG, 1 << (mean_rows // 4).bit_length())))


def _segment_copies(cnt_ref, loff_ref, goff_ref, tile, local_ref, global_ref, sem, to_global):
    for e in range(N_EXPERTS):
        cnt = cnt_ref[tile * N_EXPERTS + e]
        rows = pl.multiple_of(cnt * SEG, SEG)
        loc = local_ref.at[pl.ds(pl.multiple_of(loff_ref[tile * N_EXPERTS + e] * SEG, SEG), rows)]
        glo = global_ref.at[pl.ds(pl.multiple_of(goff_ref[tile * N_EXPERTS + e] * SEG, SEG), rows)]
        cp = pltpu.make_async_copy(loc, glo, sem) if to_global else pltpu.make_async_copy(glo, loc, sem)
        yield cnt > 0, cp


def _tile_units(cnt_ref, tile):
    total = jnp.int32(0)
    for e in range(N_EXPERTS):
        total = total + cnt_ref[tile * N_EXPERTS + e]
    return total


def _wait_segments(cnt_ref, tile, local_ref, global_ref, sem, to_global):
    rows = pl.multiple_of(_tile_units(cnt_ref, tile) * SEG, SEG)
    loc = local_ref.at[pl.ds(0, rows)]
    glo = global_ref.at[pl.ds(0, rows)]
    cp = pltpu.make_async_copy(loc, glo, sem) if to_global else pltpu.make_async_copy(glo, loc, sem)
    pl.when(rows > 0)(cp.wait)


def _sort_kernel(cnt_ref, loff_ref, goff_ref, h_ref, irow_ref, xin_ref, xout_ref, xs2_ref, sem2):
    del xin_ref
    i = pl.program_id(0)
    last = pl.num_programs(0) - 1
    slot = lax.rem(i, 2)
    xs_ref = xs2_ref.at[slot]
    sem = sem2.at[slot]

    def wait_tile(tile, s):
        _wait_segments(cnt_ref, tile, xs2_ref.at[s], xout_ref, sem2.at[s], True)

    @pl.when(i >= 2)
    def _():
        wait_tile(i - 2, slot)

    gr = GATHER_ROWS
    n_gather = (_tile_units(cnt_ref, i) * SEG + gr - 1) // gr
    pos1 = irow_ref[0:1, :]
    pos2 = irow_ref[1:2, :]
    h = h_ref[...]

    def gather(gb, carry):
        r0 = pl.multiple_of(gb * gr, gr)
        srow = (lax.broadcasted_iota(jnp.int32, (gr, 1), 0) + r0).astype(F32)
        sel = jnp.where((srow == pos1) | (srow == pos2), 1.0, 0.0).astype(BF16)
        xs_ref[pl.ds(r0, gr), :] = _dot(sel, h).astype(BF16)
        return carry

    lax.fori_loop(0, n_gather, gather, 0)
    for pred, cp in _segment_copies(cnt_ref, loff_ref, goff_ref, i, xs_ref, xout_ref, sem, True):
        pl.when(pred)(cp.start)

    @pl.when(i == last)
    def _():
        @pl.when(i >= 1)
        def _():
            wait_tile(i - 1, 1 - slot)

        wait_tile(i, slot)


def _experts_kernel(be_ref, nv_ref, x_ref, wg_ref, wu_ref, wd_ref, o_ref, a_ref):
    del be_ref
    b = pl.program_id(0)
    n_valid = nv_ref[0]
    slot = lax.rem(b, 2)

    def up(s):
        xb = x_ref[...]
        a = _silu(_dot(xb, wg_ref[0].astype(BF16))) * _dot(xb, wu_ref[0].astype(BF16))
        a_ref[s] = a.astype(BF16)

    def down(s):
        o_ref[...] = _dot(a_ref[s], wd_ref[0].astype(BF16)).astype(BF16)

    @pl.when(b == 0)
    def _():
        up(slot)

    @pl.when((b >= 1) & (b < n_valid))
    def _():
        down(1 - slot)
        up(slot)

    @pl.when((b >= 1) & (b == n_valid))
    def _():
        down(1 - slot)

    @pl.when(b > n_valid)
    def _():
        o_ref[...] = jnp.zeros(o_ref.shape, BF16)


def _combine_kernel(cnt_ref, loff_ref, goff_ref, ds_ref, icol_ref, x_ref, gt_ref, o_ref, dl2_ref, sem2, *, rb):
    i = pl.program_id(0)
    tm = x_ref.shape[1]
    s_loc = dl2_ref.shape[1]
    slot = lax.rem(i, 2)

    def fetch_tile(tile, s):
        for pred, cp in _segment_copies(cnt_ref, loff_ref, goff_ref, tile, dl2_ref.at[s], ds_ref, sem2.at[s], False):
            pl.when(pred)(cp.start)

        def zero_tail(u, carry):
            dl2_ref[s, pl.ds(pl.multiple_of(u * SEG, SEG), SEG), :] = jnp.zeros((SEG, dl2_ref.shape[2]), BF16)
            return carry

        lax.fori_loop(_tile_units(cnt_ref, tile), s_loc // SEG, zero_tail, 0)

    @pl.when(i == 0)
    def _():
        fetch_tile(0, 0)

    @pl.when(i + 1 < pl.num_programs(0))
    def _():
        fetch_tile(i + 1, 1 - slot)

    _wait_segments(cnt_ref, i, dl2_ref.at[slot], ds_ref, sem2.at[slot], False)
    scol = lax.broadcasted_iota(jnp.int32, (1, s_loc), 1).astype(F32)
    for r in range(tm // rb):
        rows = slice(r * rb, (r + 1) * rb)
        cmb = (jnp.where(scol == icol_ref[rows, 0:1], icol_ref[rows, 2:3], 0.0)
               + jnp.where(scol == icol_ref[rows, 1:2], icol_ref[rows, 3:4], 0.0))
        y = _dot(cmb.astype(BF16), dl2_ref[slot])
        gt = gt_ref[0] if gt_ref.shape[1] == 1 else gt_ref[0, rows, :]
        o_ref[0, rows, :] = x_ref[0, rows, :] + gt * y


def _sorted_rows(n_tokens, tm):
    n_tiles = n_tokens // tm
    er = _expert_rows(n_tokens)
    rows = 2 * n_tokens + n_tiles * N_EXPERTS * (SEG - 1) + N_EXPERTS * (er - 1)
    return -(-rows // er) * er


def _moe(cnt, h2, x, gt, info_row, info_col, w_eg, w_eu, w_ed, x_sorted, tm, layer):
    g, t, d = x.shape
    n = g * t
    n_tiles = n // tm
    tpg = t // tm
    rm = gt.shape[1]
    er = _expert_rows(n)
    n_blocks = x_sorted.shape[0] // er
    s_loc = -(-(2 * tm + N_EXPERTS * (SEG - 1)) // GATHER_ROWS) * GATHER_ROWS
    upb = er // SEG
    reg_blk = (jnp.sum(cnt, axis=0) + upb - 1) // upb
    blk_end = jnp.cumsum(reg_blk)
    goff = ((blk_end - reg_blk)[None, :] * upb + jnp.cumsum(cnt, axis=0) - cnt).reshape(-1)
    loff = (jnp.cumsum(cnt, axis=1) - cnt).reshape(-1)
    cntf = cnt.reshape(-1)
    n_valid = blk_end[-1:]
    blk_expert = jnp.minimum(
        jnp.sum(jnp.arange(n_blocks, dtype=jnp.int32)[:, None] >= blk_end[None, :], axis=1), N_EXPERTS - 1
    ).astype(jnp.int32)

    x_sorted = pl.pallas_call(
        _sort_kernel,
        out_shape=jax.ShapeDtypeStruct(x_sorted.shape, BF16),
        grid_spec=pltpu.PrefetchScalarGridSpec(
            num_scalar_prefetch=3,
            grid=(n_tiles,),
            in_specs=[pl.BlockSpec((tm, d), lambda i, *_: (i, 0)),
                      pl.BlockSpec((8, tm), lambda i, *_: (0, i)),
                      pl.BlockSpec(memory_space=pl.ANY)],
            out_specs=pl.BlockSpec(memory_space=pl.ANY),
            scratch_shapes=[pltpu.VMEM((2, s_loc, d), BF16), pltpu.SemaphoreType.DMA((2,))],
        ),
        input_output_aliases={5: 0},
        compiler_params=_cparams(("arbitrary",)),
        name="moe_sort",
    )(cntf, loff, goff, h2, info_row, x_sorted)

    blk = lambda b, be, nv: (jnp.minimum(b, nv[0] - 1), 0)
    last = n_blocks - 1
    d_sorted = pl.pallas_call(
        _experts_kernel,
        out_shape=jax.ShapeDtypeStruct(x_sorted.shape, BF16),
        grid_spec=pltpu.PrefetchScalarGridSpec(
            num_scalar_prefetch=2,
            grid=(n_blocks + 1,),
            in_specs=[pl.BlockSpec((er, d), blk),
                      pl.BlockSpec((None, 1, d, D_EXPERT), lambda b, be, nv: (layer, be[jnp.minimum(b, last)], 0, 0)),
                      pl.BlockSpec((None, 1, d, D_EXPERT), lambda b, be, nv: (layer, be[jnp.minimum(b, last)], 0, 0)),
                      pl.BlockSpec((None, 1, D_EXPERT, d), lambda b, be, nv: (layer, be[jnp.maximum(b - 1, 0)], 0, 0))],
            out_specs=pl.BlockSpec((er, d), lambda b, be, nv: (jnp.maximum(b - 1, 0), 0)),
            scratch_shapes=[pltpu.VMEM((2, er, D_EXPERT), BF16)],
        ),
        compiler_params=_cparams(("arbitrary",), VMEM_LIMIT),
        name="moe_experts",
    )(blk_expert, n_valid, x_sorted, w_eg, w_eu, w_ed)

    gt_spec = pl.BlockSpec((1, rm, d), (lambda i, *_: (i // tpg, 0, 0)) if rm == 1
                           else (lambda i, *_: (i // tpg, i % tpg, 0)))
    tok3 = pl.BlockSpec((1, tm, d), lambda i, *_: (i // tpg, i % tpg, 0))
    out = pl.pallas_call(
        functools.partial(_combine_kernel, rb=min(tm, 256)),
        out_shape=jax.ShapeDtypeStruct((g, t, d), F32),
        grid_spec=pltpu.PrefetchScalarGridSpec(
            num_scalar_prefetch=3,
            grid=(n_tiles,),
            in_specs=[pl.BlockSpec(memory_space=pl.ANY),
                      pl.BlockSpec((tm, 8), lambda i, *_: (i, 0)),
                      tok3, gt_spec],
            out_specs=tok3,
            scratch_shapes=[pltpu.VMEM((2, s_loc, d), BF16), pltpu.SemaphoreType.DMA((2,))],
        ),
        compiler_params=_cparams(("arbitrary",), VMEM_LIMIT),
        name="moe_combine",
    )(cntf, loff, goff, d_sorted, info_col, x, gt)
    return out, x_sorted


def _rope_tables(pos):
    half = HEAD_DK // 2
    inv_freq = ROPE_BASE ** (-jnp.arange(half, dtype=F32) / half)
    ang = pos[:, None] * inv_freq[None, :]
    cos, sin = jnp.cos(ang), jnp.sin(ang)
    return jnp.concatenate([cos, cos], axis=1), jnp.concatenate([-sin, sin], axis=1)


def _layer(x, mods, lw, cfg, ret0, gla0, cache_k, cache_v, cosf, sinf, x_sorted):
    g, t, d = x.shape
    b, tseq = cfg["b"], cfg["tseq"]
    sh_m, sc_m, gt_m, sh_f, sc_f, gt_f = mods
    proj, log_a = _in_proj(x, sc_m, sh_m, lw["g_mix"], lw["w_in"], lw["w_ga"], lw["a2"], lw["a_bias"],
                           cosf, sinf, lw["gq"], lw["gk"], cfg["tm"], lw["layer"])
    proj_s = proj.reshape(b, tseq, PROJ_COLS)
    log_a_s = log_a.reshape(b, tseq, BRANCH_WIDTH)
    o_r, ret_new = _retention(proj_s, ret0, lw["g_ret_gn"], cfg["chunk"], cfg["tc"])
    o_g, gla_new = _gla(proj_s, log_a_s, gla0, lw["g_gla_gn"], cfg["chunk"], cfg["tc"])
    if cache_k is None:
        o_a = _band_attn_prompt(proj_s, lw["rel_bias"])
    else:
        o_a = _band_attn_sample(proj_s, cache_k, cache_v, lw["rel_bias"], lw["layer"])
    tok = lambda a: a.reshape(g, t, BRANCH_WIDTH)
    x1, h2 = _merge(x, tok(o_r), tok(o_g), tok(o_a), (sc_m, sh_m, gt_m, sc_f, sh_f), lw["g_mix"], lw["g_ffn"],
                    lw["w_gate"], lw["b_gate"], lw["w_branch"], lw["w_out"], cfg["tm"], lw["layer"])
    h2f = h2.reshape(g * t, d)
    info_row, tbl = _route(h2f, lw["wr_t"], lw["br_t"], cfg["tme"], SEG)
    x2, x_sorted = _moe(tbl[:, :, 0], h2f, x1, gt_f, info_row, info_row.T, lw["w_eg"], lw["w_eu"], lw["w_ed"],
                        x_sorted, cfg["tme"], lw["layer"])
    tail = proj_s[:, -min(tseq, BAND_PAST):]
    k_new = tail[:, :, 9 * BRANCH_WIDTH:10 * BRANCH_WIDTH]
    v_new = tail[:, :, 10 * BRANCH_WIDTH:11 * BRANCH_WIDTH]
    return x2, ret_new, gla_new, k_new, v_new, x_sorted


def kernel(x_prompt, x_sample, state_ret, state_gla, cache_att_k, cache_att_v, c_prompt, c_sample,
           w_ada, b_ada, g_mix, w_in, gla_a2, gla_a_bias, g_ret_gn, g_gla_gn, g_q_att, g_k_att, rel_bias,
           w_branch, w_gate, b_gate, w_out, g_ffn, w_router_group, b_router_group, w_router_exp,
           b_router_exp, w_exp_gate, w_exp_up, w_exp_down):
    depth = w_ada.shape[0]
    bp, seq, d = x_prompt.shape
    bs, dseq, _ = x_sample.shape
    n_s = bs * dseq

    pad = (-(bp + bs)) % 8
    c_all = jnp.concatenate([c_prompt, c_sample, jnp.zeros((pad, d), F32)], axis=0)
    mod = _ada_mod(c_all, w_ada, b_ada)

    ga0 = 8 * BRANCH_WIDTH
    w_in_b = jnp.concatenate([w_in[:, :, :ga0], w_in[:, :, ga0 + GLA_RANK:]], axis=2).astype(BF16)
    w_ga = jnp.pad(w_in[:, :, ga0:ga0 + GLA_RANK], ((0, 0), (0, 0), (0, V7X_LANES - GLA_RANK))).astype(BF16)
    a2 = jnp.pad(gla_a2, ((0, 0), (0, V7X_LANES - GLA_RANK), (0, 0))).astype(BF16)
    wr_t = jnp.zeros((depth, 32, d), F32)
    wr_t = wr_t.at[:, 0:N_GROUPS].set(jnp.swapaxes(w_router_group, 1, 2))
    wr_t = wr_t.at[:, 8:8 + N_EXPERTS].set(jnp.swapaxes(w_router_exp, 1, 2)).astype(BF16)
    br_t = jnp.zeros((depth, 32, 1), F32)
    br_t = br_t.at[:, 0:N_GROUPS, 0].set(b_router_group).at[:, 8:8 + N_EXPERTS, 0].set(b_router_exp)
    w_gate_b, w_branch_b, w_out_b = w_gate.astype(BF16), w_branch.astype(BF16), w_out.astype(BF16)
    w_eg, w_eu, w_ed = w_exp_gate, w_exp_up, w_exp_down

    cos_p, sin_p = _rope_tables(jnp.arange(seq, dtype=F32))
    cos_s, sin_s = _rope_tables(PAST_LEN + jnp.arange(dseq, dtype=F32))
    cos_s, sin_s = jnp.tile(cos_s, (bs, 1)), jnp.tile(sin_s, (bs, 1))

    cfg_p = dict(b=bp, tseq=seq, tm=512, chunk=4 * CHUNK, tc=512, tme=512)
    cfg_s = dict(b=bs, tseq=dseq, tm=n_s, chunk=min(dseq, CHUNK), tc=dseq, tme=n_s)
    zero_state = jnp.zeros((bp, RET_HEADS, HEAD_DK, HEAD_DK), F32)
    sorted_p = jnp.zeros((_sorted_rows(bp * seq, cfg_p["tme"]), d), BF16)
    sorted_s = jnp.zeros((_sorted_rows(n_s, cfg_s["tme"]), d), BF16)

    cache_kt = jnp.transpose(cache_att_k, (0, 1, 3, 4, 2))
    cache_vt = jnp.transpose(cache_att_v, (0, 1, 3, 4, 2))

    xp = x_prompt
    xs = x_sample.reshape(1, n_s, d)
    outs = [[] for _ in range(8)]
    for l in range(depth):
        lw = dict(
            layer=l, g_mix=g_mix[l][None], g_ffn=g_ffn[l][None], w_in=w_in_b, w_ga=w_ga[l], a2=a2[l],
            a_bias=gla_a_bias[l][None], gq=jnp.tile(g_q_att[l], ATT_HEADS)[None],
            gk=jnp.tile(g_k_att[l], ATT_HEADS)[None], g_ret_gn=g_ret_gn[l][None], g_gla_gn=g_gla_gn[l][None],
            rel_bias=rel_bias[l], w_gate=w_gate_b, b_gate=b_gate[l][None], w_branch=w_branch_b,
            w_out=w_out_b, wr_t=wr_t[l], br_t=br_t[l], w_eg=w_eg, w_eu=w_eu, w_ed=w_ed)
        mods_p = tuple(m[:, None, :] for m in jnp.split(mod[l, :bp], 6, axis=-1))
        mods_s = tuple(jnp.repeat(m, dseq, axis=0)[None] for m in jnp.split(mod[l, bp:bp + bs], 6, axis=-1))
        xp, rp, gp, kp, vp, sorted_p = _layer(xp, mods_p, lw, cfg_p, zero_state, zero_state, None, None,
                                              cos_p, sin_p, sorted_p)
        xs, rs, gs, ks, vs, sorted_s = _layer(xs, mods_s, lw, cfg_s, state_ret[l], state_gla[l], cache_kt, cache_vt,
                                              cos_s, sin_s, sorted_s)
        heads = lambda a: a.astype(F32).reshape(a.shape[0], a.shape[1], ATT_HEADS, ATT_HD)
        for lst, val in zip(outs, (rp, gp, heads(kp), heads(vp), rs, gs, heads(ks), heads(vs))):
            lst.append(val)
    return (xp, xs.reshape(bs, dseq, d)) + tuple(jnp.stack(o) for o in outs)
```

```python
import functools

import numpy as np
import jax
import jax.numpy as jnp
from jax import lax
from jax.experimental import pallas as pl
from jax.experimental.pallas import tpu as pltpu

F32 = jnp.float32
BF16 = jnp.bfloat16

D_MODEL = 1024
CHUNK = 64
BRANCH_WIDTH = 512
N_BRANCH = 3
RET_HEADS = 4
HEAD_DK = 128
GLA_RANK = 16
GLA_TAU = 16.0
ATT_HEADS = 8
ATT_HD = 64
BAND_PAST = 512
MAX_REL = 128
N_GROUPS = 4
EXPERTS_PER_GROUP = 4
N_EXPERTS = 16
D_EXPERT = 512
ROPE_BASE = 10000.0
EPS = 1e-6
GN_EPS = 1e-5
PAST_LEN = 1024
N_PROJ_BLOCKS = 11
PROJ_COLS = N_PROJ_BLOCKS * BRANCH_WIDTH

V7X_LANES = 128
V7X_VMEM_BYTES = 64 * 1024 * 1024
VMEM_LIMIT = 56 * 1024 * 1024
NEG_BIG = -1e30
LOG2_E = 1.4426950408889634


def _cparams(sem, vmem=None):
    return pltpu.CompilerParams(dimension_semantics=sem, vmem_limit_bytes=vmem)


def _const_spec(shape):
    nd = len(shape)
    return pl.BlockSpec(shape, lambda *_: (0,) * nd, pipeline_mode=pl.Buffered(1))


def _layer_spec(shape, layer):
    nd = len(shape)
    return pl.BlockSpec((None,) + tuple(shape), lambda *_: (layer,) + (0,) * nd, pipeline_mode=pl.Buffered(1))


def _dot(a, b):
    return jnp.dot(a, b, preferred_element_type=F32)


def _dot_nt(a, b):
    return lax.dot_general(a, b, (((1,), (1,)), ((), ())), preferred_element_type=F32)


def _dot_tn(a, b):
    return lax.dot_general(a, b, (((0,), (0,)), ((), ())), preferred_element_type=F32)


def _silu(x):
    return x * jax.nn.sigmoid(x)


def _norm_mod(x, g, sc, sh):
    ms = jnp.mean(x * x, axis=-1, keepdims=True)
    return x * lax.rsqrt(ms + EPS) * g * (1.0 + sc) + sh


def _ada_kernel(c_ref, w_ref, b_ref, o_ref):
    s = _silu(c_ref[...])
    o_ref[0] = _dot(s.astype(BF16), w_ref[0].astype(BF16)) + b_ref[0]


def _ada_mod(c_all, w_ada, b_ada):
    depth, d, n = w_ada.shape
    rows = c_all.shape[0]
    bn = 1536
    return pl.pallas_call(
        _ada_kernel,
        out_shape=jax.ShapeDtypeStruct((depth, rows, n), F32),
        grid=(depth, n // bn),
        in_specs=[
            pl.BlockSpec((rows, d), lambda l, j: (0, 0)),
            pl.BlockSpec((1, d, bn), lambda l, j: (l, 0, j)),
            pl.BlockSpec((1, 1, bn), lambda l, j: (l, 0, j)),
        ],
        out_specs=pl.BlockSpec((1, rows, bn), lambda l, j: (l, 0, j)),
        compiler_params=_cparams(("arbitrary", "arbitrary"), VMEM_LIMIT),
        name="ada_mod",
    )(c_all, w_ada, b_ada.reshape(depth, 1, n))


def _rope_heads(a, cosf, sinf):
    outs = []
    for h in range(RET_HEADS):
        ah = a[:, h * HEAD_DK:(h + 1) * HEAD_DK]
        outs.append(ah * cosf + pltpu.roll(ah, HEAD_DK // 2, 1) * sinf)
    return jnp.concatenate(outs, axis=1)


def _rms_heads64(a, gain):
    low = lax.broadcasted_iota(jnp.int32, (1, V7X_LANES), 1) < ATT_HD
    outs = []
    for c in range(a.shape[1] // V7X_LANES):
        ac = a[:, c * V7X_LANES:(c + 1) * V7X_LANES]
        sq = ac * ac
        lo = jnp.sum(jnp.where(low, sq, 0.0), axis=-1, keepdims=True)
        hi = jnp.sum(jnp.where(low, 0.0, sq), axis=-1, keepdims=True)
        ms = jnp.where(low, lo, hi) * (1.0 / ATT_HD)
        outs.append(ac * lax.rsqrt(ms + EPS))
    return jnp.concatenate(outs, axis=1) * gain


def _in_proj_kernel(x_ref, sc_ref, sh_ref, g_ref, w_ref, wga_ref, a2_ref, ab_ref, cos_ref, sin_ref,
                    gq_ref, gk_ref, proj_ref, la_ref):
    x = x_ref[0]
    hb = _norm_mod(x, g_ref[...], sc_ref[0], sh_ref[0]).astype(BF16)
    cosf = cos_ref[...]
    sinf = sin_ref[...]
    ga = _dot(hb, wga_ref[...])
    for j in range(N_PROJ_BLOCKS):
        if j == N_PROJ_BLOCKS // 2:
            z = _dot(ga.astype(BF16), a2_ref[...]) + ab_ref[...]
            la_ref[0] = jax.nn.log_sigmoid(z) * (1.0 / GLA_TAU)
        cols = slice(j * BRANCH_WIDTH, (j + 1) * BRANCH_WIDTH)
        acc = _dot(hb, w_ref[:, cols])
        if j == 0:
            acc = _rope_heads(acc, cosf, sinf)
        elif j == 1:
            acc = _rope_heads(acc, cosf, sinf) * (HEAD_DK ** -0.5)
        elif j == 4:
            acc = acc * (HEAD_DK ** -0.5)
        elif j == 8:
            acc = _rms_heads64(acc, gq_ref[...]) * (ATT_HD ** -0.5 * LOG2_E)
        elif j == 9:
            acc = _rms_heads64(acc, gk_ref[...])
        proj_ref[0, :, cols] = acc.astype(BF16)


def _in_proj(x, sc, sh, g_mix, w_in, w_ga, a2, a_bias, cosf, sinf, gq, gk, tm, layer):
    g, t, d = x.shape
    rm = sc.shape[1]
    mod_spec = pl.BlockSpec((1, rm, d), (lambda b, i: (b, 0, 0)) if rm == 1 else (lambda b, i: (b, i, 0)))
    return pl.pallas_call(
        _in_proj_kernel,
        out_shape=(jax.ShapeDtypeStruct((g, t, PROJ_COLS), BF16),
                   jax.ShapeDtypeStruct((g, t, BRANCH_WIDTH), F32)),
        grid=(g, t // tm),
        in_specs=[
            pl.BlockSpec((1, tm, d), lambda b, i: (b, i, 0)),
            mod_spec, mod_spec,
            _const_spec((1, d)),
            _layer_spec((d, PROJ_COLS), layer),
            _const_spec((d, V7X_LANES)),
            _const_spec((V7X_LANES, BRANCH_WIDTH)),
            _const_spec((1, BRANCH_WIDTH)),
            pl.BlockSpec((tm, HEAD_DK), lambda b, i: (i, 0)),
            pl.BlockSpec((tm, HEAD_DK), lambda b, i: (i, 0)),
            _const_spec((1, BRANCH_WIDTH)),
            _const_spec((1, BRANCH_WIDTH)),
        ],
        out_specs=(pl.BlockSpec((1, tm, PROJ_COLS), lambda b, i: (b, i, 0)),
                   pl.BlockSpec((1, tm, BRANCH_WIDTH), lambda b, i: (b, i, 0))),
        compiler_params=_cparams(("arbitrary", "arbitrary"), VMEM_LIMIT),
        name="in_proj",
    )(x, sc, sh, g_mix, w_in, w_ga, a2, a_bias, cosf, sinf, gq, gk)


def _retention_kernel(q_ref, k_ref, v_ref, g_ref, s0_ref, dm_ref, qd_ref, kd_ref, bd_ref, gn_ref,
                      o_ref, so_ref, st_ref, *, chunk, n_chunks):
    n_seq = q_ref.shape[0]

    @pl.when(pl.program_id(1) == 0)
    def _():
        st_ref[...] = s0_ref[...]

    def body(c, carry):
        rows = pl.ds(pl.multiple_of(c * chunk, chunk), chunk)
        for b in range(n_seq):
            for h in range(RET_HEADS):
                cols = slice(h * HEAD_DK, (h + 1) * HEAD_DK)
                q = q_ref[b, rows, cols]
                k = k_ref[b, rows, cols]
                v = v_ref[b, rows, cols]
                scores = _dot_nt(q, k) * dm_ref[h]
                o = _dot(scores.astype(BF16), v)
                o = o + _dot((q.astype(F32) * qd_ref[h]).astype(BF16), st_ref[b, h].astype(BF16))
                st_ref[b, h] = (st_ref[b, h] * bd_ref[h]
                                + _dot_tn((k.astype(F32) * kd_ref[h]).astype(BF16), v))
                mu = jnp.mean(o, axis=-1, keepdims=True)
                oc = o - mu
                var = jnp.mean(oc * oc, axis=-1, keepdims=True)
                on = oc * lax.rsqrt(var + GN_EPS) * gn_ref[:, cols]
                o_ref[b, rows, cols] = (_silu(g_ref[b, rows, cols].astype(F32)) * on).astype(BF16)
        return carry

    lax.fori_loop(0, n_chunks, body, 0)

    @pl.when(pl.program_id(1) == pl.num_programs(1) - 1)
    def _():
        so_ref[...] = st_ref[...]


def _retention_tables(chunk):
    log_gamma = jnp.log(1.0 - jnp.exp2(-5.0 - jnp.arange(RET_HEADS, dtype=F32)))
    idx = jnp.arange(chunk, dtype=F32)
    diff = idx[:, None] - idx[None, :]
    dmask = jnp.where(diff >= 0, jnp.exp(log_gamma[:, None, None] * jnp.maximum(diff, 0.0)), 0.0)
    qd = jnp.exp(log_gamma[:, None] * (idx + 1.0))[:, :, None]
    kd = jnp.exp(log_gamma[:, None] * (chunk - 1.0 - idx))[:, :, None]
    bd = jnp.exp(log_gamma * chunk)[:, None, None]
    bc = lambda a, r: jnp.broadcast_to(a, (RET_HEADS, r, HEAD_DK)).astype(F32)
    return dmask.astype(F32), bc(qd, chunk), bc(kd, chunk), bc(bd, 1)


def _seqs_per_step(b):
    return 4 if b % 4 == 0 else (2 if b % 2 == 0 else 1)


def _retention(proj, state0, gn, chunk, tc):
    b, t, _ = proj.shape
    nb = _seqs_per_step(b)
    dm, qd, kd, bd = _retention_tables(chunk)
    col = lambda j: pl.BlockSpec((nb, tc, BRANCH_WIDTH), lambda i, s, j=j: (i, s, j))
    st_spec = pl.BlockSpec((nb, RET_HEADS, HEAD_DK, HEAD_DK), lambda i, s: (i, 0, 0, 0))
    return pl.pallas_call(
        functools.partial(_retention_kernel, chunk=chunk, n_chunks=tc // chunk),
        out_shape=(jax.ShapeDtypeStruct((b, t, BRANCH_WIDTH), BF16),
                   jax.ShapeDtypeStruct((b, RET_HEADS, HEAD_DK, HEAD_DK), F32)),
        grid=(b // nb, t // tc),
        in_specs=[col(0), col(1), col(2), col(3), st_spec,
                  _const_spec(dm.shape), _const_spec(qd.shape), _const_spec(kd.shape), _const_spec(bd.shape),
                  _const_spec((1, BRANCH_WIDTH))],
        out_specs=(pl.BlockSpec((nb, tc, BRANCH_WIDTH), lambda i, s: (i, s, 0)), st_spec),
        scratch_shapes=[pltpu.VMEM((nb, RET_HEADS, HEAD_DK, HEAD_DK), F32)],
        compiler_params=_cparams(("arbitrary", "arbitrary")),
        name="retention",
    )(proj, proj, proj, proj, state0, dm, qd, kd, bd, gn)


GLA_SAFE_LOG_DECAY = -60.0


def _gla_intra_direct(q, bh, rows_ref, chunk):
    trow = lax.broadcasted_iota(jnp.int32, (chunk, 1), 0)

    def step(s, acc):
        bs = rows_ref[0, pl.ds(s, 1), :]
        ks = rows_ref[1, pl.ds(s, 1), :]
        vs = rows_ref[2, pl.ds(s, 1), :]
        e = jnp.exp(jnp.minimum(bh - bs, 0.0))
        col = jnp.sum(q * ks * e, axis=-1, keepdims=True)
        return acc + jnp.where(trow >= s, col, 0.0) * vs

    return lax.fori_loop(0, chunk, step, jnp.zeros((chunk, HEAD_DK), F32))


def _gla_kernel(q_ref, k_ref, v_ref, r_ref, la_ref, s0_ref, tri_ref, gn_ref, o_ref, so_ref, st_ref, rows_ref,
                *, chunk, n_chunks):
    n_seq = q_ref.shape[0]

    @pl.when(pl.program_id(1) == 0)
    def _():
        for b in range(n_seq):
            for h in range(RET_HEADS):
                st_ref[b, h] = s0_ref[b, h].T

    tri = tri_ref[...]
    causal = (lax.broadcasted_iota(jnp.int32, (chunk, chunk), 0)
              >= lax.broadcasted_iota(jnp.int32, (chunk, chunk), 1))

    def chunk_body(factored):
        def body(c, carry):
            rows = pl.ds(pl.multiple_of(c * chunk, chunk), chunk)
            for b in range(n_seq):
                la = la_ref[b, rows, :]
                la_hi = la.astype(BF16)
                la_lo = (la - la_hi.astype(F32)).astype(BF16)
                b_all = _dot(tri, la_hi) + _dot(tri, la_lo)
                for h in range(RET_HEADS):
                    cols = slice(h * HEAD_DK, (h + 1) * HEAD_DK)
                    q = q_ref[b, rows, cols].astype(F32)
                    k = k_ref[b, rows, cols].astype(F32)
                    v = v_ref[b, rows, cols]
                    bh = b_all[:, cols]
                    bl = bh[chunk - 1:chunk, :]
                    qh = (q * jnp.exp(bh)).astype(BF16)
                    if factored:
                        kh = (k * jnp.exp(-bh)).astype(BF16)
                        scores = jnp.where(causal, _dot_nt(qh, kh), 0.0)
                        o = _dot(scores.astype(BF16), v)
                    else:
                        rows_ref[0] = bh
                        rows_ref[1] = k
                        rows_ref[2] = v.astype(F32)
                        o = _gla_intra_direct(q, bh, rows_ref, chunk)
                    o = o + _dot_nt(qh, st_ref[b, h].astype(BF16))
                    kd = (k * jnp.exp(bl - bh)).astype(BF16)
                    st_ref[b, h] = st_ref[b, h] * jnp.exp(bl) + _dot_tn(v, kd)
                    ms = jnp.mean(o * o, axis=-1, keepdims=True)
                    on = o * lax.rsqrt(ms + EPS) * gn_ref[:, cols]
                    o_ref[b, rows, cols] = (_silu(r_ref[b, rows, cols].astype(F32)) * on).astype(BF16)
            return carry
        return body

    lowest = None
    for b in range(n_seq):
        for c in range(n_chunks):
            tot = jnp.sum(la_ref[b, c * chunk:(c + 1) * chunk, :], axis=0, keepdims=True)
            lowest = tot if lowest is None else jnp.minimum(lowest, tot)
    safe = jnp.min(lowest) > GLA_SAFE_LOG_DECAY

    @pl.when(safe)
    def _():
        lax.fori_loop(0, n_chunks, chunk_body(True), 0)

    @pl.when(jnp.logical_not(safe))
    def _():
        lax.fori_loop(0, n_chunks, chunk_body(False), 0)

    @pl.when(pl.program_id(1) == pl.num_programs(1) - 1)
    def _():
        for b in range(n_seq):
            for h in range(RET_HEADS):
                so_ref[b, h] = st_ref[b, h].T


def _gla(proj, log_a, state0, gn, chunk, tc):
    b, t, _ = proj.shape
    nb = _seqs_per_step(b)
    tri = jnp.tril(jnp.ones((chunk, chunk), F32)).astype(BF16)
    col = lambda j: pl.BlockSpec((nb, tc, BRANCH_WIDTH), lambda i, s, j=j: (i, s, j))
    st_spec = pl.BlockSpec((nb, RET_HEADS, HEAD_DK, HEAD_DK), lambda i, s: (i, 0, 0, 0))
    return pl.pallas_call(
        functools.partial(_gla_kernel, chunk=chunk, n_chunks=tc // chunk),
        out_shape=(jax.ShapeDtypeStruct((b, t, BRANCH_WIDTH), BF16),
                   jax.ShapeDtypeStruct((b, RET_HEADS, HEAD_DK, HEAD_DK), F32)),
        grid=(b // nb, t // tc),
        in_specs=[col(4), col(5), col(6), col(7),
                  pl.BlockSpec((nb, tc, BRANCH_WIDTH), lambda i, s: (i, s, 0)),
                  st_spec, _const_spec((chunk, chunk)), _const_spec((1, BRANCH_WIDTH))],
        out_specs=(pl.BlockSpec((nb, tc, BRANCH_WIDTH), lambda i, s: (i, s, 0)), st_spec),
        scratch_shapes=[pltpu.VMEM((nb, RET_HEADS, HEAD_DK, HEAD_DK), F32),
                        pltpu.VMEM((3, chunk, HEAD_DK), F32)],
        compiler_params=_cparams(("arbitrary", "arbitrary")),
        name="gla",
    )(proj, proj, proj, proj, log_a, state0, tri, gn)


def _attn_stage_scratch(rb, win):
    pairs = ATT_HEADS // 2
    return [pltpu.VMEM((2, pairs, 2 * rb, win), F32), pltpu.VMEM((2, pairs, 2 * rb, win), BF16)]


def _band_attn_kernel(q_ref, kp_ref, kc_ref, vp_ref, vc_ref, bias_ref, o_ref, kw_ref, vw_ref,
                      s_ref, e_ref,
                      *, n_sub, rb, win, prev_rows):
    cur_rows = kc_ref.shape[1]
    kw_ref[0:prev_rows, :] = kp_ref[0]
    kw_ref[prev_rows:prev_rows + cur_rows, :] = kc_ref[0]
    pairs = ATT_HEADS // 2
    for p in range(pairs):
        cols = slice(p * V7X_LANES, (p + 1) * V7X_LANES)
        vw_ref[0:prev_rows, 2 * p * V7X_LANES:(2 * p + 1) * V7X_LANES] = vp_ref[0, :, cols]
        vw_ref[prev_rows:prev_rows + cur_rows, 2 * p * V7X_LANES:(2 * p + 1) * V7X_LANES] = vc_ref[0, :, cols]
        vw_ref[:, (2 * p + 1) * V7X_LANES:(2 * p + 2) * V7X_LANES] = jnp.ones((vw_ref.shape[0], V7X_LANES), BF16)
    low = lax.broadcasted_iota(jnp.int32, (1, V7X_LANES), 1) < ATT_HD
    q0 = pl.program_id(1) * cur_rows

    def scores(i, masked):
        if masked:
            valid = lax.broadcasted_iota(jnp.int32, (1, win), 1) >= prev_rows - q0 - i * rb
        for p in range(pairs):
            cols = slice(p * V7X_LANES, (p + 1) * V7X_LANES)
            q2 = q_ref[0, i * rb:(i + 1) * rb, cols]
            zero = jnp.zeros_like(q2)
            qst = jnp.concatenate([jnp.where(low, q2, zero), jnp.where(low, zero, q2)], axis=0)
            s = _dot_nt(qst, kw_ref[i * rb:i * rb + win, cols]) + bias_ref[p]
            if masked:
                s = jnp.where(valid, s, NEG_BIG)
            s_ref[i % 2, p] = s

    def softmax(i):
        for p in range(pairs):
            s = s_ref[i % 2, p]
            e_ref[i % 2, p] = jnp.exp2(s - jnp.max(s, axis=-1, keepdims=True)).astype(BF16)

    def values(i):
        for p in range(pairs):
            cols = slice(p * V7X_LANES, (p + 1) * V7X_LANES)
            od = _dot(e_ref[i % 2, p], vw_ref[i * rb:i * rb + win, 2 * p * V7X_LANES:2 * (p + 1) * V7X_LANES])
            o2 = od[:, :V7X_LANES] / od[:, V7X_LANES:]
            o_ref[0, i * rb:(i + 1) * rb, cols] = jnp.where(low, o2[:rb], o2[rb:]).astype(BF16)

    def pipeline(masked):
        for step in range(n_sub + 2):
            if step < n_sub:
                scores(step, masked)
            if 0 <= step - 1 < n_sub:
                softmax(step - 1)
            if 0 <= step - 2 < n_sub:
                values(step - 2)

    pl.when(q0 < prev_rows)(functools.partial(pipeline, True))
    pl.when(q0 >= prev_rows)(functools.partial(pipeline, False))


def _rel_bias_table(rel_bias, rb, win, chunk):
    period = win + rb
    u = np.arange(period)
    u = np.where(u < win, u, u - period)
    vec = rel_bias.astype(F32)[:, np.clip(BAND_PAST - u, -MAX_REL, MAX_REL) + MAX_REL]
    bias = jnp.tile(vec, (1, rb))[:, :rb * (period - 1)].reshape(-1, rb, period - 1)[:, :, :win]
    r = np.arange(rb)[:, None]
    j = np.arange(win)[None, :]
    c0 = (r // chunk) * chunk
    allowed = (j >= c0) & (j < c0 + BAND_PAST + chunk)
    bias = jnp.where(jnp.asarray(allowed)[None], bias * LOG2_E, NEG_BIG)
    return bias.reshape(ATT_HEADS // 2, 2 * rb, win)


def _band_attn_prompt(proj, rel_bias):
    b, t, _ = proj.shape
    qb, rb = 512, 128
    win = BAND_PAST + rb
    bias = _rel_bias_table(rel_bias, rb, win, CHUNK)
    cur = lambda j: pl.BlockSpec((1, qb, BRANCH_WIDTH), lambda i, s, j=j: (i, s, j))
    prev = lambda j: pl.BlockSpec((1, qb, BRANCH_WIDTH), lambda i, s, j=j: (i, jnp.maximum(s - 1, 0), j))
    return pl.pallas_call(
        functools.partial(_band_attn_kernel, n_sub=qb // rb, rb=rb, win=win, prev_rows=BAND_PAST),
        out_shape=jax.ShapeDtypeStruct((b, t, BRANCH_WIDTH), BF16),
        grid=(b, t // qb),
        in_specs=[cur(8), prev(9), cur(9), prev(10), cur(10), _const_spec(bias.shape)],
        out_specs=pl.BlockSpec((1, qb, BRANCH_WIDTH), lambda i, s: (i, s, 0)),
        scratch_shapes=[pltpu.VMEM((BAND_PAST + qb, BRANCH_WIDTH), BF16),
                        pltpu.VMEM((BAND_PAST + qb, 2 * BRANCH_WIDTH), BF16)] + _attn_stage_scratch(rb, win),
        compiler_params=_cparams(("arbitrary", "arbitrary")),
        name="band_attn",
    )(proj, proj, proj, proj, proj, bias)


def _band_attn_sample_kernel(q_ref, kn_ref, vn_ref, kc_ref, vc_ref, bias_ref, o_ref):
    t = q_ref.shape[1]
    past = kc_ref.shape[-1]
    low = lax.broadcasted_iota(jnp.int32, (1, V7X_LANES), 1) < ATT_HD
    for p in range(ATT_HEADS // 2):
        cols = slice(p * V7X_LANES, (p + 1) * V7X_LANES)
        q2 = q_ref[0, :, cols]
        zero = jnp.zeros_like(q2)
        qst = jnp.concatenate([jnp.where(low, q2, zero), jnp.where(low, zero, q2)], axis=0)
        kt = kc_ref[0, 2 * p:2 * p + 2].reshape(V7X_LANES, past).astype(BF16)
        vt = vc_ref[0, 2 * p:2 * p + 2].reshape(V7X_LANES, past).astype(BF16)
        s = jnp.concatenate([_dot(qst, kt), _dot_nt(qst, kn_ref[0, :, cols])], axis=1) + bias_ref[p]
        m = jnp.max(s, axis=-1, keepdims=True)
        e = jnp.exp2(s - m)
        den = jnp.sum(e, axis=-1, keepdims=True)
        eb = e.astype(BF16)
        o2 = (_dot_nt(eb[:, :past], vt) + _dot(eb[:, past:], vn_ref[0, :, cols])) / den
        o_ref[0, :, cols] = jnp.where(low, o2[:t], o2[t:]).astype(BF16)


def _band_attn_sample(proj, cache_kt, cache_vt, rel_bias, layer):
    b, t, _ = proj.shape
    win = BAND_PAST + t
    bias = _rel_bias_table(rel_bias, t, win, t)
    cur = lambda j: pl.BlockSpec((1, t, BRANCH_WIDTH), lambda i, j=j: (i, 0, j))
    cache = pl.BlockSpec((None, 1, ATT_HEADS, ATT_HD, BAND_PAST), lambda i: (layer, i, 0, 0, 0))
    return pl.pallas_call(
        _band_attn_sample_kernel,
        out_shape=jax.ShapeDtypeStruct((b, t, BRANCH_WIDTH), BF16),
        grid=(b,),
        in_specs=[cur(8), cur(9), cur(10), cache, cache, _const_spec(bias.shape)],
        out_specs=pl.BlockSpec((1, t, BRANCH_WIDTH), lambda i: (i, 0, 0)),
        compiler_params=_cparams(("arbitrary",)),
        name="band_attn_sample",
    )(proj, proj, proj, cache_kt, cache_vt, bias)


def _merge_kernel(x_ref, or_ref, og_ref, oa_ref, scm_ref, shm_ref, gtm_ref, scf_ref, shf_ref,
                  gmix_ref, gffn_ref, wg_ref, bg_ref, wb_ref, wo_ref, xo_ref, h2_ref):
    x = x_ref[0]
    d = x.shape[1]
    hb = _norm_mod(x, gmix_ref[...], scm_ref[0], shm_ref[0]).astype(BF16)
    merged = None
    for n, o_ref in enumerate((or_ref, og_ref, oa_ref)):
        gate = jax.nn.sigmoid(_dot(hb, wg_ref[:, n * d:(n + 1) * d]) + bg_ref[:, n * d:(n + 1) * d])
        y = gate * _dot(o_ref[0], wb_ref[n])
        merged = y if merged is None else merged + y
    mix = _dot(merged.astype(BF16), wo_ref[...])
    xn = x + gtm_ref[0] * mix
    xo_ref[0] = xn
    h2_ref[0] = _norm_mod(xn, gffn_ref[...], scf_ref[0], shf_ref[0]).astype(BF16)


def _merge(x, o_r, o_g, o_a, mods, g_mix, g_ffn, w_gate, b_gate, w_branch, w_out, tm, layer):
    g, t, d = x.shape
    rm = mods[0].shape[1]
    mod_spec = pl.BlockSpec((1, rm, d), (lambda b, i: (b, 0, 0)) if rm == 1 else (lambda b, i: (b, i, 0)))
    tok = lambda w: pl.BlockSpec((1, tm, w), lambda b, i: (b, i, 0))
    return pl.pallas_call(
        _merge_kernel,
        out_shape=(jax.ShapeDtypeStruct((g, t, d), F32), jax.ShapeDtypeStruct((g, t, d), BF16)),
        grid=(g, t // tm),
        in_specs=[tok(d), tok(BRANCH_WIDTH), tok(BRANCH_WIDTH), tok(BRANCH_WIDTH)] + [mod_spec] * 5 + [
            _const_spec((1, d)), _const_spec((1, d)),
            _layer_spec((d, N_BRANCH * d), layer), _const_spec((1, N_BRANCH * d)),
            _layer_spec((N_BRANCH, BRANCH_WIDTH, d), layer), _layer_spec((d, d), layer)],
        out_specs=(tok(d), tok(d)),
        compiler_params=_cparams(("arbitrary", "arbitrary"), VMEM_LIMIT),
        name="merge",
    )(x, o_r, o_g, o_a, *mods, g_mix, g_ffn, w_gate, b_gate, w_branch, w_out)


def _first_argmax(vals, n):
    row = lax.broadcasted_iota(jnp.int32, vals.shape, 0).astype(F32)
    m = jnp.max(vals, axis=0, keepdims=True)
    idx = jnp.min(jnp.where(vals == m, row, float(n)), axis=0, keepdims=True)
    return m, idx


def _route_kernel(h_ref, wr_ref, br_ref, up_ref, l16_ref, info_ref, tbl_ref, *, rows_per_block):
    tm = h_ref.shape[0]
    lg = _dot_nt(wr_ref[...], h_ref[...]) + br_ref[...]
    g = lg[0:N_GROUPS]
    gmax, grp = _first_argmax(g, N_GROUPS)
    p_group = 1.0 / jnp.sum(jnp.exp(g - gmax), axis=0, keepdims=True)
    esel = jnp.zeros((EXPERTS_PER_GROUP, tm), F32)
    for gi in range(N_GROUPS):
        blk = lg[8 + gi * EXPERTS_PER_GROUP:8 + (gi + 1) * EXPERTS_PER_GROUP]
        esel = esel + jnp.where(grp == float(gi), blk, 0.0)
    v1, i1 = _first_argmax(esel, EXPERTS_PER_GROUP)
    row4 = lax.broadcasted_iota(jnp.int32, esel.shape, 0).astype(F32)
    v2, i2 = _first_argmax(jnp.where(row4 == i1, -jnp.inf, esel), EXPERTS_PER_GROUP)
    e21 = jnp.exp(v2 - v1)
    w1 = p_group / (1.0 + e21)
    w2 = p_group * e21 / (1.0 + e21)
    e1 = grp * float(EXPERTS_PER_GROUP) + i1
    e2 = grp * float(EXPERTS_PER_GROUP) + i2
    row16 = lax.broadcasted_iota(jnp.int32, (N_EXPERTS, tm), 0).astype(F32)
    hit1 = row16 == e1
    hit2 = row16 == e2
    onehot = jnp.where(hit1 | hit2, 1.0, 0.0)
    prefix = _dot(onehot.astype(BF16), up_ref[...])
    cnt = jnp.sum(onehot, axis=1, keepdims=True)
    nblk = jnp.floor((cnt + float(rows_per_block - 1)) * (1.0 / rows_per_block))
    nblk_b = jnp.broadcast_to(nblk, (N_EXPERTS, V7X_LANES))
    offb = _dot(l16_ref[...], nblk_b.astype(BF16))[:, 0:1]
    base = offb * float(rows_per_block) + prefix
    info_ref[0:1, :] = jnp.sum(jnp.where(hit1, base, 0.0), axis=0, keepdims=True)
    info_ref[1:2, :] = jnp.sum(jnp.where(hit2, base, 0.0), axis=0, keepdims=True)
    info_ref[2:3, :] = w1
    info_ref[3:4, :] = w2
    info_ref[4:8, :] = jnp.zeros((4, tm), F32)
    tbl_ref[0] = nblk_b.astype(jnp.int32)


def _route(h2, wr_t, br_t, tm, rows_per_block):
    n, d = h2.shape
    nt = n // tm
    upper = jnp.triu(jnp.ones((tm, tm), F32), 1).astype(BF16)
    l16 = jnp.tril(jnp.ones((N_EXPERTS, N_EXPERTS), F32), -1).astype(BF16)
    return pl.pallas_call(
        functools.partial(_route_kernel, rows_per_block=rows_per_block),
        out_shape=(jax.ShapeDtypeStruct((8, n), F32),
                   jax.ShapeDtypeStruct((nt, N_EXPERTS, V7X_LANES), jnp.int32)),
        grid=(nt,),
        in_specs=[pl.BlockSpec((tm, d), lambda i: (i, 0)),
                  _const_spec((32, d)), _const_spec((32, 1)),
                  _const_spec((tm, tm)), _const_spec((N_EXPERTS, N_EXPERTS))],
        out_specs=(pl.BlockSpec((8, tm), lambda i: (0, i)),
                   pl.BlockSpec((1, N_EXPERTS, V7X_LANES), lambda i: (i, 0, 0))),
        compiler_params=_cparams(("arbitrary",)),
        name="route",
    )(h2, wr_t, br_t, upper, l16)


SEG = 16
GATHER_ROWS = 256
MAX_EXPERT_ROWS = 512


def _expert_rows(n_tokens):
    mean_rows = 2 * n_tokens // N_EXPERTS
    return int(min(MAX_EXPERT_ROWS, max(4 * SEG, 1 << (mean_rows // 4).bit_length())))


def _segment_copies(cnt_ref, loff_ref, goff_ref, tile, local_ref, global_ref, sem, to_global):
    for e in range(N_EXPERTS):
        cnt = cnt_ref[tile * N_EXPERTS + e]
        rows = pl.multiple_of(cnt * SEG, SEG)
        loc = local_ref.at[pl.ds(pl.multiple_of(loff_ref[tile * N_EXPERTS + e] * SEG, SEG), rows)]
        glo = global_ref.at[pl.ds(pl.multiple_of(goff_ref[tile * N_EXPERTS + e] * SEG, SEG), rows)]
        cp = pltpu.make_async_copy(loc, glo, sem) if to_global else pltpu.make_async_copy(glo, loc, sem)
        yield cnt > 0, cp


def _tile_units(cnt_ref, tile):
    total = jnp.int32(0)
    for e in range(N_EXPERTS):
        total = total + cnt_ref[tile * N_EXPERTS + e]
    return total


def _wait_segments(cnt_ref, tile, local_ref, global_ref, sem, to_global):
    rows = pl.multiple_of(_tile_units(cnt_ref, tile) * SEG, SEG)
    loc = local_ref.at[pl.ds(0, rows)]
    glo = global_ref.at[pl.ds(0, rows)]
    cp = pltpu.make_async_copy(loc, glo, sem) if to_global else pltpu.make_async_copy(glo, loc, sem)
    pl.when(rows > 0)(cp.wait)


def _sort_kernel(cnt_ref, loff_ref, goff_ref, h_ref, irow_ref, xin_ref, xout_ref, xs2_ref, sem2):
    del xin_ref
    i = pl.program_id(0)
    last = pl.num_programs(0) - 1
    slot = lax.rem(i, 2)
    xs_ref = xs2_ref.at[slot]
    sem = sem2.at[slot]

    def wait_tile(tile, s):
        _wait_segments(cnt_ref, tile, xs2_ref.at[s], xout_ref, sem2.at[s], True)

    @pl.when(i >= 2)
    def _():
        wait_tile(i - 2, slot)

    gr = GATHER_ROWS
    n_gather = (_tile_units(cnt_ref, i) * SEG + gr - 1) // gr
    pos1 = irow_ref[0:1, :]
    pos2 = irow_ref[1:2, :]
    h = h_ref[...]

    def gather(gb, carry):
        r0 = pl.multiple_of(gb * gr, gr)
        srow = (lax.broadcasted_iota(jnp.int32, (gr, 1), 0) + r0).astype(F32)
        sel = jnp.where((srow == pos1) | (srow == pos2), 1.0, 0.0).astype(BF16)
        xs_ref[pl.ds(r0, gr), :] = _dot(sel, h).astype(BF16)
        return carry

    lax.fori_loop(0, n_gather, gather, 0)
    for pred, cp in _segment_copies(cnt_ref, loff_ref, goff_ref, i, xs_ref, xout_ref, sem, True):
        pl.when(pred)(cp.start)

    @pl.when(i == last)
    def _():
        @pl.when(i >= 1)
        def _():
            wait_tile(i - 1, 1 - slot)

        wait_tile(i, slot)


def _experts_kernel(be_ref, nv_ref, x_ref, wg_ref, wu_ref, wd_ref, o_ref, a_ref):
    del be_ref
    b = pl.program_id(0)
    n_valid = nv_ref[0]
    slot = lax.rem(b, 2)

    def up(s):
        xb = x_ref[...]
        a = _silu(_dot(xb, wg_ref[0].astype(BF16))) * _dot(xb, wu_ref[0].astype(BF16))
        a_ref[s] = a.astype(BF16)

    def down(s):
        o_ref[...] = _dot(a_ref[s], wd_ref[0].astype(BF16)).astype(BF16)

    @pl.when(b == 0)
    def _():
        up(slot)

    @pl.when((b >= 1) & (b < n_valid))
    def _():
        down(1 - slot)
        up(slot)

    @pl.when((b >= 1) & (b == n_valid))
    def _():
        down(1 - slot)

    @pl.when(b > n_valid)
    def _():
        o_ref[...] = jnp.zeros(o_ref.shape, BF16)


def _combine_kernel(cnt_ref, loff_ref, goff_ref, ds_ref, icol_ref, x_ref, gt_ref, o_ref, dl2_ref, sem2, *, rb):
    i = pl.program_id(0)
    tm = x_ref.shape[1]
    s_loc = dl2_ref.shape[1]
    slot = lax.rem(i, 2)

    def fetch_tile(tile, s):
        for pred, cp in _segment_copies(cnt_ref, loff_ref, goff_ref, tile, dl2_ref.at[s], ds_ref, sem2.at[s], False):
            pl.when(pred)(cp.start)

        def zero_tail(u, carry):
            dl2_ref[s, pl.ds(pl.multiple_of(u * SEG, SEG), SEG), :] = jnp.zeros((SEG, dl2_ref.shape[2]), BF16)
            return carry

        lax.fori_loop(_tile_units(cnt_ref, tile), s_loc // SEG, zero_tail, 0)

    @pl.when(i == 0)
    def _():
        fetch_tile(0, 0)

    @pl.when(i + 1 < pl.num_programs(0))
    def _():
        fetch_tile(i + 1, 1 - slot)

    _wait_segments(cnt_ref, i, dl2_ref.at[slot], ds_ref, sem2.at[slot], False)
    scol = lax.broadcasted_iota(jnp.int32, (1, s_loc), 1).astype(F32)
    for r in range(tm // rb):
        rows = slice(r * rb, (r + 1) * rb)
        cmb = (jnp.where(scol == icol_ref[rows, 0:1], icol_ref[rows, 2:3], 0.0)
               + jnp.where(scol == icol_ref[rows, 1:2], icol_ref[rows, 3:4], 0.0))
        y = _dot(cmb.astype(BF16), dl2_ref[slot])
        gt = gt_ref[0] if gt_ref.shape[1] == 1 else gt_ref[0, rows, :]
        o_ref[0, rows, :] = x_ref[0, rows, :] + gt * y


def _sorted_rows(n_tokens, tm):
    n_tiles = n_tokens // tm
    er = _expert_rows(n_tokens)
    rows = 2 * n_tokens + n_tiles * N_EXPERTS * (SEG - 1) + N_EXPERTS * (er - 1)
    return -(-rows // er) * er


def _moe(cnt, h2, x, gt, info_row, info_col, w_eg, w_eu, w_ed, x_sorted, tm, layer):
    g, t, d = x.shape
    n = g * t
    n_tiles = n // tm
    tpg = t // tm
    rm = gt.shape[1]
    er = _expert_rows(n)
    n_blocks = x_sorted.shape[0] // er
    s_loc = -(-(2 * tm + N_EXPERTS * (SEG - 1)) // GATHER_ROWS) * GATHER_ROWS
    upb = er // SEG
    reg_blk = (jnp.sum(cnt, axis=0) + upb - 1) // upb
    blk_end = jnp.cumsum(reg_blk)
    goff = ((blk_end - reg_blk)[None, :] * upb + jnp.cumsum(cnt, axis=0) - cnt).reshape(-1)
    loff = (jnp.cumsum(cnt, axis=1) - cnt).reshape(-1)
    cntf = cnt.reshape(-1)
    n_valid = blk_end[-1:]
    blk_expert = jnp.minimum(
        jnp.sum(jnp.arange(n_blocks, dtype=jnp.int32)[:, None] >= blk_end[None, :], axis=1), N_EXPERTS - 1
    ).astype(jnp.int32)

    x_sorted = pl.pallas_call(
        _sort_kernel,
        out_shape=jax.ShapeDtypeStruct(x_sorted.shape, BF16),
        grid_spec=pltpu.PrefetchScalarGridSpec(
            num_scalar_prefetch=3,
            grid=(n_tiles,),
            in_specs=[pl.BlockSpec((tm, d), lambda i, *_: (i, 0)),
                      pl.BlockSpec((8, tm), lambda i, *_: (0, i)),
                      pl.BlockSpec(memory_space=pl.ANY)],
            out_specs=pl.BlockSpec(memory_space=pl.ANY),
            scratch_shapes=[pltpu.VMEM((2, s_loc, d), BF16), pltpu.SemaphoreType.DMA((2,))],
        ),
        input_output_aliases={5: 0},
        compiler_params=_cparams(("arbitrary",)),
        name="moe_sort",
    )(cntf, loff, goff, h2, info_row, x_sorted)

    blk = lambda b, be, nv: (jnp.minimum(b, nv[0] - 1), 0)
    last = n_blocks - 1
    d_sorted = pl.pallas_call(
        _experts_kernel,
        out_shape=jax.ShapeDtypeStruct(x_sorted.shape, BF16),
        grid_spec=pltpu.PrefetchScalarGridSpec(
            num_scalar_prefetch=2,
            grid=(n_blocks + 1,),
            in_specs=[pl.BlockSpec((er, d), blk),
                      pl.BlockSpec((None, 1, d, D_EXPERT), lambda b, be, nv: (layer, be[jnp.minimum(b, last)], 0, 0)),
                      pl.BlockSpec((None, 1, d, D_EXPERT), lambda b, be, nv: (layer, be[jnp.minimum(b, last)], 0, 0)),
                      pl.BlockSpec((None, 1, D_EXPERT, d), lambda b, be, nv: (layer, be[jnp.maximum(b - 1, 0)], 0, 0))],
            out_specs=pl.BlockSpec((er, d), lambda b, be, nv: (jnp.maximum(b - 1, 0), 0)),
            scratch_shapes=[pltpu.VMEM((2, er, D_EXPERT), BF16)],
        ),
        compiler_params=_cparams(("arbitrary",), VMEM_LIMIT),
        name="moe_experts",
    )(blk_expert, n_valid, x_sorted, w_eg, w_eu, w_ed)

    gt_spec = pl.BlockSpec((1, rm, d), (lambda i, *_: (i // tpg, 0, 0)) if rm == 1
                           else (lambda i, *_: (i // tpg, i % tpg, 0)))
    tok3 = pl.BlockSpec((1, tm, d), lambda i, *_: (i // tpg, i % tpg, 0))
    out = pl.pallas_call(
        functools.partial(_combine_kernel, rb=min(tm, 256)),
        out_shape=jax.ShapeDtypeStruct((g, t, d), F32),
        grid_spec=pltpu.PrefetchScalarGridSpec(
            num_scalar_prefetch=3,
            grid=(n_tiles,),
            in_specs=[pl.BlockSpec(memory_space=pl.ANY),
                      pl.BlockSpec((tm, 8), lambda i, *_: (i, 0)),
                      tok3, gt_spec],
            out_specs=tok3,
            scratch_shapes=[pltpu.VMEM((2, s_loc, d), BF16), pltpu.SemaphoreType.DMA((2,))],
        ),
        compiler_params=_cparams(("arbitrary",), VMEM_LIMIT),
        name="moe_combine",
    )(cntf, loff, goff, d_sorted, info_col, x, gt)
    return out, x_sorted


def _rope_tables(pos):
    half = HEAD_DK // 2
    inv_freq = ROPE_BASE ** (-jnp.arange(half, dtype=F32) / half)
    ang = pos[:, None] * inv_freq[None, :]
    cos, sin = jnp.cos(ang), jnp.sin(ang)
    return jnp.concatenate([cos, cos], axis=1), jnp.concatenate([-sin, sin], axis=1)


def _layer(x, mods, lw, cfg, ret0, gla0, cache_k, cache_v, cosf, sinf, x_sorted):
    g, t, d = x.shape
    b, tseq = cfg["b"], cfg["tseq"]
    sh_m, sc_m, gt_m, sh_f, sc_f, gt_f = mods
    proj, log_a = _in_proj(x, sc_m, sh_m, lw["g_mix"], lw["w_in"], lw["w_ga"], lw["a2"], lw["a_bias"],
                           cosf, sinf, lw["gq"], lw["gk"], cfg["tm"], lw["layer"])
    proj_s = proj.reshape(b, tseq, PROJ_COLS)
    log_a_s = log_a.reshape(b, tseq, BRANCH_WIDTH)
    o_r, ret_new = _retention(proj_s, ret0, lw["g_ret_gn"], cfg["chunk"], cfg["tc"])
    o_g, gla_new = _gla(proj_s, log_a_s, gla0, lw["g_gla_gn"], cfg["chunk"], cfg["tc"])
    if cache_k is None:
        o_a = _band_attn_prompt(proj_s, lw["rel_bias"])
    else:
        o_a = _band_attn_sample(proj_s, cache_k, cache_v, lw["rel_bias"], lw["layer"])
    tok = lambda a: a.reshape(g, t, BRANCH_WIDTH)
    x1, h2 = _merge(x, tok(o_r), tok(o_g), tok(o_a), (sc_m, sh_m, gt_m, sc_f, sh_f), lw["g_mix"], lw["g_ffn"],
                    lw["w_gate"], lw["b_gate"], lw["w_branch"], lw["w_out"], cfg["tm"], lw["layer"])
    h2f = h2.reshape(g * t, d)
    info_row, tbl = _route(h2f, lw["wr_t"], lw["br_t"], cfg["tme"], SEG)
    x2, x_sorted = _moe(tbl[:, :, 0], h2f, x1, gt_f, info_row, info_row.T, lw["w_eg"], lw["w_eu"], lw["w_ed"],
                        x_sorted, cfg["tme"], lw["layer"])
    tail = proj_s[:, -min(tseq, BAND_PAST):]
    k_new = tail[:, :, 9 * BRANCH_WIDTH:10 * BRANCH_WIDTH]
    v_new = tail[:, :, 10 * BRANCH_WIDTH:11 * BRANCH_WIDTH]
    return x2, ret_new, gla_new, k_new, v_new, x_sorted


def kernel(x_prompt, x_sample, state_ret, state_gla, cache_att_k, cache_att_v, c_prompt, c_sample,
           w_ada, b_ada, g_mix, w_in, gla_a2, gla_a_bias, g_ret_gn, g_gla_gn, g_q_att, g_k_att, rel_bias,
           w_branch, w_gate, b_gate, w_out, g_ffn, w_router_group, b_router_group, w_router_exp,
           b_router_exp, w_exp_gate, w_exp_up, w_exp_down):
    depth = w_ada.shape[0]
    bp, seq, d = x_prompt.shape
    bs, dseq, _ = x_sample.shape
    n_s = bs * dseq

    pad = (-(bp + bs)) % 8
    c_all = jnp.concatenate([c_prompt, c_sample, jnp.zeros((pad, d), F32)], axis=0)
    mod = _ada_mod(c_all, w_ada, b_ada)

    ga0 = 8 * BRANCH_WIDTH
    w_in_b = jnp.concatenate([w_in[:, :, :ga0], w_in[:, :, ga0 + GLA_RANK:]], axis=2).astype(BF16)
    w_ga = jnp.pad(w_in[:, :, ga0:ga0 + GLA_RANK], ((0, 0), (0, 0), (0, V7X_LANES - GLA_RANK))).astype(BF16)
    a2 = jnp.pad(gla_a2, ((0, 0), (0, V7X_LANES - GLA_RANK), (0, 0))).astype(BF16)
    wr_t = jnp.zeros((depth, 32, d), F32)
    wr_t = wr_t.at[:, 0:N_GROUPS].set(jnp.swapaxes(w_router_group, 1, 2))
    wr_t = wr_t.at[:, 8:8 + N_EXPERTS].set(jnp.swapaxes(w_router_exp, 1, 2)).astype(BF16)
    br_t = jnp.zeros((depth, 32, 1), F32)
    br_t = br_t.at[:, 0:N_GROUPS, 0].set(b_router_group).at[:, 8:8 + N_EXPERTS, 0].set(b_router_exp)
    w_gate_b, w_branch_b, w_out_b = w_gate.astype(BF16), w_branch.astype(BF16), w_out.astype(BF16)
    w_eg, w_eu, w_ed = w_exp_gate, w_exp_up, w_exp_down

    cos_p, sin_p = _rope_tables(jnp.arange(seq, dtype=F32))
    cos_s, sin_s = _rope_tables(PAST_LEN + jnp.arange(dseq, dtype=F32))
    cos_s, sin_s = jnp.tile(cos_s, (bs, 1)), jnp.tile(sin_s, (bs, 1))

    cfg_p = dict(b=bp, tseq=seq, tm=512, chunk=4 * CHUNK, tc=512, tme=512)
    cfg_s = dict(b=bs, tseq=dseq, tm=n_s, chunk=min(dseq, CHUNK), tc=dseq, tme=n_s)
    zero_state = jnp.zeros((bp, RET_HEADS, HEAD_DK, HEAD_DK), F32)
    sorted_p = jnp.zeros((_sorted_rows(bp * seq, cfg_p["tme"]), d), BF16)
    sorted_s = jnp.zeros((_sorted_rows(n_s, cfg_s["tme"]), d), BF16)

    cache_kt = jnp.transpose(cache_att_k, (0, 1, 3, 4, 2))
    cache_vt = jnp.transpose(cache_att_v, (0, 1, 3, 4, 2))

    xp = x_prompt
    xs = x_sample.reshape(1, n_s, d)
    outs = [[] for _ in range(8)]
    for l in range(depth):
        lw = dict(
            layer=l, g_mix=g_mix[l][None], g_ffn=g_ffn[l][None], w_in=w_in_b, w_ga=w_ga[l], a2=a2[l],
            a_bias=gla_a_bias[l][None], gq=jnp.tile(g_q_att[l], ATT_HEADS)[None],
            gk=jnp.tile(g_k_att[l], ATT_HEADS)[None], g_ret_gn=g_ret_gn[l][None], g_gla_gn=g_gla_gn[l][None],
            rel_bias=rel_bias[l], w_gate=w_gate_b, b_gate=b_gate[l][None], w_branch=w_branch_b,
            w_out=w_out_b, wr_t=wr_t[l], br_t=br_t[l], w_eg=w_eg, w_eu=w_eu, w_ed=w_ed)
        mods_p = tuple(m[:, None, :] for m in jnp.split(mod[l, :bp], 6, axis=-1))
        mods_s = tuple(jnp.repeat(m, dseq, axis=0)[None] for m in jnp.split(mod[l, bp:bp + bs], 6, axis=-1))
        xp, rp, gp, kp, vp, sorted_p = _layer(xp, mods_p, lw, cfg_p, zero_state, zero_state, None, None,
                                              cos_p, sin_p, sorted_p)
        xs, rs, gs, ks, vs, sorted_s = _layer(xs, mods_s, lw, cfg_s, state_ret[l], state_gla[l], cache_kt, cache_vt,
                                              cos_s, sin_s, sorted_s)
        heads = lambda a: a.astype(F32).reshape(a.shape[0], a.shape[1], ATT_HEADS, ATT_HD)
        for lst, val in zip(outs, (rp, gp, heads(kp), heads(vp), rs, gs, heads(ks), heads(vs))):
            lst.append(val)
    return (xp, xs.reshape(bs, dseq, d)) + tuple(jnp.stack(o) for o in outs)
```

```python
import functools

import numpy as np
import jax
import jax.numpy as jnp
from jax import lax
from jax.experimental import pallas as pl
from jax.experimental.pallas import tpu as pltpu

F32 = jnp.float32
BF16 = jnp.bfloat16

D_MODEL = 1024
CHUNK = 64
BRANCH_WIDTH = 512
N_BRANCH = 3
RET_HEADS = 4
HEAD_DK = 128
GLA_RANK = 16
GLA_TAU = 16.0
ATT_HEADS = 8
ATT_HD = 64
BAND_PAST = 512
MAX_REL = 128
N_GROUPS = 4
EXPERTS_PER_GROUP = 4
N_EXPERTS = 16
D_EXPERT = 512
ROPE_BASE = 10000.0
EPS = 1e-6
GN_EPS = 1e-5
PAST_LEN = 1024
N_PROJ_BLOCKS = 11
PROJ_COLS = N_PROJ_BLOCKS * BRANCH_WIDTH

V7X_LANES = 128
V7X_VMEM_BYTES = 64 * 1024 * 1024
VMEM_LIMIT = 56 * 1024 * 1024
NEG_BIG = -1e30
LOG2_E = 1.4426950408889634


def _cparams(sem, vmem=None):
    return pltpu.CompilerParams(dimension_semantics=sem, vmem_limit_bytes=vmem)


def _const_spec(shape):
    nd = len(shape)
    return pl.BlockSpec(shape, lambda *_: (0,) * nd, pipeline_mode=pl.Buffered(1))


def _layer_spec(shape, layer):
    nd = len(shape)
    return pl.BlockSpec((None,) + tuple(shape), lambda *_: (layer,) + (0,) * nd, pipeline_mode=pl.Buffered(1))


def _dot(a, b):
    return jnp.dot(a, b, preferred_element_type=F32)


def _dot_nt(a, b):
    return lax.dot_general(a, b, (((1,), (1,)), ((), ())), preferred_element_type=F32)


def _dot_tn(a, b):
    return lax.dot_general(a, b, (((0,), (0,)), ((), ())), preferred_element_type=F32)


def _silu(x):
    return x * jax.nn.sigmoid(x)


def _norm_mod(x, g, sc, sh):
    ms = jnp.mean(x * x, axis=-1, keepdims=True)
    return x * lax.rsqrt(ms + EPS) * g * (1.0 + sc) + sh


def _ada_kernel(c_ref, w_ref, b_ref, o_ref):
    s = _silu(c_ref[...])
    o_ref[0] = _dot(s.astype(BF16), w_ref[0].astype(BF16)) + b_ref[0]


def _ada_mod(c_all, w_ada, b_ada):
    depth, d, n = w_ada.shape
    rows = c_all.shape[0]
    bn = 1536
    return pl.pallas_call(
        _ada_kernel,
        out_shape=jax.ShapeDtypeStruct((depth, rows, n), F32),
        grid=(depth, n // bn),
        in_specs=[
            pl.BlockSpec((rows, d), lambda l, j: (0, 0)),
            pl.BlockSpec((1, d, bn), lambda l, j: (l, 0, j)),
            pl.BlockSpec((1, 1, bn), lambda l, j: (l, 0, j)),
        ],
        out_specs=pl.BlockSpec((1, rows, bn), lambda l, j: (l, 0, j)),
        compiler_params=_cparams(("arbitrary", "arbitrary"), VMEM_LIMIT),
        name="ada_mod",
    )(c_all, w_ada, b_ada.reshape(depth, 1, n))


def _rope_heads(a, cosf, sinf):
    outs = []
    for h in range(RET_HEADS):
        ah = a[:, h * HEAD_DK:(h + 1) * HEAD_DK]
        outs.append(ah * cosf + pltpu.roll(ah, HEAD_DK // 2, 1) * sinf)
    return jnp.concatenate(outs, axis=1)


def _rms_heads64(a, gain):
    low = lax.broadcasted_iota(jnp.int32, (1, V7X_LANES), 1) < ATT_HD
    outs = []
    for c in range(a.shape[1] // V7X_LANES):
        ac = a[:, c * V7X_LANES:(c + 1) * V7X_LANES]
        sq = ac * ac
        lo = jnp.sum(jnp.where(low, sq, 0.0), axis=-1, keepdims=True)
        hi = jnp.sum(jnp.where(low, 0.0, sq), axis=-1, keepdims=True)
        ms = jnp.where(low, lo, hi) * (1.0 / ATT_HD)
        outs.append(ac * lax.rsqrt(ms + EPS))
    return jnp.concatenate(outs, axis=1) * gain


def _in_proj_kernel(x_ref, sc_ref, sh_ref, g_ref, w_ref, wga_ref, a2_ref, ab_ref, cos_ref, sin_ref,
                    gq_ref, gk_ref, proj_ref, la_ref):
    x = x_ref[0]
    hb = _norm_mod(x, g_ref[...], sc_ref[0], sh_ref[0]).astype(BF16)
    cosf = cos_ref[...]
    sinf = sin_ref[...]
    ga = _dot(hb, wga_ref[...])
    for j in range(N_PROJ_BLOCKS):
        if j == N_PROJ_BLOCKS // 2:
            z = _dot(ga.astype(BF16), a2_ref[...]) + ab_ref[...]
            la_ref[0] = jax.nn.log_sigmoid(z) * (1.0 / GLA_TAU)
        cols = slice(j * BRANCH_WIDTH, (j + 1) * BRANCH_WIDTH)
        acc = _dot(hb, w_ref[:, cols])
        if j == 0:
            acc = _rope_heads(acc, cosf, sinf)
        elif j == 1:
            acc = _rope_heads(acc, cosf, sinf) * (HEAD_DK ** -0.5)
        elif j == 4:
            acc = acc * (HEAD_DK ** -0.5)
        elif j == 8:
            acc = _rms_heads64(acc, gq_ref[...]) * (ATT_HD ** -0.5 * LOG2_E)
        elif j == 9:
            acc = _rms_heads64(acc, gk_ref[...])
        proj_ref[0, :, cols] = acc.astype(BF16)


def _in_proj(x, sc, sh, g_mix, w_in, w_ga, a2, a_bias, cosf, sinf, gq, gk, tm, layer):
    g, t, d = x.shape
    rm = sc.shape[1]
    mod_spec = pl.BlockSpec((1, rm, d), (lambda b, i: (b, 0, 0)) if rm == 1 else (lambda b, i: (b, i, 0)))
    return pl.pallas_call(
        _in_proj_kernel,
        out_shape=(jax.ShapeDtypeStruct((g, t, PROJ_COLS), BF16),
                   jax.ShapeDtypeStruct((g, t, BRANCH_WIDTH), F32)),
        grid=(g, t // tm),
        in_specs=[
            pl.BlockSpec((1, tm, d), lambda b, i: (b, i, 0)),
            mod_spec, mod_spec,
            _const_spec((1, d)),
            _layer_spec((d, PROJ_COLS), layer),
            _const_spec((d, V7X_LANES)),
            _const_spec((V7X_LANES, BRANCH_WIDTH)),
            _const_spec((1, BRANCH_WIDTH)),
            pl.BlockSpec((tm, HEAD_DK), lambda b, i: (i, 0)),
            pl.BlockSpec((tm, HEAD_DK), lambda b, i: (i, 0)),
            _const_spec((1, BRANCH_WIDTH)),
            _const_spec((1, BRANCH_WIDTH)),
        ],
        out_specs=(pl.BlockSpec((1, tm, PROJ_COLS), lambda b, i: (b, i, 0)),
                   pl.BlockSpec((1, tm, BRANCH_WIDTH), lambda b, i: (b, i, 0))),
        compiler_params=_cparams(("arbitrary", "arbitrary"), VMEM_LIMIT),
        name="in_proj",
    )(x, sc, sh, g_mix, w_in, w_ga, a2, a_bias, cosf, sinf, gq, gk)


def _retention_kernel(q_ref, k_ref, v_ref, g_ref, s0_ref, dm_ref, qd_ref, kd_ref, bd_ref, gn_ref,
                      o_ref, so_ref, st_ref, *, chunk, n_chunks):
    n_seq = q_ref.shape[0]

    @pl.when(pl.program_id(1) == 0)
    def _():
        st_ref[...] = s0_ref[...]

    def body(c, carry):
        rows = pl.ds(pl.multiple_of(c * chunk, chunk), chunk)
        for b in range(n_seq):
            for h in range(RET_HEADS):
                cols = slice(h * HEAD_DK, (h + 1) * HEAD_DK)
                q = q_ref[b, rows, cols]
                k = k_ref[b, rows, cols]
                v = v_ref[b, rows, cols]
                scores = _dot_nt(q, k) * dm_ref[h]
                o = _dot(scores.astype(BF16), v)
                o = o + _dot((q.astype(F32) * qd_ref[h]).astype(BF16), st_ref[b, h].astype(BF16))
                st_ref[b, h] = (st_ref[b, h] * bd_ref[h]
                                + _dot_tn((k.astype(F32) * kd_ref[h]).astype(BF16), v))
                mu = jnp.mean(o, axis=-1, keepdims=True)
                oc = o - mu
                var = jnp.mean(oc * oc, axis=-1, keepdims=True)
                on = oc * lax.rsqrt(var + GN_EPS) * gn_ref[:, cols]
                o_ref[b, rows, cols] = (_silu(g_ref[b, rows, cols].astype(F32)) * on).astype(BF16)
        return carry

    lax.fori_loop(0, n_chunks, body, 0)

    @pl.when(pl.program_id(1) == pl.num_programs(1) - 1)
    def _():
        so_ref[...] = st_ref[...]


def _retention_tables(chunk):
    log_gamma = jnp.log(1.0 - jnp.exp2(-5.0 - jnp.arange(RET_HEADS, dtype=F32)))
    idx = jnp.arange(chunk, dtype=F32)
    diff = idx[:, None] - idx[None, :]
    dmask = jnp.where(diff >= 0, jnp.exp(log_gamma[:, None, None] * jnp.maximum(diff, 0.0)), 0.0)
    qd = jnp.exp(log_gamma[:, None] * (idx + 1.0))[:, :, None]
    kd = jnp.exp(log_gamma[:, None] * (chunk - 1.0 - idx))[:, :, None]
    bd = jnp.exp(log_gamma * chunk)[:, None, None]
    bc = lambda a, r: jnp.broadcast_to(a, (RET_HEADS, r, HEAD_DK)).astype(F32)
    return dmask.astype(F32), bc(qd, chunk), bc(kd, chunk), bc(bd, 1)


def _seqs_per_step(b):
    return 4 if b % 4 == 0 else (2 if b % 2 == 0 else 1)


def _retention(proj, state0, gn, chunk, tc):
    b, t, _ = proj.shape
    nb = _seqs_per_step(b)
    dm, qd, kd, bd = _retention_tables(chunk)
    col = lambda j: pl.BlockSpec((nb, tc, BRANCH_WIDTH), lambda i, s, j=j: (i, s, j))
    st_spec = pl.BlockSpec((nb, RET_HEADS, HEAD_DK, HEAD_DK), lambda i, s: (i, 0, 0, 0))
    return pl.pallas_call(
        functools.partial(_retention_kernel, chunk=chunk, n_chunks=tc // chunk),
        out_shape=(jax.ShapeDtypeStruct((b, t, BRANCH_WIDTH), BF16),
                   jax.ShapeDtypeStruct((b, RET_HEADS, HEAD_DK, HEAD_DK), F32)),
        grid=(b // nb, t // tc),
        in_specs=[col(0), col(1), col(2), col(3), st_spec,
                  _const_spec(dm.shape), _const_spec(qd.shape), _const_spec(kd.shape), _const_spec(bd.shape),
                  _const_spec((1, BRANCH_WIDTH))],
        out_specs=(pl.BlockSpec((nb, tc, BRANCH_WIDTH), lambda i, s: (i, s, 0)), st_spec),
        scratch_shapes=[pltpu.VMEM((nb, RET_HEADS, HEAD_DK, HEAD_DK), F32)],
        compiler_params=_cparams(("arbitrary", "arbitrary")),
        name="retention",
    )(proj, proj, proj, proj, state0, dm, qd, kd, bd, gn)


GLA_SAFE_LOG_DECAY = -60.0


def _gla_intra_direct(q, bh, rows_ref, chunk):
    trow = lax.broadcasted_iota(jnp.int32, (chunk, 1), 0)

    def step(s, acc):
        bs = rows_ref[0, pl.ds(s, 1), :]
        ks = rows_ref[1, pl.ds(s, 1), :]
        vs = rows_ref[2, pl.ds(s, 1), :]
        e = jnp.exp(jnp.minimum(bh - bs, 0.0))
        col = jnp.sum(q * ks * e, axis=-1, keepdims=True)
        return acc + jnp.where(trow >= s, col, 0.0) * vs

    return lax.fori_loop(0, chunk, step, jnp.zeros((chunk, HEAD_DK), F32))


def _gla_kernel(q_ref, k_ref, v_ref, r_ref, la_ref, s0_ref, tri_ref, gn_ref, o_ref, so_ref, st_ref, rows_ref,
                *, chunk, n_chunks):
    n_seq = q_ref.shape[0]

    @pl.when(pl.program_id(1) == 0)
    def _():
        for b in range(n_seq):
            for h in range(RET_HEADS):
                st_ref[b, h] = s0_ref[b, h].T

    tri = tri_ref[...]
    causal = (lax.broadcasted_iota(jnp.int32, (chunk, chunk), 0)
              >= lax.broadcasted_iota(jnp.int32, (chunk, chunk), 1))

    def chunk_body(factored):
        def body(c, carry):
            rows = pl.ds(pl.multiple_of(c * chunk, chunk), chunk)
            for b in range(n_seq):
                la = la_ref[b, rows, :]
                la_hi = la.astype(BF16)
                la_lo = (la - la_hi.astype(F32)).astype(BF16)
                b_all = _dot(tri, la_hi) + _dot(tri, la_lo)
                for h in range(RET_HEADS):
                    cols = slice(h * HEAD_DK, (h + 1) * HEAD_DK)
                    q = q_ref[b, rows, cols].astype(F32)
                    k = k_ref[b, rows, cols].astype(F32)
                    v = v_ref[b, rows, cols]
                    bh = b_all[:, cols]
                    bl = bh[chunk - 1:chunk, :]
                    qh = (q * jnp.exp(bh)).astype(BF16)
                    if factored:
                        kh = (k * jnp.exp(-bh)).astype(BF16)
                        scores = jnp.where(causal, _dot_nt(qh, kh), 0.0)
                        o = _dot(scores.astype(BF16), v)
                    else:
                        rows_ref[0] = bh
                        rows_ref[1] = k
                        rows_ref[2] = v.astype(F32)
                        o = _gla_intra_direct(q, bh, rows_ref, chunk)
                    o = o + _dot_nt(qh, st_ref[b, h].astype(BF16))
                    kd = (k * jnp.exp(bl - bh)).astype(BF16)
                    st_ref[b, h] = st_ref[b, h] * jnp.exp(bl) + _dot_tn(v, kd)
                    ms = jnp.mean(o * o, axis=-1, keepdims=True)
                    on = o * lax.rsqrt(ms + EPS) * gn_ref[:, cols]
                    o_ref[b, rows, cols] = (_silu(r_ref[b, rows, cols].astype(F32)) * on).astype(BF16)
            return carry
        return body

    lowest = None
    for b in range(n_seq):
        for c in range(n_chunks):
            tot = jnp.sum(la_ref[b, c * chunk:(c + 1) * chunk, :], axis=0, keepdims=True)
            lowest = tot if lowest is None else jnp.minimum(lowest, tot)
    safe = jnp.min(lowest) > GLA_SAFE_LOG_DECAY

    @pl.when(safe)
    def _():
        lax.fori_loop(0, n_chunks, chunk_body(True), 0)

    @pl.when(jnp.logical_not(safe))
    def _():
        lax.fori_loop(0, n_chunks, chunk_body(False), 0)

    @pl.when(pl.program_id(1) == pl.num_programs(1) - 1)
    def _():
        for b in range(n_seq):
            for h in range(RET_HEADS):
                so_ref[b, h] = st_ref[b, h].T


def _gla(proj, log_a, state0, gn, chunk, tc):
    b, t, _ = proj.shape
    nb = _seqs_per_step(b)
    tri = jnp.tril(jnp.ones((chunk, chunk), F32)).astype(BF16)
    col = lambda j: pl.BlockSpec((nb, tc, BRANCH_WIDTH), lambda i, s, j=j: (i, s, j))
    st_spec = pl.BlockSpec((nb, RET_HEADS, HEAD_DK, HEAD_DK), lambda i, s: (i, 0, 0, 0))
    return pl.pallas_call(
        functools.partial(_gla_kernel, chunk=chunk, n_chunks=tc // chunk),
        out_shape=(jax.ShapeDtypeStruct((b, t, BRANCH_WIDTH), BF16),
                   jax.ShapeDtypeStruct((b, RET_HEADS, HEAD_DK, HEAD_DK), F32)),
        grid=(b // nb, t // tc),
        in_specs=[col(4), col(5), col(6), col(7),
                  pl.BlockSpec((nb, tc, BRANCH_WIDTH), lambda i, s: (i, s, 0)),
                  st_spec, _const_spec((chunk, chunk)), _const_spec((1, BRANCH_WIDTH))],
        out_specs=(pl.BlockSpec((nb, tc, BRANCH_WIDTH), lambda i, s: (i, s, 0)), st_spec),
        scratch_shapes=[pltpu.VMEM((nb, RET_HEADS, HEAD_DK, HEAD_DK), F32),
                        pltpu.VMEM((3, chunk, HEAD_DK), F32)],
        compiler_params=_cparams(("arbitrary", "arbitrary")),
        name="gla",
    )(proj, proj, proj, proj, log_a, state0, tri, gn)


def _attn_stage_scratch(rb, win):
    pairs = ATT_HEADS // 2
    return [pltpu.VMEM((2, pairs, 2 * rb, win), F32), pltpu.VMEM((2, pairs, 2 * rb, win), BF16)]


def _band_attn_kernel(q_ref, kp_ref, kc_ref, vp_ref, vc_ref, bias_ref, o_ref, kw_ref, vw_ref,
                      s_ref, e_ref,
                      *, n_sub, rb, win, prev_rows):
    cur_rows = kc_ref.shape[1]
    kw_ref[0:prev_rows, :] = kp_ref[0]
    kw_ref[prev_rows:prev_rows + cur_rows, :] = kc_ref[0]
    pairs = ATT_HEADS // 2
    for p in range(pairs):
        cols = slice(p * V7X_LANES, (p + 1) * V7X_LANES)
        vw_ref[0:prev_rows, 2 * p * V7X_LANES:(2 * p + 1) * V7X_LANES] = vp_ref[0, :, cols]
        vw_ref[prev_rows:prev_rows + cur_rows, 2 * p * V7X_LANES:(2 * p + 1) * V7X_LANES] = vc_ref[0, :, cols]
        vw_ref[:, (2 * p + 1) * V7X_LANES:(2 * p + 2) * V7X_LANES] = jnp.ones((vw_ref.shape[0], V7X_LANES), BF16)
    low = lax.broadcasted_iota(jnp.int32, (1, V7X_LANES), 1) < ATT_HD
    q0 = pl.program_id(1) * cur_rows

    def scores(i, masked):
        if masked:
            valid = lax.broadcasted_iota(jnp.int32, (1, win), 1) >= prev_rows - q0 - i * rb
        for p in range(pairs):
            cols = slice(p * V7X_LANES, (p + 1) * V7X_LANES)
            q2 = q_ref[0, i * rb:(i + 1) * rb, cols]
            zero = jnp.zeros_like(q2)
            qst = jnp.concatenate([jnp.where(low, q2, zero), jnp.where(low, zero, q2)], axis=0)
            s = _dot_nt(qst, kw_ref[i * rb:i * rb + win, cols]) + bias_ref[p]
            if masked:
                s = jnp.where(valid, s, NEG_BIG)
            s_ref[i % 2, p] = s

    def softmax(i):
        for p in range(pairs):
            s = s_ref[i % 2, p]
            e_ref[i % 2, p] = jnp.exp2(s - jnp.max(s, axis=-1, keepdims=True)).astype(BF16)

    def values(i):
        for p in range(pairs):
            cols = slice(p * V7X_LANES, (p + 1) * V7X_LANES)
            od = _dot(e_ref[i % 2, p], vw_ref[i * rb:i * rb + win, 2 * p * V7X_LANES:2 * (p + 1) * V7X_LANES])
            o2 = od[:, :V7X_LANES] / od[:, V7X_LANES:]
            o_ref[0, i * rb:(i + 1) * rb, cols] = jnp.where(low, o2[:rb], o2[rb:]).astype(BF16)

    def pipeline(masked):
        for step in range(n_sub + 2):
            if step < n_sub:
                scores(step, masked)
            if 0 <= step - 1 < n_sub:
                softmax(step - 1)
            if 0 <= step - 2 < n_sub:
                values(step - 2)

    pl.when(q0 < prev_rows)(functools.partial(pipeline, True))
    pl.when(q0 >= prev_rows)(functools.partial(pipeline, False))


def _rel_bias_table(rel_bias, rb, win, chunk):
    period = win + rb
    u = np.arange(period)
    u = np.where(u < win, u, u - period)
    vec = rel_bias.astype(F32)[:, np.clip(BAND_PAST - u, -MAX_REL, MAX_REL) + MAX_REL]
    bias = jnp.tile(vec, (1, rb))[:, :rb * (period - 1)].reshape(-1, rb, period - 1)[:, :, :win]
    r = np.arange(rb)[:, None]
    j = np.arange(win)[None, :]
    c0 = (r // chunk) * chunk
    allowed = (j >= c0) & (j < c0 + BAND_PAST + chunk)
    bias = jnp.where(jnp.asarray(allowed)[None], bias * LOG2_E, NEG_BIG)
    return bias.reshape(ATT_HEADS // 2, 2 * rb, win)


def _band_attn_prompt(proj, rel_bias):
    b, t, _ = proj.shape
    qb, rb = 512, 128
    win = BAND_PAST + rb
    bias = _rel_bias_table(rel_bias, rb, win, CHUNK)
    cur = lambda j: pl.BlockSpec((1, qb, BRANCH_WIDTH), lambda i, s, j=j: (i, s, j))
    prev = lambda j: pl.BlockSpec((1, qb, BRANCH_WIDTH), lambda i, s, j=j: (i, jnp.maximum(s - 1, 0), j))
    return pl.pallas_call(
        functools.partial(_band_attn_kernel, n_sub=qb // rb, rb=rb, win=win, prev_rows=BAND_PAST),
        out_shape=jax.ShapeDtypeStruct((b, t, BRANCH_WIDTH), BF16),
        grid=(b, t // qb),
        in_specs=[cur(8), prev(9), cur(9), prev(10), cur(10), _const_spec(bias.shape)],
        out_specs=pl.BlockSpec((1, qb, BRANCH_WIDTH), lambda i, s: (i, s, 0)),
        scratch_shapes=[pltpu.VMEM((BAND_PAST + qb, BRANCH_WIDTH), BF16),
                        pltpu.VMEM((BAND_PAST + qb, 2 * BRANCH_WIDTH), BF16)] + _attn_stage_scratch(rb, win),
        compiler_params=_cparams(("arbitrary", "arbitrary")),
        name="band_attn",
    )(proj, proj, proj, proj, proj, bias)


def _band_attn_sample_kernel(q_ref, kn_ref, vn_ref, kc_ref, vc_ref, bias_ref, o_ref):
    t = q_ref.shape[1]
    past = kc_ref.shape[-1]
    low = lax.broadcasted_iota(jnp.int32, (1, V7X_LANES), 1) < ATT_HD
    for p in range(ATT_HEADS // 2):
        cols = slice(p * V7X_LANES, (p + 1) * V7X_LANES)
        q2 = q_ref[0, :, cols]
        zero = jnp.zeros_like(q2)
        qst = jnp.concatenate([jnp.where(low, q2, zero), jnp.where(low, zero, q2)], axis=0)
        kt = kc_ref[0, 2 * p:2 * p + 2].reshape(V7X_LANES, past).astype(BF16)
        vt = vc_ref[0, 2 * p:2 * p + 2].reshape(V7X_LANES, past).astype(BF16)
        s = jnp.concatenate([_dot(qst, kt), _dot_nt(qst, kn_ref[0, :, cols])], axis=1) + bias_ref[p]
        m = jnp.max(s, axis=-1, keepdims=True)
        e = jnp.exp2(s - m)
        den = jnp.sum(e, axis=-1, keepdims=True)
        eb = e.astype(BF16)
        o2 = (_dot_nt(eb[:, :past], vt) + _dot(eb[:, past:], vn_ref[0, :, cols])) / den
        o_ref[0, :, cols] = jnp.where(low, o2[:t], o2[t:]).astype(BF16)


def _band_attn_sample(proj, cache_kt, cache_vt, rel_bias, layer):
    b, t, _ = proj.shape
    win = BAND_PAST + t
    bias = _rel_bias_table(rel_bias, t, win, t)
    cur = lambda j: pl.BlockSpec((1, t, BRANCH_WIDTH), lambda i, j=j: (i, 0, j))
    cache = pl.BlockSpec((None, 1, ATT_HEADS, ATT_HD, BAND_PAST), lambda i: (layer, i, 0, 0, 0))
    return pl.pallas_call(
        _band_attn_sample_kernel,
        out_shape=jax.ShapeDtypeStruct((b, t, BRANCH_WIDTH), BF16),
        grid=(b,),
        in_specs=[cur(8), cur(9), cur(10), cache, cache, _const_spec(bias.shape)],
        out_specs=pl.BlockSpec((1, t, BRANCH_WIDTH), lambda i: (i, 0, 0)),
        compiler_params=_cparams(("arbitrary",)),
        name="band_attn_sample",
    )(proj, proj, proj, cache_kt, cache_vt, bias)


def _merge_kernel(x_ref, or_ref, og_ref, oa_ref, scm_ref, shm_ref, gtm_ref, scf_ref, shf_ref,
                  gmix_ref, gffn_ref, wg_ref, bg_ref, wb_ref, wo_ref, xo_ref, h2_ref):
    x = x_ref[0]
    d = x.shape[1]
    hb = _norm_mod(x, gmix_ref[...], scm_ref[0], shm_ref[0]).astype(BF16)
    merged = None
    for n, o_ref in enumerate((or_ref, og_ref, oa_ref)):
        gate = jax.nn.sigmoid(_dot(hb, wg_ref[:, n * d:(n + 1) * d]) + bg_ref[:, n * d:(n + 1) * d])
        y = gate * _dot(o_ref[0], wb_ref[n])
        merged = y if merged is None else merged + y
    mix = _dot(merged.astype(BF16), wo_ref[...])
    xn = x + gtm_ref[0] * mix
    xo_ref[0] = xn
    h2_ref[0] = _norm_mod(xn, gffn_ref[...], scf_ref[0], shf_ref[0]).astype(BF16)


def _merge(x, o_r, o_g, o_a, mods, g_mix, g_ffn, w_gate, b_gate, w_branch, w_out, tm, layer):
    g, t, d = x.shape
    rm = mods[0].shape[1]
    mod_spec = pl.BlockSpec((1, rm, d), (lambda b, i: (b, 0, 0)) if rm == 1 else (lambda b, i: (b, i, 0)))
    tok = lambda w: pl.BlockSpec((1, tm, w), lambda b, i: (b, i, 0))
    return pl.pallas_call(
        _merge_kernel,
        out_shape=(jax.ShapeDtypeStruct((g, t, d), F32), jax.ShapeDtypeStruct((g, t, d), BF16)),
        grid=(g, t // tm),
        in_specs=[tok(d), tok(BRANCH_WIDTH), tok(BRANCH_WIDTH), tok(BRANCH_WIDTH)] + [mod_spec] * 5 + [
            _const_spec((1, d)), _const_spec((1, d)),
            _layer_spec((d, N_BRANCH * d), layer), _const_spec((1, N_BRANCH * d)),
            _layer_spec((N_BRANCH, BRANCH_WIDTH, d), layer), _layer_spec((d, d), layer)],
        out_specs=(tok(d), tok(d)),
        compiler_params=_cparams(("arbitrary", "arbitrary"), VMEM_LIMIT),
        name="merge",
    )(x, o_r, o_g, o_a, *mods, g_mix, g_ffn, w_gate, b_gate, w_branch, w_out)


def _first_argmax(vals, n):
    row = lax.broadcasted_iota(jnp.int32, vals.shape, 0).astype(F32)
    m = jnp.max(vals, axis=0, keepdims=True)
    idx = jnp.min(jnp.where(vals == m, row, float(n)), axis=0, keepdims=True)
    return m, idx


def _route_kernel(h_ref, wr_ref, br_ref, up_ref, l16_ref, info_ref, tbl_ref, *, rows_per_block):
    tm = h_ref.shape[0]
    lg = _dot_nt(wr_ref[...], h_ref[...]) + br_ref[...]
    g = lg[0:N_GROUPS]
    gmax, grp = _first_argmax(g, N_GROUPS)
    p_group = 1.0 / jnp.sum(jnp.exp(g - gmax), axis=0, keepdims=True)
    esel = jnp.zeros((EXPERTS_PER_GROUP, tm), F32)
    for gi in range(N_GROUPS):
        blk = lg[8 + gi * EXPERTS_PER_GROUP:8 + (gi + 1) * EXPERTS_PER_GROUP]
        esel = esel + jnp.where(grp == float(gi), blk, 0.0)
    v1, i1 = _first_argmax(esel, EXPERTS_PER_GROUP)
    row4 = lax.broadcasted_iota(jnp.int32, esel.shape, 0).astype(F32)
    v2, i2 = _first_argmax(jnp.where(row4 == i1, -jnp.inf, esel), EXPERTS_PER_GROUP)
    e21 = jnp.exp(v2 - v1)
    w1 = p_group / (1.0 + e21)
    w2 = p_group * e21 / (1.0 + e21)
    e1 = grp * float(EXPERTS_PER_GROUP) + i1
    e2 = grp * float(EXPERTS_PER_GROUP) + i2
    row16 = lax.broadcasted_iota(jnp.int32, (N_EXPERTS, tm), 0).astype(F32)
    hit1 = row16 == e1
    hit2 = row16 == e2
    onehot = jnp.where(hit1 | hit2, 1.0, 0.0)
    prefix = _dot(onehot.astype(BF16), up_ref[...])
    cnt = jnp.sum(onehot, axis=1, keepdims=True)
    nblk = jnp.floor((cnt + float(rows_per_block - 1)) * (1.0 / rows_per_block))
    nblk_b = jnp.broadcast_to(nblk, (N_EXPERTS, V7X_LANES))
    offb = _dot(l16_ref[...], nblk_b.astype(BF16))[:, 0:1]
    base = offb * float(rows_per_block) + prefix
    info_ref[0:1, :] = jnp.sum(jnp.where(hit1, base, 0.0), axis=0, keepdims=True)
    info_ref[1:2, :] = jnp.sum(jnp.where(hit2, base, 0.0), axis=0, keepdims=True)
    info_ref[2:3, :] = w1
    info_ref[3:4, :] = w2
    info_ref[4:8, :] = jnp.zeros((4, tm), F32)
    tbl_ref[0] = nblk_b.astype(jnp.int32)


def _route(h2, wr_t, br_t, tm, rows_per_block):
    n, d = h2.shape
    nt = n // tm
    upper = jnp.triu(jnp.ones((tm, tm), F32), 1).astype(BF16)
    l16 = jnp.tril(jnp.ones((N_EXPERTS, N_EXPERTS), F32), -1).astype(BF16)
    return pl.pallas_call(
        functools.partial(_route_kernel, rows_per_block=rows_per_block),
        out_shape=(jax.ShapeDtypeStruct((8, n), F32),
                   jax.ShapeDtypeStruct((nt, N_EXPERTS, V7X_LANES), jnp.int32)),
        grid=(nt,),
        in_specs=[pl.BlockSpec((tm, d), lambda i: (i, 0)),
                  _const_spec((32, d)), _const_spec((32, 1)),
                  _const_spec((tm, tm)), _const_spec((N_EXPERTS, N_EXPERTS))],
        out_specs=(pl.BlockSpec((8, tm), lambda i: (0, i)),
                   pl.BlockSpec((1, N_EXPERTS, V7X_LANES), lambda i: (i, 0, 0))),
        compiler_params=_cparams(("arbitrary",)),
        name="route",
    )(h2, wr_t, br_t, upper, l16)


SEG = 16
GATHER_ROWS = 256
MAX_EXPERT_ROWS = 512


def _expert_rows(n_tokens):
    mean_rows = 2 * n_tokens // N_EXPERTS
    return int(min(MAX_EXPERT_ROWS, max(4 * SEG, 1 << (mean_rows // 4).bit_length())))


def _segment_copies(cnt_ref, loff_ref, goff_ref, tile, local_ref, global_ref, sem, to_global):
    for e in range(N_EXPERTS):
        cnt = cnt_ref[tile * N_EXPERTS + e]
        rows = pl.multiple_of(cnt * SEG, SEG)
        loc = local_ref.at[pl.ds(pl.multiple_of(loff_ref[tile * N_EXPERTS + e] * SEG, SEG), rows)]
        glo = global_ref.at[pl.ds(pl.multiple_of(goff_ref[tile * N_EXPERTS + e] * SEG, SEG), rows)]
        cp = pltpu.make_async_copy(loc, glo, sem) if to_global else pltpu.make_async_copy(glo, loc, sem)
        yield cnt > 0, cp


def _tile_units(cnt_ref, tile):
    total = jnp.int32(0)
    for e in range(N_EXPERTS):
        total = total + cnt_ref[tile * N_EXPERTS + e]
    return total


def _wait_segments(cnt_ref, tile, local_ref, global_ref, sem, to_global):
    rows = pl.multiple_of(_tile_units(cnt_ref, tile) * SEG, SEG)
    loc = local_ref.at[pl.ds(0, rows)]
    glo = global_ref.at[pl.ds(0, rows)]
    cp = pltpu.make_async_copy(loc, glo, sem) if to_global else pltpu.make_async_copy(glo, loc, sem)
    pl.when(rows > 0)(cp.wait)


def _sort_kernel(cnt_ref, loff_ref, goff_ref, h_ref, irow_ref, xin_ref, xout_ref, xs2_ref, sem2):
    del xin_ref
    i = pl.program_id(0)
    last = pl.num_programs(0) - 1
    slot = lax.rem(i, 2)
    xs_ref = xs2_ref.at[slot]
    sem = sem2.at[slot]

    def wait_tile(tile, s):
        _wait_segments(cnt_ref, tile, xs2_ref.at[s], xout_ref, sem2.at[s], True)

    @pl.when(i >= 2)
    def _():
        wait_tile(i - 2, slot)

    gr = GATHER_ROWS
    n_gather = (_tile_units(cnt_ref, i) * SEG + gr - 1) // gr
    pos1 = irow_ref[0:1, :]
    pos2 = irow_ref[1:2, :]
    h = h_ref[...]

    def gather(gb, carry):
        r0 = pl.multiple_of(gb * gr, gr)
        srow = (lax.broadcasted_iota(jnp.int32, (gr, 1), 0) + r0).astype(F32)
        sel = jnp.where((srow == pos1) | (srow == pos2), 1.0, 0.0).astype(BF16)
        xs_ref[pl.ds(r0, gr), :] = _dot(sel, h).astype(BF16)
        return carry

    lax.fori_loop(0, n_gather, gather, 0)
    for pred, cp in _segment_copies(cnt_ref, loff_ref, goff_ref, i, xs_ref, xout_ref, sem, True):
        pl.when(pred)(cp.start)

    @pl.when(i == last)
    def _():
        @pl.when(i >= 1)
        def _():
            wait_tile(i - 1, 1 - slot)

        wait_tile(i, slot)


def _experts_kernel(be_ref, nv_ref, x_ref, wg_ref, wu_ref, wd_ref, o_ref, a_ref):
    del be_ref
    b = pl.program_id(0)
    n_valid = nv_ref[0]
    slot = lax.rem(b, 2)

    def up(s):
        xb = x_ref[...]
        a = _silu(_dot(xb, wg_ref[0].astype(BF16))) * _dot(xb, wu_ref[0].astype(BF16))
        a_ref[s] = a.astype(BF16)

    def down(s):
        o_ref[...] = _dot(a_ref[s], wd_ref[0].astype(BF16)).astype(BF16)

    @pl.when(b == 0)
    def _():
        up(slot)

    @pl.when((b >= 1) & (b < n_valid))
    def _():
        down(1 - slot)
        up(slot)

    @pl.when((b >= 1) & (b == n_valid))
    def _():
        down(1 - slot)

    @pl.when(b > n_valid)
    def _():
        o_ref[...] = jnp.zeros(o_ref.shape, BF16)


def _combine_kernel(cnt_ref, loff_ref, goff_ref, ds_ref, icol_ref, x_ref, gt_ref, o_ref, dl2_ref, sem2, *, rb):
    i = pl.program_id(0)
    tm = x_ref.shape[1]
    s_loc = dl2_ref.shape[1]
    slot = lax.rem(i, 2)

    def fetch_tile(tile, s):
        for pred, cp in _segment_copies(cnt_ref, loff_ref, goff_ref, tile, dl2_ref.at[s], ds_ref, sem2.at[s], False):
            pl.when(pred)(cp.start)

        def zero_tail(u, carry):
            dl2_ref[s, pl.ds(pl.multiple_of(u * SEG, SEG), SEG), :] = jnp.zeros((SEG, dl2_ref.shape[2]), BF16)
            return carry

        lax.fori_loop(_tile_units(cnt_ref, tile), s_loc // SEG, zero_tail, 0)

    @pl.when(i == 0)
    def _():
        fetch_tile(0, 0)

    _wait_segments(cnt_ref, i, dl2_ref.at[slot], ds_ref, sem2.at[slot], False)
    scol = lax.broadcasted_iota(jnp.int32, (1, s_loc), 1).astype(F32)
    for r in range(tm // rb):
        rows = slice(r * rb, (r + 1) * rb)
        cmb = (jnp.where(scol == icol_ref[rows, 0:1], icol_ref[rows, 2:3], 0.0)
               + jnp.where(scol == icol_ref[rows, 1:2], icol_ref[rows, 3:4], 0.0))
        y = _dot(cmb.astype(BF16), dl2_ref[slot])
        gt = gt_ref[0] if gt_ref.shape[1] == 1 else gt_ref[0, rows, :]
        o_ref[0, rows, :] = x_ref[0, rows, :] + gt * y
        if r == 0:
            @pl.when(i + 1 < pl.num_programs(0))
            def _():
                fetch_tile(i + 1, 1 - slot)


def _sorted_rows(n_tokens, tm):
    n_tiles = n_tokens // tm
    er = _expert_rows(n_tokens)
    rows = 2 * n_tokens + n_tiles * N_EXPERTS * (SEG - 1) + N_EXPERTS * (er - 1)
    return -(-rows // er) * er


def _moe(cnt, h2, x, gt, info_row, info_col, w_eg, w_eu, w_ed, x_sorted, tm, layer):
    g, t, d = x.shape
    n = g * t
    n_tiles = n // tm
    tpg = t // tm
    rm = gt.shape[1]
    er = _expert_rows(n)
    n_blocks = x_sorted.shape[0] // er
    s_loc = -(-(2 * tm + N_EXPERTS * (SEG - 1)) // GATHER_ROWS) * GATHER_ROWS
    upb = er // SEG
    reg_blk = (jnp.sum(cnt, axis=0) + upb - 1) // upb
    blk_end = jnp.cumsum(reg_blk)
    goff = ((blk_end - reg_blk)[None, :] * upb + jnp.cumsum(cnt, axis=0) - cnt).reshape(-1)
    loff = (jnp.cumsum(cnt, axis=1) - cnt).reshape(-1)
    cntf = cnt.reshape(-1)
    n_valid = blk_end[-1:]
    blk_expert = jnp.minimum(
        jnp.sum(jnp.arange(n_blocks, dtype=jnp.int32)[:, None] >= blk_end[None, :], axis=1), N_EXPERTS - 1
    ).astype(jnp.int32)

    x_sorted = pl.pallas_call(
        _sort_kernel,
        out_shape=jax.ShapeDtypeStruct(x_sorted.shape, BF16),
        grid_spec=pltpu.PrefetchScalarGridSpec(
            num_scalar_prefetch=3,
            grid=(n_tiles,),
            in_specs=[pl.BlockSpec((tm, d), lambda i, *_: (i, 0)),
                      pl.BlockSpec((8, tm), lambda i, *_: (0, i)),
                      pl.BlockSpec(memory_space=pl.ANY)],
            out_specs=pl.BlockSpec(memory_space=pl.ANY),
            scratch_shapes=[pltpu.VMEM((2, s_loc, d), BF16), pltpu.SemaphoreType.DMA((2,))],
        ),
        input_output_aliases={5: 0},
        compiler_params=_cparams(("arbitrary",)),
        name="moe_sort",
    )(cntf, loff, goff, h2, info_row, x_sorted)

    blk = lambda b, be, nv: (jnp.minimum(b, nv[0] - 1), 0)
    last = n_blocks - 1
    d_sorted = pl.pallas_call(
        _experts_kernel,
        out_shape=jax.ShapeDtypeStruct(x_sorted.shape, BF16),
        grid_spec=pltpu.PrefetchScalarGridSpec(
            num_scalar_prefetch=2,
            grid=(n_blocks + 1,),
            in_specs=[pl.BlockSpec((er, d), blk),
                      pl.BlockSpec((None, 1, d, D_EXPERT), lambda b, be, nv: (layer, be[jnp.minimum(b, last)], 0, 0)),
                      pl.BlockSpec((None, 1, d, D_EXPERT), lambda b, be, nv: (layer, be[jnp.minimum(b, last)], 0, 0)),
                      pl.BlockSpec((None, 1, D_EXPERT, d), lambda b, be, nv: (layer, be[jnp.maximum(b - 1, 0)], 0, 0))],
            out_specs=pl.BlockSpec((er, d), lambda b, be, nv: (jnp.maximum(b - 1, 0), 0)),
            scratch_shapes=[pltpu.VMEM((2, er, D_EXPERT), BF16)],
        ),
        compiler_params=_cparams(("arbitrary",), VMEM_LIMIT),
        name="moe_experts",
    )(blk_expert, n_valid, x_sorted, w_eg, w_eu, w_ed)

    gt_spec = pl.BlockSpec((1, rm, d), (lambda i, *_: (i // tpg, 0, 0)) if rm == 1
                           else (lambda i, *_: (i // tpg, i % tpg, 0)))
    tok3 = pl.BlockSpec((1, tm, d), lambda i, *_: (i // tpg, i % tpg, 0))
    out = pl.pallas_call(
        functools.partial(_combine_kernel, rb=min(tm, 256)),
        out_shape=jax.ShapeDtypeStruct((g, t, d), F32),
        grid_spec=pltpu.PrefetchScalarGridSpec(
            num_scalar_prefetch=3,
            grid=(n_tiles,),
            in_specs=[pl.BlockSpec(memory_space=pl.ANY),
                      pl.BlockSpec((tm, 8), lambda i, *_: (i, 0)),
                      tok3, gt_spec],
            out_specs=tok3,
            scratch_shapes=[pltpu.VMEM((2, s_loc, d), BF16), pltpu.SemaphoreType.DMA((2,))],
        ),
        compiler_params=_cparams(("arbitrary",), VMEM_LIMIT),
        name="moe_combine",
    )(cntf, loff, goff, d_sorted, info_col, x, gt)
    return out, x_sorted


def _rope_tables(pos):
    half = HEAD_DK // 2
    inv_freq = ROPE_BASE ** (-jnp.arange(half, dtype=F32) / half)
    ang = pos[:, None] * inv_freq[None, :]
    cos, sin = jnp.cos(ang), jnp.sin(ang)
    return jnp.concatenate([cos, cos], axis=1), jnp.concatenate([-sin, sin], axis=1)


def _layer(x, mods, lw, cfg, ret0, gla0, cache_k, cache_v, cosf, sinf, x_sorted):
    g, t, d = x.shape
    b, tseq = cfg["b"], cfg["tseq"]
    sh_m, sc_m, gt_m, sh_f, sc_f, gt_f = mods
    proj, log_a = _in_proj(x, sc_m, sh_m, lw["g_mix"], lw["w_in"], lw["w_ga"], lw["a2"], lw["a_bias"],
                           cosf, sinf, lw["gq"], lw["gk"], cfg["tm"], lw["layer"])
    proj_s = proj.reshape(b, tseq, PROJ_COLS)
    log_a_s = log_a.reshape(b, tseq, BRANCH_WIDTH)
    o_r, ret_new = _retention(proj_s, ret0, lw["g_ret_gn"], cfg["chunk"], cfg["tc"])
    o_g, gla_new = _gla(proj_s, log_a_s, gla0, lw["g_gla_gn"], cfg["chunk"], cfg["tc"])
    if cache_k is None:
        o_a = _band_attn_prompt(proj_s, lw["rel_bias"])
    else:
        o_a = _band_attn_sample(proj_s, cache_k, cache_v, lw["rel_bias"], lw["layer"])
    tok = lambda a: a.reshape(g, t, BRANCH_WIDTH)
    x1, h2 = _merge(x, tok(o_r), tok(o_g), tok(o_a), (sc_m, sh_m, gt_m, sc_f, sh_f), lw["g_mix"], lw["g_ffn"],
                    lw["w_gate"], lw["b_gate"], lw["w_branch"], lw["w_out"], cfg["tm"], lw["layer"])
    h2f = h2.reshape(g * t, d)
    info_row, tbl = _route(h2f, lw["wr_t"], lw["br_t"], cfg["tme"], SEG)
    x2, x_sorted = _moe(tbl[:, :, 0], h2f, x1, gt_f, info_row, info_row.T, lw["w_eg"], lw["w_eu"], lw["w_ed"],
                        x_sorted, cfg["tme"], lw["layer"])
    tail = proj_s[:, -min(tseq, BAND_PAST):]
    k_new = tail[:, :, 9 * BRANCH_WIDTH:10 * BRANCH_WIDTH]
    v_new = tail[:, :, 10 * BRANCH_WIDTH:11 * BRANCH_WIDTH]
    return x2, ret_new, gla_new, k_new, v_new, x_sorted


def kernel(x_prompt, x_sample, state_ret, state_gla, cache_att_k, cache_att_v, c_prompt, c_sample,
           w_ada, b_ada, g_mix, w_in, gla_a2, gla_a_bias, g_ret_gn, g_gla_gn, g_q_att, g_k_att, rel_bias,
           w_branch, w_gate, b_gate, w_out, g_ffn, w_router_group, b_router_group, w_router_exp,
           b_router_exp, w_exp_gate, w_exp_up, w_exp_down):
    depth = w_ada.shape[0]
    bp, seq, d = x_prompt.shape
    bs, dseq, _ = x_sample.shape
    n_s = bs * dseq

    pad = (-(bp + bs)) % 8
    c_all = jnp.concatenate([c_prompt, c_sample, jnp.zeros((pad, d), F32)], axis=0)
    mod = _ada_mod(c_all, w_ada, b_ada)

    ga0 = 8 * BRANCH_WIDTH
    w_in_b = jnp.concatenate([w_in[:, :, :ga0], w_in[:, :, ga0 + GLA_RANK:]], axis=2).astype(BF16)
    w_ga = jnp.pad(w_in[:, :, ga0:ga0 + GLA_RANK], ((0, 0), (0, 0), (0, V7X_LANES - GLA_RANK))).astype(BF16)
    a2 = jnp.pad(gla_a2, ((0, 0), (0, V7X_LANES - GLA_RANK), (0, 0))).astype(BF16)
    wr_t = jnp.zeros((depth, 32, d), F32)
    wr_t = wr_t.at[:, 0:N_GROUPS].set(jnp.swapaxes(w_router_group, 1, 2))
    wr_t = wr_t.at[:, 8:8 + N_EXPERTS].set(jnp.swapaxes(w_router_exp, 1, 2)).astype(BF16)
    br_t = jnp.zeros((depth, 32, 1), F32)
    br_t = br_t.at[:, 0:N_GROUPS, 0].set(b_router_group).at[:, 8:8 + N_EXPERTS, 0].set(b_router_exp)
    w_gate_b, w_branch_b, w_out_b = w_gate.astype(BF16), w_branch.astype(BF16), w_out.astype(BF16)
    w_eg, w_eu, w_ed = w_exp_gate, w_exp_up, w_exp_down

    cos_p, sin_p = _rope_tables(jnp.arange(seq, dtype=F32))
    cos_s, sin_s = _rope_tables(PAST_LEN + jnp.arange(dseq, dtype=F32))
    cos_s, sin_s = jnp.tile(cos_s, (bs, 1)), jnp.tile(sin_s, (bs, 1))

    cfg_p = dict(b=bp, tseq=seq, tm=512, chunk=4 * CHUNK, tc=512, tme=512)
    cfg_s = dict(b=bs, tseq=dseq, tm=n_s, chunk=min(dseq, CHUNK), tc=dseq, tme=n_s)
    zero_state = jnp.zeros((bp, RET_HEADS, HEAD_DK, HEAD_DK), F32)
    sorted_p = jnp.zeros((_sorted_rows(bp * seq, cfg_p["tme"]), d), BF16)
    sorted_s = jnp.zeros((_sorted_rows(n_s, cfg_s["tme"]), d), BF16)

    cache_kt = jnp.transpose(cache_att_k, (0, 1, 3, 4, 2))
    cache_vt = jnp.transpose(cache_att_v, (0, 1, 3, 4, 2))

    xp = x_prompt
    xs = x_sample.reshape(1, n_s, d)
    outs = [[] for _ in range(8)]
    for l in range(depth):
        lw = dict(
            layer=l, g_mix=g_mix[l][None], g_ffn=g_ffn[l][None], w_in=w_in_b, w_ga=w_ga[l], a2=a2[l],
            a_bias=gla_a_bias[l][None], gq=jnp.tile(g_q_att[l], ATT_HEADS)[None],
            gk=jnp.tile(g_k_att[l], ATT_HEADS)[None], g_ret_gn=g_ret_gn[l][None], g_gla_gn=g_gla_gn[l][None],
            rel_bias=rel_bias[l], w_gate=w_gate_b, b_gate=b_gate[l][None], w_branch=w_branch_b,
            w_out=w_out_b, wr_t=wr_t[l], br_t=br_t[l], w_eg=w_eg, w_eu=w_eu, w_ed=w_ed)
        mods_p = tuple(m[:, None, :] for m in jnp.split(mod[l, :bp], 6, axis=-1))
        mods_s = tuple(jnp.repeat(m, dseq, axis=0)[None] for m in jnp.split(mod[l, bp:bp + bs], 6, axis=-1))
        xp, rp, gp, kp, vp, sorted_p = _layer(xp, mods_p, lw, cfg_p, zero_state, zero_state, None, None,
                                              cos_p, sin_p, sorted_p)
        xs, rs, gs, ks, vs, sorted_s = _layer(xs, mods_s, lw, cfg_s, state_ret[l], state_gla[l], cache_kt, cache_vt,
                                              cos_s, sin_s, sorted_s)
        heads = lambda a: a.astype(F32).reshape(a.shape[0], a.shape[1], ATT_HEADS, ATT_HD)
        for lst, val in zip(outs, (rp, gp, heads(kp), heads(vp), rs, gs, heads(ks), heads(vs))):
            lst.append(val)
    return (xp, xs.reshape(bs, dseq, d)) + tuple(jnp.stack(o) for o in outs)
```
